```python
import math
import jax, jax.numpy as jnp
from jax import lax
import numpy as np

D_MODEL = 1024
BATCH = 4
SEQ = 8192
DEPTH = 1

D_MIX = D_MODEL
CONV_WIDTH = D_MIX // 2
CONV_GROUPS = 8
CONV_K = 3
DN_WIDTH = D_MIX - CONV_WIDTH
DN_HEADS = 4
DN_HEAD_DIM = DN_WIDTH // DN_HEADS
DN_CONV_K = 4
DN_CHUNK = 64
IN_COLS = 3 * CONV_WIDTH + 4 * DN_WIDTH + 2 * DN_HEADS
MEM_LEN = 256
X_HEADS = 4
X_HEAD_DIM = D_MODEL // X_HEADS
PEER_HEADS = 8
PEER_KEYS = 128
PEER_EXPERTS = PEER_KEYS * PEER_KEYS
PEER_TOPK = 16
PEER_QDIM = 128
PEER_HALF = PEER_QDIM // 2
PEER_BLOCK = 128
EPS = 1e-6

kernel_name = "hybrid_conv_deltanet_peer_layer"


def rms_norm(x, gain, eps=EPS):
    xf = x.astype(jnp.float32)
    y = xf * lax.rsqrt(jnp.mean(xf * xf, axis=-1, keepdims=True) + eps)
    return (y * gain.astype(jnp.float32)).astype(x.dtype)


def l2_normalize(t, eps=EPS):
    tf = t.astype(jnp.float32)
    return tf * lax.rsqrt(jnp.sum(tf * tf, axis=-1, keepdims=True) + eps)


def causal_depthwise_conv(x, w):
    k = w.shape[0]
    return lax.conv_general_dilated(
        x, w[:, None, :].astype(x.dtype), window_strides=(1,), padding=[(k - 1, 0)],
        dimension_numbers=("NWC", "WIO", "NWC"), feature_group_count=x.shape[-1])


def chunk_gated_delta_rule(q, k, v, g, beta):
    f32 = jnp.float32
    bsz, seq, nh, dk = q.shape
    dv = v.shape[-1]
    nc = seq // DN_CHUNK

    def to_chunks(t):
        t = t.astype(f32).reshape(bsz, nc, DN_CHUNK, nh, *t.shape[3:])
        return jnp.moveaxis(t, 3, 1)

    q = to_chunks(q) * (dk ** -0.5)
    k = to_chunks(k)
    v = to_chunks(v)
    beta = to_chunks(beta)
    g = jnp.cumsum(to_chunks(g), axis=-1)

    causal = jnp.tril(jnp.ones((DN_CHUNK, DN_CHUNK), dtype=bool))
    strict = jnp.tril(jnp.ones((DN_CHUNK, DN_CHUNK), dtype=bool), -1)
    diff = g[..., :, None] - g[..., None, :]
    decay = jnp.where(causal, jnp.exp(jnp.where(causal, diff, 0.0)), 0.0)

    k_beta = k * beta[..., None]
    v_beta = v * beta[..., None]
    a = jnp.where(strict, jnp.einsum("bhnid,bhnjd->bhnij", k_beta, k) * decay, 0.0)
    eye = jnp.eye(DN_CHUNK, dtype=f32)
    t_mat = lax.linalg.triangular_solve(a + eye, jnp.broadcast_to(eye, a.shape),
                                        left_side=True, lower=True, unit_diagonal=True)
    u = jnp.einsum("bhnij,bhnjd->bhnid", t_mat, v_beta)
    w = jnp.einsum("bhnij,bhnjd->bhnid", t_mat, k_beta * jnp.exp(g)[..., None])
    intra = jnp.einsum("bhnid,bhnjd->bhnij", q, k) * decay
    q_dec = q * jnp.exp(g)[..., None]
    k_dec = k * jnp.exp(g[..., -1:] - g)[..., None]
    g_last = jnp.exp(g[..., -1])

    xs = tuple(jnp.moveaxis(t, 2, 0) for t in (q_dec, k_dec, u, w, intra, g_last))

    def step(state, inp):
        q_c, k_c, u_c, w_c, a_c, gl = inp
        v_new = u_c - jnp.einsum("bhcd,bhde->bhce", w_c, state)
        o = jnp.einsum("bhcd,bhde->bhce", q_c, state) + jnp.einsum("bhij,bhje->bhie", a_c, v_new)
        state = state * gl[..., None, None] + jnp.einsum("bhcd,bhce->bhde", k_c, v_new)
        return state, o

    s0 = jnp.zeros((bsz, nh, dk, dv), f32)
    _, o = lax.scan(step, s0, xs)
    return o.transpose(1, 0, 3, 2, 4).reshape(bsz, seq, nh, dv)


def hybrid_mixer(xn, w_in, conv_w, dn_conv_w, dn_a_log, dn_dt_bias, dn_norm, w_out):
    bsz, seq, _ = xn.shape
    proj = xn @ w_in
    o1 = CONV_WIDTH
    o2 = 2 * CONV_WIDTH
    o3 = 3 * CONV_WIDTH
    o4 = o3 + 3 * DN_WIDTH
    o5 = o4 + DN_WIDTH
    o6 = o5 + DN_HEADS
    b_gate, c_gate, h_conv, qkv, z, beta_logit, a_logit = jnp.split(
        proj, [o1, o2, o3, o4, o5, o6], axis=-1)

    y_conv = b_gate * causal_depthwise_conv(c_gate * h_conv, conv_w)

    qkv = jax.nn.silu(causal_depthwise_conv(qkv, dn_conv_w))
    q, k, v = jnp.split(qkv, 3, axis=-1)
    hs = (bsz, seq, DN_HEADS, DN_HEAD_DIM)
    q = l2_normalize(q.reshape(hs))
    k = l2_normalize(k.reshape(hs))
    v = v.reshape(hs)
    beta = jax.nn.sigmoid(beta_logit.astype(jnp.float32))
    g = -jnp.exp(dn_a_log.astype(jnp.float32)) * jax.nn.softplus(
        a_logit.astype(jnp.float32) + dn_dt_bias.astype(jnp.float32))
    o = chunk_gated_delta_rule(q, k, v, g, beta)
    o = rms_norm(o, dn_norm) * jax.nn.silu(z.reshape(hs).astype(jnp.float32))
    y_dn = o.reshape(bsz, seq, DN_WIDTH).astype(xn.dtype)

    return jnp.concatenate([y_conv, y_dn], axis=-1) @ w_out


def memory_cross_attention(xn, mem_n, w_q, w_kv, w_o):
    bsz, seq, _ = xn.shape
    q = (xn @ w_q).reshape(bsz, seq, X_HEADS, X_HEAD_DIM)
    kv = (mem_n @ w_kv).reshape(bsz, mem_n.shape[1], 2, X_HEADS, X_HEAD_DIM)
    k, v = kv[:, :, 0], kv[:, :, 1]
    s = jnp.einsum("bshd,bmhd->bhsm", q, k).astype(jnp.float32) * (X_HEAD_DIM ** -0.5)
    p = jax.nn.softmax(s, axis=-1).astype(xn.dtype)
    o = jnp.einsum("bhsm,bmhd->bshd", p, v).reshape(bsz, seq, D_MODEL)
    return o @ w_o


def peer_ffn(xn, w_query, sub_keys, expert_u, expert_v):
    bsz, seq, d = xn.shape
    q = (xn @ w_query).reshape(bsz, seq, PEER_HEADS, 2, PEER_HALF)
    scores = jnp.einsum("bshpd,hpnd->bshpn", q, sub_keys).astype(jnp.float32)
    s_top, i_top = lax.top_k(scores, PEER_TOPK)
    cand = (s_top[..., 0, :, None] + s_top[..., 1, None, :]).reshape(
        bsz, seq, PEER_HEADS, PEER_TOPK * PEER_TOPK)
    cand_idx = (i_top[..., 0, :, None] * PEER_KEYS + i_top[..., 1, None, :]).reshape(
        bsz, seq, PEER_HEADS, PEER_TOPK * PEER_TOPK)
    best, pos = lax.top_k(cand, PEER_TOPK)
    idx = jnp.take_along_axis(cand_idx, pos, axis=-1)
    gates = jax.nn.softmax(best, axis=-1).astype(xn.dtype)

    n_blk = (bsz * seq) // PEER_BLOCK
    sel = PEER_HEADS * PEER_TOPK
    xb = xn.reshape(n_blk, PEER_BLOCK, d)
    ib = idx.reshape(n_blk, PEER_BLOCK, sel)
    gb = gates.reshape(n_blk, PEER_BLOCK, sel)

    def block(args):
        xt, it, gt = args
        u = expert_u[it]
        v = expert_v[it]
        act = jax.nn.gelu(jnp.einsum("td,ted->te", xt, u), approximate=False)
        return jnp.einsum("te,ted->td", gt * act, v)

    y = lax.map(block, (xb, ib, gb))
    return y.reshape(bsz, seq, d)


def setup_inputs(seed: int = 0) -> dict:
    key = jax.random.key(seed)
    ks = iter(jax.random.split(key, 32))
    f32 = jnp.float32
    L = DEPTH

    def nrm(shape, scale):
        return jax.random.normal(next(ks), shape, f32) * scale

    def gain(shape):
        return 1.0 + 0.02 * jax.random.normal(next(ks), shape, f32)

    x = nrm((BATCH, SEQ, D_MODEL), 1.0)
    mem = nrm((BATCH, MEM_LEN, D_MODEL), 1.0)
    norm_mix = gain((L, D_MODEL))
    w_in = nrm((L, D_MODEL, IN_COLS), D_MODEL ** -0.5)
    conv_w = nrm((L, CONV_K, CONV_WIDTH), CONV_K ** -0.5)
    dn_conv_w = nrm((L, DN_CONV_K, 3 * DN_WIDTH), DN_CONV_K ** -0.5)
    dn_a_log = jnp.log(jax.random.uniform(next(ks), (L, DN_HEADS), f32, 1.0, 16.0))
    dt = jnp.exp(jax.random.uniform(next(ks), (L, DN_HEADS), f32,
                                    math.log(1e-3), math.log(1e-1)))
    dn_dt_bias = dt + jnp.log(-jnp.expm1(-dt))
    dn_norm = gain((L, DN_HEAD_DIM))
    w_out = nrm((L, D_MIX, D_MODEL), D_MIX ** -0.5)
    norm_x = gain((L, D_MODEL))
    norm_mem = gain((L, D_MODEL))
    w_xq = nrm((L, D_MODEL, D_MODEL), D_MODEL ** -0.5)
    w_xkv = nrm((L, D_MODEL, 2 * D_MODEL), D_MODEL ** -0.5)
    w_xo = nrm((L, D_MODEL, D_MODEL), D_MODEL ** -0.5)
    norm_ffn = gain((L, D_MODEL))
    w_pq = nrm((L, D_MODEL, PEER_HEADS * PEER_QDIM), D_MODEL ** -0.5)
    peer_keys = nrm((L, PEER_HEADS, 2, PEER_KEYS, PEER_HALF), PEER_HALF ** -0.5)
    expert_u = nrm((L, PEER_EXPERTS, D_MODEL), D_MODEL ** -0.5)
    expert_v = nrm((L, PEER_EXPERTS, D_MODEL), PEER_TOPK ** -0.5)
    norm_final = gain((D_MODEL,))
    return {"x": x, "mem": mem, "norm_mix": norm_mix, "w_in": w_in, "conv_w": conv_w,
            "dn_conv_w": dn_conv_w, "dn_a_log": dn_a_log, "dn_dt_bias": dn_dt_bias,
            "dn_norm": dn_norm, "w_out": w_out, "norm_x": norm_x, "norm_mem": norm_mem,
            "w_xq": w_xq, "w_xkv": w_xkv, "w_xo": w_xo, "norm_ffn": norm_ffn,
            "w_pq": w_pq, "peer_keys": peer_keys, "expert_u": expert_u,
            "expert_v": expert_v, "norm_final": norm_final}


def reference(x, mem, norm_mix, w_in, conv_w, dn_conv_w, dn_a_log, dn_dt_bias, dn_norm,
              w_out, norm_x, norm_mem, w_xq, w_xkv, w_xo, norm_ffn, w_pq, peer_keys,
              expert_u, expert_v, norm_final):
    h = x
    for layer in range(DEPTH):
        h = h + hybrid_mixer(rms_norm(h, norm_mix[layer]), w_in[layer], conv_w[layer],
                             dn_conv_w[layer], dn_a_log[layer], dn_dt_bias[layer],
                             dn_norm[layer], w_out[layer])
        h = h + memory_cross_attention(rms_norm(h, norm_x[layer]),
                                       rms_norm(mem, norm_mem[layer]),
                                       w_xq[layer], w_xkv[layer], w_xo[layer])
        h = h + peer_ffn(rms_norm(h, norm_ffn[layer]), w_pq[layer], peer_keys[layer],
                         expert_u[layer], expert_v[layer])
    return rms_norm(h, norm_final)
```

```python
import functools

import jax
import jax.numpy as jnp
from jax import lax
from jax.experimental import pallas as pl
from jax.experimental.pallas import tpu as pltpu

EPS = 1e-6
F32 = jnp.float32
BF16 = jnp.bfloat16
HIGHEST = lax.Precision.HIGHEST

LANES = 128
SUBLANES = 8
VMEM_LIMIT = 56 * 1024 * 1024

CONV_GROUP_K = 3
DN_CONV_K = 4
DN_HEADS = 4
DN_HEAD_DIM = 128
DN_CHUNK = 64
X_HEADS = 4
PEER_HEADS = 8
PEER_KEYS = 128
PEER_TOPK = 16
PEER_HALF = 64


def _params(*sem):
    return pltpu.CompilerParams(dimension_semantics=sem, vmem_limit_bytes=VMEM_LIMIT)


def _rms(x, gain):
    ms = jnp.mean(x * x, axis=-1, keepdims=True)
    return x * lax.rsqrt(ms + EPS) * gain


def _bdot(a, b):
    return jnp.dot(a.astype(BF16), b.astype(BF16), preferred_element_type=F32)


def _hdot(a, b):
    return jnp.dot(a, b, preferred_element_type=F32, precision=HIGHEST)


def _hdot_nt(a, b):
    return lax.dot_general(a, b, (((1,), (1,)), ((), ())),
                           preferred_element_type=F32, precision=HIGHEST)


def _hdot_tn(a, b):
    return lax.dot_general(a, b, (((0,), (0,)), ((), ())),
                           preferred_element_type=F32, precision=HIGHEST)


def _norm_matmul_body(x_ref, g_ref, w_ref, o_ref):
    xn = _rms(x_ref[...], g_ref[...])
    o_ref[...] = jnp.dot(xn.astype(BF16), w_ref[...], preferred_element_type=F32)


def _norm_matmul(x, gain, w, tm, name):
    t, d = x.shape
    n = w.shape[1]
    return pl.pallas_call(
        _norm_matmul_body,
        grid=(t // tm,),
        in_specs=[pl.BlockSpec((tm, d), lambda i: (i, 0)),
                  pl.BlockSpec((1, d), lambda i: (0, 0)),
                  pl.BlockSpec((d, n), lambda i: (0, 0))],
        out_specs=pl.BlockSpec((tm, n), lambda i: (i, 0)),
        out_shape=jax.ShapeDtypeStruct((t, n), F32),
        compiler_params=_params("parallel"),
        name=name,
    )(x, gain.reshape(1, d), w)


def _shift_rows(x, halo, k):
    rolled = pltpu.roll(x, k, 0)
    hal = pltpu.roll(halo, k, 0)
    row = lax.broadcasted_iota(jnp.int32, (SUBLANES, x.shape[1]), 0)
    head = jnp.where(row < k, hal, rolled[:SUBLANES])
    return jnp.concatenate([head, rolled[SUBLANES:]], axis=0)


def _causal_conv(x, halo, w_ref):
    kk = w_ref.shape[0]
    y = x * w_ref[kk - 1:kk, :]
    for j in range(1, kk):
        y = y + _shift_rows(x, halo, j) * w_ref[kk - 1 - j:kk - j, :]
    return y


def _l2norm_heads(x):
    outs = []
    for h in range(DN_HEADS):
        xh = x[:, h * DN_HEAD_DIM:(h + 1) * DN_HEAD_DIM]
        outs.append(xh * lax.rsqrt(jnp.sum(xh * xh, axis=-1, keepdims=True) + EPS))
    return jnp.concatenate(outs, axis=-1)


def _mixer_pre_body(b_ref, c_ref, h_ref, q_ref, k_ref, v_ref, t_ref,
                    ch_ref, hh_ref, qh_ref, kh_ref, vh_ref,
                    cw_ref, qw_ref, kw_ref, vw_ref, alog_ref, dtb_ref, tril_ref,
                    yc_ref, qo_ref, ko_ref, vo_ref, gb_ref):
    first = pl.program_id(1) == 0
    keep = jnp.where(first, 0.0, 1.0).astype(F32)

    u = c_ref[...] * h_ref[...]
    uh = ch_ref[...] * hh_ref[...] * keep
    yc_ref[...] = b_ref[...] * _causal_conv(u, uh, cw_ref)

    def dn_branch(x_ref, xh_ref, w_ref):
        y = _causal_conv(x_ref[...], xh_ref[...] * keep, w_ref)
        return y * jax.nn.sigmoid(y)

    qo_ref[...] = _l2norm_heads(dn_branch(q_ref, qh_ref, qw_ref))
    ko_ref[...] = _l2norm_heads(dn_branch(k_ref, kh_ref, kw_ref))
    vo_ref[...] = dn_branch(v_ref, vh_ref, vw_ref)

    tail = t_ref[...]
    beta = jax.nn.sigmoid(tail)
    g = -jnp.exp(alog_ref[...]) * jax.nn.softplus(tail + dtb_ref[...])
    gcum = _hdot(tril_ref[...], g)
    lane = lax.broadcasted_iota(jnp.int32, tail.shape, 1)
    gb_ref[...] = jnp.where(lane < DN_HEADS, beta, gcum)


def _mixer_pre(proj, conv_w, dn_conv_w, dn_a_log, dn_dt_bias, batch, seq, ts):
    t = proj.shape[0]
    cw = 512
    nblk = seq // ts
    hb = ts // SUBLANES

    def cur(col):
        return pl.BlockSpec((ts, cw if col < 7 else LANES),
                            lambda b, i, col=col: (b * nblk + i, col))

    def halo(col):
        return pl.BlockSpec(
            (SUBLANES, cw),
            lambda b, i, col=col: (jnp.maximum((b * nblk + i) * hb - 1, 0), col))

    def full(shape):
        return pl.BlockSpec(shape, lambda b, i: (0,) * len(shape))

    qw, kw, vw = (dn_conv_w[:, j * cw:(j + 1) * cw] for j in range(3))
    lane_pad = jnp.zeros((LANES - 2 * DN_HEADS,), F32)
    alog = jnp.concatenate([jnp.zeros((DN_HEADS,), F32), dn_a_log, lane_pad]).reshape(1, LANES)
    dtb = jnp.concatenate([jnp.zeros((DN_HEADS,), F32), dn_dt_bias, lane_pad]).reshape(1, LANES)
    r = jnp.arange(ts)
    tril = ((r[:, None] >= r[None, :]) &
            (r[:, None] // DN_CHUNK == r[None, :] // DN_CHUNK)).astype(F32)

    tail_spec = pl.BlockSpec((ts, LANES), lambda b, i: (b * nblk + i, 7 * cw // LANES))
    out_tok = lambda w: pl.BlockSpec((ts, w), lambda b, i: (b * nblk + i, 0))
    return pl.pallas_call(
        _mixer_pre_body,
        grid=(batch, nblk),
        in_specs=[cur(0), cur(1), cur(2), cur(3), cur(4), cur(5), tail_spec,
                  halo(1), halo(2), halo(3), halo(4), halo(5),
                  full((CONV_GROUP_K, cw)), full((DN_CONV_K, cw)), full((DN_CONV_K, cw)),
                  full((DN_CONV_K, cw)), full((1, LANES)), full((1, LANES)), full((ts, ts))],
        out_specs=[out_tok(cw), out_tok(cw), out_tok(cw), out_tok(cw), out_tok(LANES)],
        out_shape=[jax.ShapeDtypeStruct((t, cw), F32)] * 4 + [jax.ShapeDtypeStruct((t, LANES), F32)],
        compiler_params=_params("parallel", "parallel"),
        name="mixer_pre",
    )(proj, proj, proj, proj, proj, proj, proj, proj, proj, proj, proj, proj,
      conv_w, qw, kw, vw, alog, dtb, tril)


def _delta_rule_body(q_ref, k_ref, v_ref, gb_ref, o_ref, state_ref):
    c = DN_CHUNK

    @pl.when(pl.program_id(1) == 0)
    def _():
        state_ref[...] = jnp.zeros_like(state_ref)

    gb = gb_ref[...]
    eye = (lax.broadcasted_iota(jnp.int32, (c, c), 0) ==
           lax.broadcasted_iota(jnp.int32, (c, c), 1)).astype(F32)
    gb_t = _hdot_tn(gb, eye)
    row = lax.broadcasted_iota(jnp.int32, (c, c), 0)
    col = lax.broadcasted_iota(jnp.int32, (c, c), 1)
    causal = row >= col
    strict = row > col
    scale = DN_HEAD_DIM ** -0.5

    outs = []
    for h in range(DN_HEADS):
        sl = slice(h * DN_HEAD_DIM, (h + 1) * DN_HEAD_DIM)
        q = q_ref[:, sl] * scale
        k = k_ref[:, sl]
        v = v_ref[:, sl]
        beta = gb[:, h:h + 1]
        g_col = gb[:, DN_HEADS + h:DN_HEADS + h + 1]
        g_row = gb_t[DN_HEADS + h:DN_HEADS + h + 1, :]
        g_last = g_col[c - 1:c, :]

        diff = g_col - g_row
        decay = jnp.where(causal, jnp.exp(jnp.where(causal, diff, 0.0)), 0.0)
        k_beta = k * beta
        v_beta = v * beta
        a = jnp.where(strict, _hdot_nt(k_beta, k) * decay, 0.0)

        t_mat = eye - a
        pw = _hdot(a, a)
        for _ in range(4):
            t_mat = t_mat + _hdot(t_mat, pw)
            pw = _hdot(pw, pw)
        t_mat = t_mat + _hdot(t_mat, pw)

        e_g = jnp.exp(g_col)
        u = _hdot(t_mat, v_beta)
        w = _hdot(t_mat, k_beta * e_g)
        intra = _hdot_nt(q, k) * decay
        q_dec = q * e_g
        k_dec = k * jnp.exp(g_last - g_col)

        state = state_ref[h]
        v_new = u - _hdot(w, state)
        outs.append(_hdot(q_dec, state) + _hdot(intra, v_new))
        state_ref[h] = state * jnp.exp(g_last) + _hdot_tn(k_dec, v_new)

    o_ref[...] = jnp.concatenate(outs, axis=-1)


def _delta_rule(q, k, v, gb, batch, seq):
    t, w = q.shape
    nc = seq // DN_CHUNK
    tok = lambda width: pl.BlockSpec((DN_CHUNK, width), lambda b, i: (b * nc + i, 0))
    return pl.pallas_call(
        _delta_rule_body,
        grid=(batch, nc),
        in_specs=[tok(w), tok(w), tok(w), tok(LANES)],
        out_specs=tok(w),
        out_shape=jax.ShapeDtypeStruct((t, w), F32),
        scratch_shapes=[pltpu.VMEM((DN_HEADS, DN_HEAD_DIM, DN_HEAD_DIM), F32)],
        compiler_params=_params("parallel", "arbitrary"),
        name="delta_rule",
    )(q, k, v, gb)


def _mixer_out_body(x_ref, yc_ref, o_ref, z_ref, dng_ref, wa_ref, wb_ref, h_ref):
    o = o_ref[...]
    z = z_ref[...]
    gain = dng_ref[...]
    parts = []
    for h in range(DN_HEADS):
        sl = slice(h * DN_HEAD_DIM, (h + 1) * DN_HEAD_DIM)
        zh = z[:, sl]
        parts.append(_rms(o[:, sl], gain) * (zh * jax.nn.sigmoid(zh)))
    y_dn = jnp.concatenate(parts, axis=-1)
    h_ref[...] = (x_ref[...] + _bdot(yc_ref[...], wa_ref[...]) + _bdot(y_dn, wb_ref[...]))


def _mixer_out(x, y_conv, o_dn, proj, dn_norm, w_out, tm):
    t, d = x.shape
    cw = y_conv.shape[1]
    tok = lambda width, col=0: pl.BlockSpec((tm, width), lambda i, col=col: (i, col))
    full = lambda shape: pl.BlockSpec(shape, lambda i: (0, 0))
    return pl.pallas_call(
        _mixer_out_body,
        grid=(t // tm,),
        in_specs=[tok(d), tok(cw), tok(cw), tok(cw, 6), full((1, DN_HEAD_DIM)),
                  full((cw, d)), full((cw, d))],
        out_specs=tok(d),
        out_shape=jax.ShapeDtypeStruct((t, d), F32),
        compiler_params=_params("parallel"),
        name="mixer_out",
    )(x, y_conv, o_dn, proj, dn_norm.reshape(1, DN_HEAD_DIM),
      w_out[:cw].astype(BF16), w_out[cw:].astype(BF16))


def _cross_attn_body(h_ref, g_ref, wq_ref, kv_ref, wo_ref, o_ref):
    h_in = h_ref[...]
    d = h_in.shape[1]
    dh = d // X_HEADS
    q = _bdot(_rms(h_in, g_ref[...]), wq_ref[...])
    outs = []
    for hd in range(X_HEADS):
        qh = q[:, hd * dh:(hd + 1) * dh].astype(BF16)
        kh = kv_ref[:, hd * dh:(hd + 1) * dh]
        vh = kv_ref[:, d + hd * dh:d + (hd + 1) * dh]
        s = lax.dot_general(qh, kh, (((1,), (1,)), ((), ())),
                            preferred_element_type=F32) * (dh ** -0.5)
        s = s - jnp.max(s, axis=-1, keepdims=True)
        e = jnp.exp(s)
        p = e / jnp.sum(e, axis=-1, keepdims=True)
        outs.append(jnp.dot(p.astype(BF16), vh, preferred_element_type=F32))
    o = jnp.concatenate(outs, axis=-1)
    o_ref[...] = h_in + _bdot(o, wo_ref[...])


def _cross_attn(h, gain, w_q, kv, w_o, batch, seq, mem_len, tm):
    t, d = h.shape
    nblk = seq // tm
    tok = pl.BlockSpec((tm, d), lambda b, i: (b * nblk + i, 0))
    full = lambda shape: pl.BlockSpec(shape, lambda b, i: (0, 0))
    return pl.pallas_call(
        _cross_attn_body,
        grid=(batch, nblk),
        in_specs=[tok, full((1, d)), full((d, d)),
                  pl.BlockSpec((mem_len, 2 * d), lambda b, i: (b, 0)), full((d, d))],
        out_specs=tok,
        out_shape=jax.ShapeDtypeStruct((t, d), F32),
        compiler_params=_params("parallel", "parallel"),
        name="cross_attn",
    )(h, gain.reshape(1, d), w_q.astype(BF16), kv.astype(BF16), w_o.astype(BF16))


def _top16_rows(s, payload=None):
    n = s.shape[0]
    row = lax.broadcasted_iota(jnp.int32, s.shape, 0).astype(F32)
    vals, pays = [], []
    for _ in range(PEER_TOPK):
        m = jnp.max(s, axis=0, keepdims=True)
        pos = jnp.min(jnp.where(s == m, row, float(n)), axis=0, keepdims=True)
        hit = row == pos
        vals.append(m)
        if payload is None:
            pays.append(pos)
        else:
            pays.append(jnp.sum(jnp.where(hit, payload, 0.0), axis=0, keepdims=True))
        s = jnp.where(hit, -jnp.inf, s)
    return jnp.concatenate(vals, axis=0), jnp.concatenate(pays, axis=0)


def _peer_route_body(h_ref, g_ref, wq_ref, keys_ref, xn_ref, idx_ref, gate_ref,
                     sc_ref, sel_ref, gt_ref):
    xn = _rms(h_ref[...], g_ref[...])
    xn_ref[...] = xn
    pq = _bdot(xn, wq_ref[...])
    ngrp = pq.shape[0] // LANES
    for j in range(2 * PEER_HEADS):
        qs = pq[:, j * PEER_HALF:(j + 1) * PEER_HALF].astype(BF16)
        sc = lax.dot_general(keys_ref[j], qs, (((1,), (1,)), ((), ())),
                             preferred_element_type=F32)
        for g in range(ngrp):
            sc_ref[j, g] = sc[:, g * LANES:(g + 1) * LANES]

    def head_group(i, carry):
        h = i // ngrp
        g = i - h * ngrp
        s1, i1 = _top16_rows(sc_ref[2 * h, g])
        s2, i2 = _top16_rows(sc_ref[2 * h + 1, g])
        cand = jnp.concatenate([s1[a:a + 1] + s2 for a in range(PEER_TOPK)], axis=0)
        cand_idx = jnp.concatenate(
            [i1[a:a + 1] * float(PEER_KEYS) + i2 for a in range(PEER_TOPK)], axis=0)
        best, sel = _top16_rows(cand, cand_idx)
        e = jnp.exp(best - jnp.max(best, axis=0, keepdims=True))
        sel_ref[g, h] = sel
        gt_ref[g, h] = e / jnp.sum(e, axis=0, keepdims=True)
        return carry

    lax.fori_loop(0, PEER_HEADS * ngrp, head_group, 0)

    nsel = PEER_HEADS * PEER_TOPK
    for g in range(ngrp):
        rows = slice(g * LANES, (g + 1) * LANES)
        idx_ref[rows, :] = (sel_ref[g].reshape(nsel, LANES).T * float(PACK_ROWS)).astype(jnp.int32)
        gate_ref[rows, :] = gt_ref[g].reshape(nsel, LANES).T


def _peer_route(h, gain, w_pq, peer_keys, tm):
    t, d = h.shape
    nk = PEER_HEADS * 2
    ngrp = tm // LANES
    tok = lambda width: pl.BlockSpec((tm, width), lambda i: (i, 0))
    full = lambda shape: pl.BlockSpec(shape, lambda i: (0,) * len(shape))
    return pl.pallas_call(
        _peer_route_body,
        grid=(t // tm,),
        in_specs=[tok(d), full((1, d)), full((d, d)), full((nk, PEER_KEYS, PEER_HALF))],
        out_specs=[tok(d), tok(LANES), tok(LANES)],
        out_shape=[jax.ShapeDtypeStruct((t, d), F32),
                   jax.ShapeDtypeStruct((t, LANES), jnp.int32),
                   jax.ShapeDtypeStruct((t, LANES), F32)],
        scratch_shapes=[pltpu.VMEM((nk, ngrp, PEER_KEYS, LANES), F32),
                        pltpu.VMEM((ngrp, PEER_HEADS, PEER_TOPK, LANES), F32),
                        pltpu.VMEM((ngrp, PEER_HEADS, PEER_TOPK, LANES), F32)],
        compiler_params=_params("parallel"),
        name="peer_route",
    )(h, gain.reshape(1, d), w_pq.astype(BF16),
      peer_keys.reshape(nk, PEER_KEYS, PEER_HALF).astype(BF16))


PACK_ROWS = 4


def _pack_table(tbl):
    e, d = tbl.shape
    half = d // 2
    b = tbl.astype(BF16)
    lo = lax.bitcast_convert_type(b[:, :half], jnp.uint16).astype(jnp.uint32)
    hi = lax.bitcast_convert_type(b[:, half:], jnp.uint16).astype(jnp.uint32)
    return ((hi << 16) | lo).reshape(e * PACK_ROWS, LANES)


def _unpack(slab):
    lo = pltpu.bitcast(slab << 16, F32)
    hi = pltpu.bitcast(slab & jnp.uint32(0xFFFF0000), F32)
    return lo, hi


def _peer_dot_body(idx_ref, x_ref, tbl_ref, act_ref, prod_ref):
    tb = x_ref.shape[0]
    nsel = idx_ref.shape[1]
    ones = jnp.ones((SUBLANES, LANES), F32)
    sub = lax.broadcasted_iota(jnp.int32, (SUBLANES, LANES), 0)

    def token(t, acc):
        xt = x_ref[t]
        x_lo = xt[:PACK_ROWS]
        x_hi = xt[PACK_ROWS:]
        for m in range(nsel):
            row = pl.multiple_of(idx_ref[t, m], PACK_ROWS)
            lo, hi = _unpack(tbl_ref[pl.ds(row, PACK_ROWS), :])
            prod_ref[m * PACK_ROWS:(m + 1) * PACK_ROWS, :] = lo * x_lo + hi * x_hi
        part = prod_ref[pl.ds(0, nsel, stride=PACK_ROWS), :]
        for j in range(1, PACK_ROWS):
            part = part + prod_ref[pl.ds(j, nsel, stride=PACK_ROWS), :]
        red = _hdot_nt(ones, part)
        acc = jnp.where(sub == (t & (SUBLANES - 1)), red, acc)

        @pl.when((t & (SUBLANES - 1)) == SUBLANES - 1)
        def _():
            base = pl.multiple_of(t - (SUBLANES - 1), SUBLANES)
            act_ref[pl.ds(base, SUBLANES), :] = acc

        return acc

    lax.fori_loop(0, tb, token, jnp.zeros((SUBLANES, LANES), F32))


def _peer_dot(idx, x3, tbl, tb):
    t, nsel = idx.shape
    return pl.pallas_call(
        _peer_dot_body,
        grid=(t // tb,),
        in_specs=[pl.BlockSpec((tb, nsel), lambda i: (i, 0), memory_space=pltpu.SMEM),
                  pl.BlockSpec((tb, SUBLANES, LANES), lambda i: (i, 0, 0)),
                  pl.BlockSpec(tbl.shape, lambda i: (0, 0), pipeline_mode=pl.Buffered(1))],
        out_specs=pl.BlockSpec((tb, nsel), lambda i: (i, 0)),
        out_shape=jax.ShapeDtypeStruct((t, nsel), F32),
        scratch_shapes=[pltpu.VMEM((nsel * PACK_ROWS, LANES), F32)],
        compiler_params=_params("arbitrary"),
        name="peer_dot",
    )(idx, x3, tbl)


def _peer_weights_body(act_ref, gate_ref, w_ref):
    a = act_ref[...]
    w_ref[...] = gate_ref[...] * (0.5 * a * (1.0 + lax.erf(a * (2.0 ** -0.5))))


def _peer_weights(act, gates, tm):
    t, n = act.shape
    tok = pl.BlockSpec((tm, n), lambda i: (i, 0))
    return pl.pallas_call(
        _peer_weights_body,
        grid=(t // tm,),
        in_specs=[tok, tok],
        out_specs=tok,
        out_shape=jax.ShapeDtypeStruct((t, n), F32),
        compiler_params=_params("parallel"),
        name="peer_weights",
    )(act, gates)


def _peer_sum_body(idx_ref, w_ref, tbl_ref, y_ref):
    tb = y_ref.shape[0]
    nsel = idx_ref.shape[1]

    def token(t, carry):
        acc_lo = jnp.zeros((PACK_ROWS, LANES), F32)
        acc_hi = jnp.zeros((PACK_ROWS, LANES), F32)
        for m in range(nsel):
            row = pl.multiple_of(idx_ref[t, m], PACK_ROWS)
            lo, hi = _unpack(tbl_ref[pl.ds(row, PACK_ROWS), :])
            wt = w_ref[t, m]
            acc_lo = acc_lo + lo * wt
            acc_hi = acc_hi + hi * wt
        y_ref[t] = jnp.concatenate([acc_lo, acc_hi], axis=0)
        return carry

    lax.fori_loop(0, tb, token, 0)


def _peer_sum(idx, w, tbl, tb):
    t, nsel = idx.shape
    smem = lambda: pl.BlockSpec((tb, nsel), lambda i: (i, 0), memory_space=pltpu.SMEM)
    return pl.pallas_call(
        _peer_sum_body,
        grid=(t // tb,),
        in_specs=[smem(), smem(),
                  pl.BlockSpec(tbl.shape, lambda i: (0, 0), pipeline_mode=pl.Buffered(1))],
        out_specs=pl.BlockSpec((tb, SUBLANES, LANES), lambda i: (i, 0, 0)),
        out_shape=jax.ShapeDtypeStruct((t, SUBLANES, LANES), F32),
        compiler_params=_params("arbitrary"),
        name="peer_sum",
    )(idx, w, tbl)


def _final_body(h_ref, y_ref, g_ref, o_ref):
    o_ref[...] = _rms(h_ref[...] + y_ref[...], g_ref[...])


def _final(h, y, gain, tm):
    t, d = h.shape
    tok = pl.BlockSpec((tm, d), lambda i: (i, 0))
    return pl.pallas_call(
        _final_body,
        grid=(t // tm,),
        in_specs=[tok, tok, pl.BlockSpec((1, d), lambda i: (0, 0))],
        out_specs=tok,
        out_shape=jax.ShapeDtypeStruct((t, d), F32),
        compiler_params=_params("parallel"),
        name="final_norm",
    )(h, y, gain.reshape(1, d))


def _pick_tile(n, pref):
    tile = min(n, pref)
    while n % tile:
        tile //= 2
    return tile


def _layer(h, mem, batch, seq, norm_mix, w_in, conv_w, dn_conv_w, dn_a_log, dn_dt_bias, dn_norm,
           w_out, norm_x, norm_mem, w_xq, w_xkv, w_xo, norm_ffn, w_pq, peer_keys,
           expert_u, expert_v):
    t, d = h.shape
    mem_len = mem.shape[0] // batch
    in_cols = w_in.shape[1]
    pad = (-in_cols) % LANES
    w_in_p = jnp.pad(w_in, ((0, 0), (0, pad))).astype(BF16)

    proj = _norm_matmul(h, norm_mix, w_in_p, _pick_tile(t, 256), "in_proj")
    y_conv, q, k, v, gb = _mixer_pre(proj, conv_w, dn_conv_w, dn_a_log, dn_dt_bias,
                                     batch, seq, _pick_tile(seq, 256))
    o_dn = _delta_rule(q, k, v, gb, batch, seq)
    h = _mixer_out(h, y_conv, o_dn, proj, dn_norm, w_out, _pick_tile(t, 512))

    kv = _norm_matmul(mem, norm_mem, w_xkv.astype(BF16), _pick_tile(mem.shape[0], 256), "kv_proj")
    h = _cross_attn(h, norm_x, w_xq, kv, w_xo, batch, seq, mem_len, _pick_tile(seq, 512))

    xn, idx, gates = _peer_route(h, norm_ffn, w_pq, peer_keys, _pick_tile(t, 256))
    tb = _pick_tile(t, 32)
    x3 = xn.reshape(t, SUBLANES, LANES)
    act = _peer_dot(idx, x3, _pack_table(expert_u), tb)
    w = _peer_weights(act, gates, _pick_tile(t, 1024))
    y = _peer_sum(idx, w, _pack_table(expert_v), tb)
    return h, y.reshape(t, d)


def kernel(x, mem, norm_mix, w_in, conv_w, dn_conv_w, dn_a_log, dn_dt_bias, dn_norm, w_out,
           norm_x, norm_mem, w_xq, w_xkv, w_xo, norm_ffn, w_pq, peer_keys, expert_u, expert_v,
           norm_final):
    batch, seq, d = x.shape
    depth = norm_mix.shape[0]
    h = x.reshape(batch * seq, d)
    mem2 = mem.reshape(batch * mem.shape[1], d)
    y = jnp.zeros_like(h)
    for layer in range(depth):
        if layer:
            h = h + y
        h, y = _layer(h, mem2, batch, seq, norm_mix[layer], w_in[layer], conv_w[layer],
                      dn_conv_w[layer], dn_a_log[layer], dn_dt_bias[layer], dn_norm[layer],
                      w_out[layer], norm_x[layer], norm_mem[layer], w_xq[layer], w_xkv[layer],
                      w_xo[layer], norm_ffn[layer], w_pq[layer], peer_keys[layer],
                      expert_u[layer], expert_v[layer])
    out = _final(h, y, norm_final, _pick_tile(batch * seq, 512))
    return out.reshape(batch, seq, d)
```

```python
import functools

import jax
import jax.numpy as jnp
from jax import lax
from jax.experimental import pallas as pl
from jax.experimental.pallas import tpu as pltpu

EPS = 1e-6
F32 = jnp.float32
BF16 = jnp.bfloat16
HIGHEST = lax.Precision.HIGHEST

LANES = 128
SUBLANES = 8
VMEM_LIMIT = 56 * 1024 * 1024

CONV_GROUP_K = 3
DN_CONV_K = 4
DN_HEADS = 4
DN_HEAD_DIM = 128
DN_CHUNK = 64
X_HEADS = 4
PEER_HEADS = 8
PEER_KEYS = 128
PEER_TOPK = 16
PEER_HALF = 64


def _params(*sem):
    return pltpu.CompilerParams(dimension_semantics=sem, vmem_limit_bytes=VMEM_LIMIT)


def _rms(x, gain):
    ms = jnp.mean(x * x, axis=-1, keepdims=True)
    return x * lax.rsqrt(ms + EPS) * gain


def _bdot(a, b):
    return jnp.dot(a.astype(BF16), b.astype(BF16), preferred_element_type=F32)


def _hdot(a, b):
    return jnp.dot(a, b, preferred_element_type=F32, precision=HIGHEST)


def _hdot_nt(a, b):
    return lax.dot_general(a, b, (((1,), (1,)), ((), ())),
                           preferred_element_type=F32, precision=HIGHEST)


def _hdot_tn(a, b):
    return lax.dot_general(a, b, (((0,), (0,)), ((), ())),
                           preferred_element_type=F32, precision=HIGHEST)


def _norm_matmul_body(x_ref, g_ref, w_ref, o_ref):
    xn = _rms(x_ref[...], g_ref[...])
    o_ref[...] = jnp.dot(xn.astype(BF16), w_ref[...], preferred_element_type=F32)


def _norm_matmul(x, gain, w, tm, name):
    t, d = x.shape
    n = w.shape[1]
    return pl.pallas_call(
        _norm_matmul_body,
        grid=(t // tm,),
        in_specs=[pl.BlockSpec((tm, d), lambda i: (i, 0)),
                  pl.BlockSpec((1, d), lambda i: (0, 0)),
                  pl.BlockSpec((d, n), lambda i: (0, 0))],
        out_specs=pl.BlockSpec((tm, n), lambda i: (i, 0)),
        out_shape=jax.ShapeDtypeStruct((t, n), F32),
        compiler_params=_params("parallel"),
        name=name,
    )(x, gain.reshape(1, d), w)


def _shift_rows(x, halo, k):
    rolled = pltpu.roll(x, k, 0)
    hal = pltpu.roll(halo, k, 0)
    row = lax.broadcasted_iota(jnp.int32, (SUBLANES, x.shape[1]), 0)
    head = jnp.where(row < k, hal, rolled[:SUBLANES])
    return jnp.concatenate([head, rolled[SUBLANES:]], axis=0)


def _causal_conv(x, halo, w_ref):
    kk = w_ref.shape[0]
    y = x * w_ref[kk - 1:kk, :]
    for j in range(1, kk):
        y = y + _shift_rows(x, halo, j) * w_ref[kk - 1 - j:kk - j, :]
    return y


def _l2norm_heads(x):
    outs = []
    for h in range(DN_HEADS):
        xh = x[:, h * DN_HEAD_DIM:(h + 1) * DN_HEAD_DIM]
        outs.append(xh * lax.rsqrt(jnp.sum(xh * xh, axis=-1, keepdims=True) + EPS))
    return jnp.concatenate(outs, axis=-1)


def _mixer_pre_body(b_ref, c_ref, h_ref, q_ref, k_ref, v_ref, t_ref,
                    ch_ref, hh_ref, qh_ref, kh_ref, vh_ref,
                    cw_ref, qw_ref, kw_ref, vw_ref, alog_ref, dtb_ref, tril_ref,
                    yc_ref, qo_ref, ko_ref, vo_ref, gb_ref):
    first = pl.program_id(1) == 0
    keep = jnp.where(first, 0.0, 1.0).astype(F32)

    u = c_ref[...] * h_ref[...]
    uh = ch_ref[...] * hh_ref[...] * keep
    yc_ref[...] = b_ref[...] * _causal_conv(u, uh, cw_ref)

    def dn_branch(x_ref, xh_ref, w_ref):
        y = _causal_conv(x_ref[...], xh_ref[...] * keep, w_ref)
        return y * jax.nn.sigmoid(y)

    qo_ref[...] = _l2norm_heads(dn_branch(q_ref, qh_ref, qw_ref))
    ko_ref[...] = _l2norm_heads(dn_branch(k_ref, kh_ref, kw_ref))
    vo_ref[...] = dn_branch(v_ref, vh_ref, vw_ref)

    tail = t_ref[...]
    beta = jax.nn.sigmoid(tail)
    g = -jnp.exp(alog_ref[...]) * jax.nn.softplus(tail + dtb_ref[...])
    gcum = _hdot(tril_ref[...], g)
    lane = lax.broadcasted_iota(jnp.int32, tail.shape, 1)
    gb_ref[...] = jnp.where(lane < DN_HEADS, beta, gcum)


def _mixer_pre(proj, conv_w, dn_conv_w, dn_a_log, dn_dt_bias, batch, seq, ts):
    t = proj.shape[0]
    cw = 512
    nblk = seq // ts
    hb = ts // SUBLANES

    def cur(col):
        return pl.BlockSpec((ts, cw if col < 7 else LANES),
                            lambda b, i, col=col: (b * nblk + i, col))

    def halo(col):
        return pl.BlockSpec(
            (SUBLANES, cw),
            lambda b, i, col=col: (jnp.maximum((b * nblk + i) * hb - 1, 0), col))

    def full(shape):
        return pl.BlockSpec(shape, lambda b, i: (0,) * len(shape))

    qw, kw, vw = (dn_conv_w[:, j * cw:(j + 1) * cw] for j in range(3))
    lane_pad = jnp.zeros((LANES - 2 * DN_HEADS,), F32)
    alog = jnp.concatenate([jnp.zeros((DN_HEADS,), F32), dn_a_log, lane_pad]).reshape(1, LANES)
    dtb = jnp.concatenate([jnp.zeros((DN_HEADS,), F32), dn_dt_bias, lane_pad]).reshape(1, LANES)
    r = jnp.arange(ts)
    tril = ((r[:, None] >= r[None, :]) &
            (r[:, None] // DN_CHUNK == r[None, :] // DN_CHUNK)).astype(F32)

    tail_spec = pl.BlockSpec((ts, LANES), lambda b, i: (b * nblk + i, 7 * cw // LANES))
    out_tok = lambda w: pl.BlockSpec((ts, w), lambda b, i: (b * nblk + i, 0))
    return pl.pallas_call(
        _mixer_pre_body,
        grid=(batch, nblk),
        in_specs=[cur(0), cur(1), cur(2), cur(3), cur(4), cur(5), tail_spec,
                  halo(1), halo(2), halo(3), halo(4), halo(5),
                  full((CONV_GROUP_K, cw)), full((DN_CONV_K, cw)), full((DN_CONV_K, cw)),
                  full((DN_CONV_K, cw)), full((1, LANES)), full((1, LANES)), full((ts, ts))],
        out_specs=[out_tok(cw), out_tok(cw), out_tok(cw), out_tok(cw), out_tok(LANES)],
        out_shape=[jax.ShapeDtypeStruct((t, cw), F32)] * 4 + [jax.ShapeDtypeStruct((t, LANES), F32)],
        compiler_params=_params("parallel", "parallel"),
        name="mixer_pre",
    )(proj, proj, proj, proj, proj, proj, proj, proj, proj, proj, proj, proj,
      conv_w, qw, kw, vw, alog, dtb, tril)


_NN = (((1,), (0,)), ((), ()))
_NT = (((1,), (1,)), ((), ()))
_TN = (((0,), (0,)), ((), ()))
DN_LOCAL_PASSES = 1
DN_SCAN_PASSES = 1


def _mm(a, b, dims, passes):
    dot = lambda x, y: lax.dot_general(x, y, dims, preferred_element_type=F32)
    if passes == 6:
        return lax.dot_general(a, b, dims, preferred_element_type=F32, precision=HIGHEST)
    a_hi = a.astype(BF16)
    b_hi = b.astype(BF16)
    if passes == 1:
        return dot(a_hi, b_hi)
    a_lo = (a - a_hi.astype(F32)).astype(BF16)
    b_lo = (b - b_hi.astype(F32)).astype(BF16)
    return dot(a_hi, b_hi) + (dot(a_hi, b_lo) + dot(a_lo, b_hi))


DN_LOCAL_CHUNKS = 2


def _dn_local_body(q_ref, k_ref, v_ref, gb_ref, *out_refs):
    for ci in range(q_ref.shape[0] // DN_CHUNK):
        _dn_local_chunk(ci, q_ref, k_ref, v_ref, gb_ref, *out_refs)


def _dn_local_chunk(ci, q_ref, k_ref, v_ref, gb_ref, u_ref, w_ref, qd_ref, kd_ref, in_ref, gl_ref):
    c = DN_CHUNK
    n = DN_HEADS * c
    p = DN_LOCAL_PASSES
    tok = slice(ci * c, (ci + 1) * c)
    gb = gb_ref[tok, :]
    stack = lambda ref: jnp.concatenate(
        [ref[tok, h * DN_HEAD_DIM:(h + 1) * DN_HEAD_DIM] for h in range(DN_HEADS)], axis=0)
    lanes = lambda x: jnp.broadcast_to(x, (x.shape[0], LANES))
    col = lambda j: jnp.concatenate([lanes(gb[:, j + h:j + h + 1]) for h in range(DN_HEADS)], axis=0)
    q = stack(q_ref) * (DN_HEAD_DIM ** -0.5)
    k = stack(k_ref)
    v = stack(v_ref)
    beta = col(0)
    g = col(DN_HEADS)
    g_last = jnp.concatenate(
        [jnp.broadcast_to(gb[c - 1:c, DN_HEADS + h:DN_HEADS + h + 1], (c, LANES))
         for h in range(DN_HEADS)], axis=0)
    g_row = g.T[0:1, :]

    row = lax.broadcasted_iota(jnp.int32, (n, n), 0)
    cl = lax.broadcasted_iota(jnp.int32, (n, n), 1)
    shift = c.bit_length() - 1
    same_head = (row >> shift) == (cl >> shift)
    causal = same_head & (row >= cl)
    strict = same_head & (row > cl)
    diff = g[:, 0:1] - g_row
    decay = jnp.where(causal, jnp.exp(jnp.where(causal, diff, 0.0)), 0.0)
    k_beta = k * beta
    v_beta = v * beta
    a = jnp.where(strict, _mm(k_beta, k, _NT, p) * decay, 0.0)

    t_mat = (row == cl).astype(F32) - a
    pw = _mm(a, a, _NN, p)
    for _ in range(4):
        t_mat = t_mat + _mm(t_mat, pw, _NN, p)
        pw = _mm(pw, pw, _NN, p)
    t_mat = t_mat + _mm(t_mat, pw, _NN, p)

    e_g = jnp.exp(g)
    uw = _mm(t_mat, jnp.concatenate([v_beta, k_beta * e_g], axis=1), _NN, p)
    u_ref[ci] = uw[:, :DN_HEAD_DIM]
    w_ref[ci] = uw[:, DN_HEAD_DIM:]
    in_ref[ci] = _mm(q, k, _NT, p) * decay
    qd_ref[ci] = q * e_g
    kd_ref[ci] = k * jnp.exp(g_last - g)
    gl_ref[ci] = jnp.concatenate(
        [jnp.exp(g_last[h * c:h * c + 1]) for h in range(DN_HEADS)] +
        [jnp.zeros((SUBLANES - DN_HEADS, LANES), F32)], axis=0)


def _dn_scan_body(u_ref, w_ref, qd_ref, kd_ref, in_ref, gl_ref, o_ref, state_ref):
    c = DN_CHUNK
    p = DN_SCAN_PASSES

    @pl.when(pl.program_id(0) == 0)
    def _():
        state_ref[...] = jnp.zeros_like(state_ref)

    for b in range(u_ref.shape[0]):
        u = u_ref[b, 0]
        w = w_ref[b, 0]
        qd = qd_ref[b, 0]
        kd = kd_ref[b, 0]
        gl = gl_ref[b, 0]
        v_new, q_state = [], []
        for h in range(DN_HEADS):
            rows = slice(h * c, (h + 1) * c)
            ws = _mm(jnp.concatenate([w[rows], qd[rows]], axis=0), state_ref[b, h], _NN, p)
            v_new.append(u[rows] - ws[:c])
            q_state.append(ws[c:])
        o = jnp.concatenate(q_state, axis=0) + _mm(in_ref[b, 0], jnp.concatenate(v_new, axis=0),
                                                   _NN, p)
        for h in range(DN_HEADS):
            rows = slice(h * c, (h + 1) * c)
            state_ref[b, h] = (state_ref[b, h] * gl[h:h + 1] +
                               _mm(kd[rows], v_new[h], _TN, p))
        o_ref[b] = jnp.concatenate([o[h * c:(h + 1) * c] for h in range(DN_HEADS)], axis=1)


def _delta_rule(q, k, v, gb, batch, seq):
    t, w = q.shape
    nc = seq // DN_CHUNK
    n = DN_HEADS * DN_CHUNK
    cps = _pick_tile(batch * nc, DN_LOCAL_CHUNKS)
    tok = lambda width: pl.BlockSpec((cps * DN_CHUNK, width), lambda i: (i, 0))
    per_chunk = lambda rows, width: pl.BlockSpec((cps, rows, width), lambda i: (i, 0, 0))
    f = lambda rows, width: jax.ShapeDtypeStruct((batch * nc, rows, width), F32)
    u, wm, qd, kd, intra, gl = pl.pallas_call(
        _dn_local_body,
        grid=(batch * nc // cps,),
        in_specs=[tok(w), tok(w), tok(w), tok(LANES)],
        out_specs=[per_chunk(n, DN_HEAD_DIM)] * 4 + [per_chunk(n, n), per_chunk(SUBLANES, LANES)],
        out_shape=[f(n, DN_HEAD_DIM)] * 4 + [f(n, n), f(SUBLANES, LANES)],
        compiler_params=_params("parallel"),
        name="dn_local",
    )(q, k, v, gb)

    seq_blk = lambda rows, width: pl.BlockSpec((batch, 1, rows, width), lambda i: (0, i, 0, 0))
    by_batch = lambda x: x.reshape(batch, nc, *x.shape[1:])
    o = pl.pallas_call(
        _dn_scan_body,
        grid=(nc,),
        in_specs=[seq_blk(n, DN_HEAD_DIM)] * 4 + [seq_blk(n, n), seq_blk(SUBLANES, LANES)],
        out_specs=pl.BlockSpec((batch, DN_CHUNK, w), lambda i: (0, i, 0)),
        out_shape=jax.ShapeDtypeStruct((batch, seq, w), F32),
        scratch_shapes=[pltpu.VMEM((batch, DN_HEADS, DN_HEAD_DIM, DN_HEAD_DIM), F32)],
        compiler_params=_params("arbitrary"),
        name="dn_scan",
    )(by_batch(u), by_batch(wm), by_batch(qd), by_batch(kd), by_batch(intra), by_batch(gl))
    return o.reshape(t, w)


def _mixer_out_body(x_ref, yc_ref, o_ref, z_ref, dng_ref, wa_ref, wb_ref, h_ref):
    o = o_ref[...]
    z = z_ref[...]
    gain = dng_ref[...]
    parts = []
    for h in range(DN_HEADS):
        sl = slice(h * DN_HEAD_DIM, (h + 1) * DN_HEAD_DIM)
        zh = z[:, sl]
        parts.append(_rms(o[:, sl], gain) * (zh * jax.nn.sigmoid(zh)))
    y_dn = jnp.concatenate(parts, axis=-1)
    h_ref[...] = (x_ref[...] + _bdot(yc_ref[...], wa_ref[...]) + _bdot(y_dn, wb_ref[...]))


def _mixer_out(x, y_conv, o_dn, proj, dn_norm, w_out, tm):
    t, d = x.shape
    cw = y_conv.shape[1]
    tok = lambda width, col=0: pl.BlockSpec((tm, width), lambda i, col=col: (i, col))
    full = lambda shape: pl.BlockSpec(shape, lambda i: (0, 0))
    return pl.pallas_call(
        _mixer_out_body,
        grid=(t // tm,),
        in_specs=[tok(d), tok(cw), tok(cw), tok(cw, 6), full((1, DN_HEAD_DIM)),
                  full((cw, d)), full((cw, d))],
        out_specs=tok(d),
        out_shape=jax.ShapeDtypeStruct((t, d), F32),
        compiler_params=_params("parallel"),
        name="mixer_out",
    )(x, y_conv, o_dn, proj, dn_norm.reshape(1, DN_HEAD_DIM),
      w_out[:cw].astype(BF16), w_out[cw:].astype(BF16))


def _cross_attn_body(h_ref, g_ref, wq_ref, kv_ref, wo_ref, o_ref):
    h_in = h_ref[...]
    d = h_in.shape[1]
    dh = d // X_HEADS
    q = _bdot(_rms(h_in, g_ref[...]), wq_ref[...])
    outs = []
    for hd in range(X_HEADS):
        qh = q[:, hd * dh:(hd + 1) * dh].astype(BF16)
        kh = kv_ref[:, hd * dh:(hd + 1) * dh]
        vh = kv_ref[:, d + hd * dh:d + (hd + 1) * dh]
        s = lax.dot_general(qh, kh, (((1,), (1,)), ((), ())),
                            preferred_element_type=F32) * (dh ** -0.5)
        s = s - jnp.max(s, axis=-1, keepdims=True)
        e = jnp.exp(s)
        p = e / jnp.sum(e, axis=-1, keepdims=True)
        outs.append(jnp.dot(p.astype(BF16), vh, preferred_element_type=F32))
    o = jnp.concatenate(outs, axis=-1)
    o_ref[...] = h_in + _bdot(o, wo_ref[...])


def _cross_attn(h, gain, w_q, kv, w_o, batch, seq, mem_len, tm):
    t, d = h.shape
    nblk = seq // tm
    tok = pl.BlockSpec((tm, d), lambda b, i: (b * nblk + i, 0))
    full = lambda shape: pl.BlockSpec(shape, lambda b, i: (0, 0))
    return pl.pallas_call(
        _cross_attn_body,
        grid=(batch, nblk),
        in_specs=[tok, full((1, d)), full((d, d)),
                  pl.BlockSpec((mem_len, 2 * d), lambda b, i: (b, 0)), full((d, d))],
        out_specs=tok,
        out_shape=jax.ShapeDtypeStruct((t, d), F32),
        compiler_params=_params("parallel", "parallel"),
        name="cross_attn",
    )(h, gain.reshape(1, d), w_q.astype(BF16), kv.astype(BF16), w_o.astype(BF16))


def _top16_rows(s, payload=None):
    n = s.shape[0]
    row = lax.broadcasted_iota(jnp.int32, s.shape, 0).astype(F32)
    vals, pays = [], []
    for _ in range(PEER_TOPK):
        m = jnp.max(s, axis=0, keepdims=True)
        pos = jnp.min(jnp.where(s == m, row, float(n)), axis=0, keepdims=True)
        hit = row == pos
        vals.append(m)
        if payload is None:
            pays.append(pos)
        else:
            pays.append(jnp.sum(jnp.where(hit, payload, 0.0), axis=0, keepdims=True))
        s = jnp.where(hit, -jnp.inf, s)
    return jnp.concatenate(vals, axis=0), jnp.concatenate(pays, axis=0)


def _pair_candidates(v1, v2, pad):
    sub = lax.broadcasted_iota(jnp.int32, (SUBLANES, v1.shape[1]), 0)
    row = lambda v, a: jnp.broadcast_to(v[a:a + 1], sub.shape)
    lo8 = v2[:SUBLANES]
    blocks = [
        row(v1, 0) + lo8,
        row(v1, 0) + v2[SUBLANES:],
        row(v1, 1) + lo8,
        jnp.where(sub < 5, row(v1, 2) + lo8, pad),
        jnp.where(sub < 7,
                  jnp.where(sub < 4, row(v1, 3), row(v1, 4)) +
                  jnp.where(sub < 4, lo8, pltpu.roll(lo8, 4, 0)), pad),
        jnp.where(sub < 6,
                  jnp.where(sub < 2, row(v1, 5), jnp.where(sub < 4, row(v1, 6), row(v1, 7))) +
                  jnp.where((sub & 1) == 0, row(v2, 0), row(v2, 1)), pad),
        v1[SUBLANES:] + row(v2, 0),
    ]
    return jnp.concatenate(blocks, axis=0)


def _peer_route_body(h_ref, g_ref, wq_ref, keys_ref, xn_ref, idx_ref, gate_ref,
                     sc_ref, sel_ref, gt_ref):
    xn = _rms(h_ref[...], g_ref[...])
    xn_ref[...] = xn
    pq = _bdot(xn, wq_ref[...])
    ngrp = pq.shape[0] // LANES
    for j in range(2 * PEER_HEADS):
        qs = pq[:, j * PEER_HALF:(j + 1) * PEER_HALF].astype(BF16)
        sc = lax.dot_general(keys_ref[j], qs, (((1,), (1,)), ((), ())),
                             preferred_element_type=F32)
        for g in range(ngrp):
            sc_ref[j, g] = sc[:, g * LANES:(g + 1) * LANES]

    def head(h, carry):
        for g in range(ngrp):
            s1, i1 = _top16_rows(sc_ref[2 * h, g])
            s2, i2 = _top16_rows(sc_ref[2 * h + 1, g])
            cand = _pair_candidates(s1, s2, -jnp.inf)
            cand_idx = _pair_candidates(i1 * float(PEER_KEYS), i2, 0.0)
            best, sel = _top16_rows(cand, cand_idx)
            e = jnp.exp(best - jnp.max(best, axis=0, keepdims=True))
            sel_ref[g, h] = sel
            gt_ref[g, h] = e / jnp.sum(e, axis=0, keepdims=True)
        return carry

    lax.fori_loop(0, PEER_HEADS, head, 0)

    nsel = PEER_HEADS * PEER_TOPK
    for g in range(ngrp):
        rows = slice(g * LANES, (g + 1) * LANES)
        idx_ref[rows, :] = (sel_ref[g].reshape(nsel, LANES).T * float(PACK_ROWS)).astype(jnp.int32)
        gate_ref[rows, :] = gt_ref[g].reshape(nsel, LANES).T


def _peer_route(h, gain, w_pq, peer_keys, tm):
    t, d = h.shape
    nk = PEER_HEADS * 2
    ngrp = tm // LANES
    tok = lambda width: pl.BlockSpec((tm, width), lambda i: (i, 0))
    full = lambda shape: pl.BlockSpec(shape, lambda i: (0,) * len(shape))
    return pl.pallas_call(
        _peer_route_body,
        grid=(t // tm,),
        in_specs=[tok(d), full((1, d)), full((d, d)), full((nk, PEER_KEYS, PEER_HALF))],
        out_specs=[tok(d), tok(LANES), tok(LANES)],
        out_shape=[jax.ShapeDtypeStruct((t, d), F32),
                   jax.ShapeDtypeStruct((t, LANES), jnp.int32),
                   jax.ShapeDtypeStruct((t, LANES), F32)],
        scratch_shapes=[pltpu.VMEM((nk, ngrp, PEER_KEYS, LANES), F32),
                        pltpu.VMEM((ngrp, PEER_HEADS, PEER_TOPK, LANES), F32),
                        pltpu.VMEM((ngrp, PEER_HEADS, PEER_TOPK, LANES), F32)],
        compiler_params=_params("parallel"),
        name="peer_route",
    )(h, gain.reshape(1, d), w_pq.astype(BF16),
      peer_keys.reshape(nk, PEER_KEYS, PEER_HALF).astype(BF16))


PACK_ROWS = 4


def _pack_table(tbl):
    e, d = tbl.shape
    half = d // 2
    b = tbl.astype(BF16)
    lo = lax.bitcast_convert_type(b[:, :half], jnp.uint16).astype(jnp.uint32)
    hi = lax.bitcast_convert_type(b[:, half:], jnp.uint16).astype(jnp.uint32)
    return ((hi << 16) | lo).reshape(e * PACK_ROWS, LANES)


def _unpack(slab):
    lo = pltpu.bitcast(slab << 16, F32)
    hi = pltpu.bitcast(slab & jnp.uint32(0xFFFF0000), F32)
    return lo, hi


STAGE_STEPS = 4


def _stage_rows(idx_ref, t, tbl_ref, g_ref, step=None):
    nsel = idx_ref.shape[1]
    per = nsel // STAGE_STEPS
    ms = range(nsel) if step is None else range(step * per, (step + 1) * per)
    for m in ms:
        row = pl.multiple_of(idx_ref[t, m], PACK_ROWS)
        g_ref[m * PACK_ROWS:(m + 1) * PACK_ROWS, :] = tbl_ref[pl.ds(row, PACK_ROWS), :]


def _bf16_pieces(x, n):
    pieces = []
    for _ in range(n - 1):
        p = x.astype(BF16).astype(F32)
        pieces.append(p)
        x = x - p
    pieces.append(x.astype(BF16).astype(F32))
    return pieces


def _staged_group(idx_ref, tbl_ref, bufs, tb, base, step_fn):
    for k in range(SUBLANES):
        t = base + k
        nxt = t + 1 if k + 1 < SUBLANES else jnp.minimum(t + 1, tb - 1)
        for step in range(STAGE_STEPS):
            _stage_rows(idx_ref, nxt, tbl_ref, bufs[(k + 1) % 2], step)
            step_fn(t, k, step, bufs[k % 2])


def _peer_dot_body(idx_ref, x_ref, tbl_ref, act_ref, g0_ref, g1_ref):
    tb = idx_ref.shape[0]
    nsel = idx_ref.shape[1]
    nslab = SUBLANES * SUBLANES
    lane = lax.broadcasted_iota(jnp.int32, (LANES, LANES), 1)
    col_slab = lane & (nslab - 1)
    _stage_rows(idx_ref, 0, tbl_ref, g0_ref)

    def group(i, carry):
        base = pl.multiple_of(i * SUBLANES, SUBLANES)
        pieces = _bf16_pieces(x_ref[i].reshape(nslab, LANES), LANES // nslab)
        xt = jnp.concatenate(pieces, axis=0).T.astype(BF16)
        zero = jnp.zeros_like(xt)
        acc = [jnp.zeros((nsel, LANES), F32)]

        def step_fn(t, k, j, g_ref):
            lo, hi = _unpack(g_ref[pl.ds(j, nsel, stride=PACK_ROWS), :])
            for r, mat in ((j, lo), (PACK_ROWS + j, hi)):
                rhs = jnp.where(col_slab == SUBLANES * r + k, xt, zero)
                acc[0] = acc[0] + jnp.dot(mat.astype(BF16), rhs, preferred_element_type=F32)

        _staged_group(idx_ref, tbl_ref, (g0_ref, g1_ref), tb, base, step_fn)
        out_t = acc[0].T
        act_ref[pl.ds(base, SUBLANES), :] = jnp.sum(
            out_t.reshape(LANES // SUBLANES, SUBLANES, nsel), axis=0)
        return carry

    lax.fori_loop(0, tb // SUBLANES, group, 0)


def _peer_dot(idx, xr, tbl, tb):
    t, nsel = idx.shape
    gbuf = pltpu.VMEM((nsel * PACK_ROWS, LANES), jnp.uint32)
    return pl.pallas_call(
        _peer_dot_body,
        grid=(t // tb,),
        in_specs=[pl.BlockSpec((tb, nsel), lambda i: (i, 0), memory_space=pltpu.SMEM),
                  pl.BlockSpec((tb // SUBLANES, SUBLANES, SUBLANES, LANES),
                               lambda i: (i, 0, 0, 0)),
                  pl.BlockSpec(tbl.shape, lambda i: (0, 0), pipeline_mode=pl.Buffered(1))],
        out_specs=pl.BlockSpec((tb, nsel), lambda i: (i, 0)),
        out_shape=jax.ShapeDtypeStruct((t, nsel), F32),
        scratch_shapes=[gbuf, gbuf],
        compiler_params=_params("arbitrary"),
        name="peer_dot",
    )(idx, xr, tbl)


def _peer_weights_body(act_ref, gate_ref, rep_ref, w_ref):
    a = act_ref[...]
    w = gate_ref[...] * (0.5 * a * (1.0 + lax.erf(a * (2.0 ** -0.5))))
    rep = rep_ref[...]
    w_ref[...] = sum(jnp.dot(p.astype(BF16), rep, preferred_element_type=F32)
                     for p in _bf16_pieces(w, 3))


def _peer_weights(act, gates, tm):
    t, n = act.shape
    nrow = 2 * PACK_ROWS * n
    rep = (jnp.arange(n)[:, None] == (jnp.arange(nrow) // (2 * PACK_ROWS))[None, :]).astype(BF16)
    tok = lambda width: pl.BlockSpec((tm, width), lambda i: (i, 0))
    return pl.pallas_call(
        _peer_weights_body,
        grid=(t // tm,),
        in_specs=[tok(n), tok(n), pl.BlockSpec(rep.shape, lambda i: (0, 0))],
        out_specs=tok(nrow),
        out_shape=jax.ShapeDtypeStruct((t, nrow), F32),
        compiler_params=_params("parallel"),
        name="peer_weights",
    )(act, gates, rep)


def _peer_sum_body(idx_ref, w_ref, mask_ref, tbl_ref, y_ref, g0_ref, g1_ref):
    tb = y_ref.shape[0]
    mask = mask_ref[...]
    nrow = g0_ref.shape[0] // STAGE_STEPS
    ncol = mask.shape[1] // STAGE_STEPS
    _stage_rows(idx_ref, 0, tbl_ref, g0_ref)

    def group(i, carry):
        base = pl.multiple_of(i * SUBLANES, SUBLANES)
        w_rep = w_ref[pl.ds(base, SUBLANES), :]
        acc = [None]

        def step_fn(t, k, step, g_ref):
            cols = slice(step * ncol, (step + 1) * ncol)
            wb = jnp.broadcast_to(w_rep[k:k + 1, cols], (SUBLANES, ncol)) * mask[:, cols]
            a_hi = wb.astype(BF16)
            a_lo = (wb - a_hi.astype(F32)).astype(BF16)
            lhs = jnp.concatenate([a_hi, a_lo], axis=0)
            staged = pltpu.bitcast(g_ref[step * nrow:(step + 1) * nrow, :], BF16)
            part = jnp.dot(lhs, staged, preferred_element_type=F32)
            acc[0] = part if step == 0 else acc[0] + part
            if step == STAGE_STEPS - 1:
                y_ref[t] = acc[0][:SUBLANES] + acc[0][SUBLANES:]

        _staged_group(idx_ref, tbl_ref, (g0_ref, g1_ref), tb, base, step_fn)
        return carry

    lax.fori_loop(0, tb // SUBLANES, group, 0)


def _peer_sum(idx, w, tbl, tb):
    t, nsel = idx.shape
    nrow = w.shape[1]
    c = jnp.arange(nrow)
    out_row = (c % 2) * PACK_ROWS + (c % (2 * PACK_ROWS)) // 2
    mask = (jnp.arange(SUBLANES)[:, None] == out_row[None, :]).astype(F32)
    gbuf = pltpu.VMEM((nsel * PACK_ROWS, LANES), jnp.uint32)
    const = lambda shape: pl.BlockSpec(shape, lambda i: (0, 0))
    return pl.pallas_call(
        _peer_sum_body,
        grid=(t // tb,),
        in_specs=[pl.BlockSpec((tb, nsel), lambda i: (i, 0), memory_space=pltpu.SMEM),
                  pl.BlockSpec((tb, nrow), lambda i: (i, 0)),
                  const(mask.shape),
                  pl.BlockSpec(tbl.shape, lambda i: (0, 0), pipeline_mode=pl.Buffered(1))],
        out_specs=pl.BlockSpec((tb, SUBLANES, LANES), lambda i: (i, 0, 0)),
        out_shape=jax.ShapeDtypeStruct((t, SUBLANES, LANES), F32),
        scratch_shapes=[gbuf, gbuf],
        compiler_params=_params("arbitrary"),
        name="peer_sum",
    )(idx, w, mask, tbl)


def _final_body(h_ref, y_ref, g_ref, o_ref):
    o_ref[...] = _rms(h_ref[...] + y_ref[...], g_ref[...])


def _final(h, y, gain, tm):
    t, d = h.shape
    tok = pl.BlockSpec((tm, d), lambda i: (i, 0))
    return pl.pallas_call(
        _final_body,
        grid=(t // tm,),
        in_specs=[tok, tok, pl.BlockSpec((1, d), lambda i: (0, 0))],
        out_specs=tok,
        out_shape=jax.ShapeDtypeStruct((t, d), F32),
        compiler_params=_params("parallel"),
        name="final_norm",
    )(h, y, gain.reshape(1, d))


def _pick_tile(n, pref):
    tile = min(n, pref)
    while n % tile:
        tile //= 2
    return tile


def _layer(h, mem, batch, seq, norm_mix, w_in, conv_w, dn_conv_w, dn_a_log, dn_dt_bias, dn_norm,
           w_out, norm_x, norm_mem, w_xq, w_xkv, w_xo, norm_ffn, w_pq, peer_keys,
           expert_u, expert_v):
    t, d = h.shape
    mem_len = mem.shape[0] // batch
    in_cols = w_in.shape[1]
    pad = (-in_cols) % LANES
    w_in_p = jnp.pad(w_in, ((0, 0), (0, pad))).astype(BF16)

    proj = _norm_matmul(h, norm_mix, w_in_p, _pick_tile(t, 256), "in_proj")
    y_conv, q, k, v, gb = _mixer_pre(proj, conv_w, dn_conv_w, dn_a_log, dn_dt_bias,
                                     batch, seq, _pick_tile(seq, 256))
    o_dn = _delta_rule(q, k, v, gb, batch, seq)
    h = _mixer_out(h, y_conv, o_dn, proj, dn_norm, w_out, _pick_tile(t, 512))

    kv = _norm_matmul(mem, norm_mem, w_xkv.astype(BF16), _pick_tile(mem.shape[0], 256), "kv_proj")
    h = _cross_attn(h, norm_x, w_xq, kv, w_xo, batch, seq, mem_len, _pick_tile(seq, 512))

    xn, idx, gates = _peer_route(h, norm_ffn, w_pq, peer_keys, _pick_tile(t, 256))
    tb = _pick_tile(t, 64)
    xr = xn.reshape(t // SUBLANES, SUBLANES, SUBLANES, LANES).transpose(0, 2, 1, 3)
    act = _peer_dot(idx, xr, _pack_table(expert_u), tb)
    w = _peer_weights(act, gates, _pick_tile(t, 1024))
    y = _peer_sum(idx, w, _pack_table(expert_v), tb)
    return h, y.reshape(t, d)


def kernel(x, mem, norm_mix, w_in, conv_w, dn_conv_w, dn_a_log, dn_dt_bias, dn_norm, w_out,
           norm_x, norm_mem, w_xq, w_xkv, w_xo, norm_ffn, w_pq, peer_keys, expert_u, expert_v,
           norm_final):
    batch, seq, d = x.shape
    depth = norm_mix.shape[0]
    h = x.reshape(batch * seq, d)
    mem2 = mem.reshape(batch * mem.shape[1], d)
    y = jnp.zeros_like(h)
    for layer in range(depth):
        if layer:
            h = h + y
        h, y = _layer(h, mem2, batch, seq, norm_mix[layer], w_in[layer], conv_w[layer],
                      dn_conv_w[layer], dn_a_log[layer], dn_dt_bias[layer], dn_norm[layer],
                      w_out[layer], norm_x[layer], norm_mem[layer], w_xq[layer], w_xkv[layer],
                      w_xo[layer], norm_ffn[layer], w_pq[layer], peer_keys[layer],
                      expert_u[layer], expert_v[layer])
    out = _final(h, y, norm_final, _pick_tile(batch * seq, 512))
    return out.reshape(batch, seq, d)
```

```python
import functools

import jax
import jax.numpy as jnp
from jax import lax
from jax.experimental import pallas as pl
from jax.experimental.pallas import tpu as pltpu

EPS = 1e-6
F32 = jnp.float32
BF16 = jnp.bfloat16
HIGHEST = lax.Precision.HIGHEST

LANES = 128
SUBLANES = 8
VMEM_LIMIT = 56 * 1024 * 1024

CONV_GROUP_K = 3
DN_CONV_K = 4
DN_HEADS = 4
DN_HEAD_DIM = 128
DN_CHUNK = 64
X_HEADS = 4
PEER_HEADS = 8
PEER_KEYS = 128
PEER_TOPK = 16
PEER_HALF = 64


def _params(*sem):
    return pltpu.CompilerParams(dimension_semantics=sem, vmem_limit_bytes=VMEM_LIMIT)


def _rms(x, gain):
    ms = jnp.mean(x * x, axis=-1, keepdims=True)
    return x * lax.rsqrt(ms + EPS) * gain


def _bdot(a, b):
    return jnp.dot(a.astype(BF16), b.astype(BF16), preferred_element_type=F32)


def _hdot(a, b):
    return jnp.dot(a, b, preferred_element_type=F32, precision=HIGHEST)


def _hdot_nt(a, b):
    return lax.dot_general(a, b, (((1,), (1,)), ((), ())),
                           preferred_element_type=F32, precision=HIGHEST)


def _hdot_tn(a, b):
    return lax.dot_general(a, b, (((0,), (0,)), ((), ())),
                           preferred_element_type=F32, precision=HIGHEST)


def _norm_matmul_body(x_ref, g_ref, w_ref, o_ref):
    xn = _rms(x_ref[...], g_ref[...])
    o_ref[...] = jnp.dot(xn.astype(BF16), w_ref[...], preferred_element_type=F32)


def _norm_matmul(x, gain, w, tm, name):
    t, d = x.shape
    n = w.shape[1]
    return pl.pallas_call(
        _norm_matmul_body,
        grid=(t // tm,),
        in_specs=[pl.BlockSpec((tm, d), lambda i: (i, 0)),
                  pl.BlockSpec((1, d), lambda i: (0, 0)),
                  pl.BlockSpec((d, n), lambda i: (0, 0))],
        out_specs=pl.BlockSpec((tm, n), lambda i: (i, 0)),
        out_shape=jax.ShapeDtypeStruct((t, n), F32),
        compiler_params=_params("parallel"),
        name=name,
    )(x, gain.reshape(1, d), w)


def _shift_rows(x, halo, k):
    rolled = pltpu.roll(x, k, 0)
    hal = pltpu.roll(halo, k, 0)
    row = lax.broadcasted_iota(jnp.int32, (SUBLANES, x.shape[1]), 0)
    head = jnp.where(row < k, hal, rolled[:SUBLANES])
    return jnp.concatenate([head, rolled[SUBLANES:]], axis=0)


def _causal_conv(x, halo, w_ref):
    kk = w_ref.shape[0]
    y = x * w_ref[kk - 1:kk, :]
    for j in range(1, kk):
        y = y + _shift_rows(x, halo, j) * w_ref[kk - 1 - j:kk - j, :]
    return y


def _l2norm_heads(x):
    outs = []
    for h in range(DN_HEADS):
        xh = x[:, h * DN_HEAD_DIM:(h + 1) * DN_HEAD_DIM]
        outs.append(xh * lax.rsqrt(jnp.sum(xh * xh, axis=-1, keepdims=True) + EPS))
    return jnp.concatenate(outs, axis=-1)


def _mixer_pre_body(b_ref, c_ref, h_ref, q_ref, k_ref, v_ref, t_ref,
                    ch_ref, hh_ref, qh_ref, kh_ref, vh_ref,
                    cw_ref, qw_ref, kw_ref, vw_ref, alog_ref, dtb_ref, tril_ref,
                    yc_ref, qo_ref, ko_ref, vo_ref, gb_ref):
    first = pl.program_id(1) == 0
    keep = jnp.where(first, 0.0, 1.0).astype(F32)

    u = c_ref[...] * h_ref[...]
    uh = ch_ref[...] * hh_ref[...] * keep
    yc_ref[...] = b_ref[...] * _causal_conv(u, uh, cw_ref)

    def dn_branch(x_ref, xh_ref, w_ref):
        y = _causal_conv(x_ref[...], xh_ref[...] * keep, w_ref)
        return y * jax.nn.sigmoid(y)

    qo_ref[...] = _l2norm_heads(dn_branch(q_ref, qh_ref, qw_ref))
    ko_ref[...] = _l2norm_heads(dn_branch(k_ref, kh_ref, kw_ref))
    vo_ref[...] = dn_branch(v_ref, vh_ref, vw_ref)

    tail = t_ref[...]
    beta = jax.nn.sigmoid(tail)
    g = -jnp.exp(alog_ref[...]) * jax.nn.softplus(tail + dtb_ref[...])
    gcum = _hdot(tril_ref[...], g)
    lane = lax.broadcasted_iota(jnp.int32, tail.shape, 1)
    gb_ref[...] = jnp.where(lane < DN_HEADS, beta, gcum)


def _mixer_pre(proj, conv_w, dn_conv_w, dn_a_log, dn_dt_bias, batch, seq, ts):
    t = proj.shape[0]
    cw = 512
    nblk = seq // ts
    hb = ts // SUBLANES

    def cur(col):
        return pl.BlockSpec((ts, cw if col < 7 else LANES),
                            lambda b, i, col=col: (b * nblk + i, col))

    def halo(col):
        return pl.BlockSpec(
            (SUBLANES, cw),
            lambda b, i, col=col: (jnp.maximum((b * nblk + i) * hb - 1, 0), col))

    def full(shape):
        return pl.BlockSpec(shape, lambda b, i: (0,) * len(shape))

    qw, kw, vw = (dn_conv_w[:, j * cw:(j + 1) * cw] for j in range(3))
    lane_pad = jnp.zeros((LANES - 2 * DN_HEADS,), F32)
    alog = jnp.concatenate([jnp.zeros((DN_HEADS,), F32), dn_a_log, lane_pad]).reshape(1, LANES)
    dtb = jnp.concatenate([jnp.zeros((DN_HEADS,), F32), dn_dt_bias, lane_pad]).reshape(1, LANES)
    r = jnp.arange(ts)
    tril = ((r[:, None] >= r[None, :]) &
            (r[:, None] // DN_CHUNK == r[None, :] // DN_CHUNK)).astype(F32)

    tail_spec = pl.BlockSpec((ts, LANES), lambda b, i: (b * nblk + i, 7 * cw // LANES))
    out_tok = lambda w: pl.BlockSpec((ts, w), lambda b, i: (b * nblk + i, 0))
    return pl.pallas_call(
        _mixer_pre_body,
        grid=(batch, nblk),
        in_specs=[cur(0), cur(1), cur(2), cur(3), cur(4), cur(5), tail_spec,
                  halo(1), halo(2), halo(3), halo(4), halo(5),
                  full((CONV_GROUP_K, cw)), full((DN_CONV_K, cw)), full((DN_CONV_K, cw)),
                  full((DN_CONV_K, cw)), full((1, LANES)), full((1, LANES)), full((ts, ts))],
        out_specs=[out_tok(cw), out_tok(cw), out_tok(cw), out_tok(cw), out_tok(LANES)],
        out_shape=[jax.ShapeDtypeStruct((t, cw), F32)] * 4 + [jax.ShapeDtypeStruct((t, LANES), F32)],
        compiler_params=_params("parallel", "parallel"),
        name="mixer_pre",
    )(proj, proj, proj, proj, proj, proj, proj, proj, proj, proj, proj, proj,
      conv_w, qw, kw, vw, alog, dtb, tril)


_NN = (((1,), (0,)), ((), ()))
_NT = (((1,), (1,)), ((), ()))
_TN = (((0,), (0,)), ((), ()))
DN_LOCAL_PASSES = 1
DN_SCAN_PASSES = 1


def _mm(a, b, dims, passes):
    dot = lambda x, y: lax.dot_general(x, y, dims, preferred_element_type=F32)
    if passes == 6:
        return lax.dot_general(a, b, dims, preferred_element_type=F32, precision=HIGHEST)
    a_hi = a.astype(BF16)
    b_hi = b.astype(BF16)
    if passes == 1:
        return dot(a_hi, b_hi)
    a_lo = (a - a_hi.astype(F32)).astype(BF16)
    b_lo = (b - b_hi.astype(F32)).astype(BF16)
    return dot(a_hi, b_hi) + (dot(a_hi, b_lo) + dot(a_lo, b_hi))


DN_LOCAL_CHUNKS = 4


def _dn_local_body(q_ref, k_ref, v_ref, gb_ref, u_ref, w_ref, qd_ref, kd_ref, in_ref, gl_ref):
    c = DN_CHUNK
    n = DN_HEADS * c
    p = DN_LOCAL_PASSES
    cis = range(q_ref.shape[0] // c)
    each = lambda f, *xs: [f(*a) for a in zip(*xs)]
    lanes = lambda x: jnp.broadcast_to(x, (x.shape[0], LANES))

    def stack(ref, ci):
        return jnp.concatenate([ref[ci * c:(ci + 1) * c, h * DN_HEAD_DIM:(h + 1) * DN_HEAD_DIM]
                                for h in range(DN_HEADS)], axis=0)

    def col(gb, j):
        return jnp.concatenate([lanes(gb[:, j + h:j + h + 1]) for h in range(DN_HEADS)], axis=0)

    gb = [gb_ref[ci * c:(ci + 1) * c, :] for ci in cis]
    q = [stack(q_ref, ci) * (DN_HEAD_DIM ** -0.5) for ci in cis]
    k = [stack(k_ref, ci) for ci in cis]
    v = [stack(v_ref, ci) for ci in cis]
    beta = [col(x, 0) for x in gb]
    g = [col(x, DN_HEADS) for x in gb]
    g_last = [jnp.concatenate(
        [jnp.broadcast_to(x[c - 1:c, DN_HEADS + h:DN_HEADS + h + 1], (c, LANES))
         for h in range(DN_HEADS)], axis=0) for x in gb]

    row = lax.broadcasted_iota(jnp.int32, (n, n), 0)
    cl = lax.broadcasted_iota(jnp.int32, (n, n), 1)
    shift = c.bit_length() - 1
    same_head = (row >> shift) == (cl >> shift)
    causal = same_head & (row >= cl)
    strict = same_head & (row > cl)
    eye = (row == cl).astype(F32)
    decay = [jnp.where(causal, jnp.exp(jnp.where(causal, x[:, 0:1] - x.T[0:1, :], 0.0)), 0.0)
             for x in g]
    k_beta = each(lambda a, b: a * b, k, beta)
    v_beta = each(lambda a, b: a * b, v, beta)
    a = each(lambda kb, kk, d: jnp.where(strict, _mm(kb, kk, _NT, p) * d, 0.0), k_beta, k, decay)

    t_mat = [eye - x for x in a]
    pw = [_mm(x, x, _NN, p) for x in a]
    for _ in range(4):
        t_mat = each(lambda t, w_: t + _mm(t, w_, _NN, p), t_mat, pw)
        pw = [_mm(x, x, _NN, p) for x in pw]
    t_mat = each(lambda t, w_: t + _mm(t, w_, _NN, p), t_mat, pw)

    e_g = [jnp.exp(x) for x in g]
    uw = each(lambda t, vb, kb, e: _mm(t, jnp.concatenate([vb, kb * e], axis=1), _NN, p),
              t_mat, v_beta, k_beta, e_g)
    intra = each(lambda qq, kk, d: _mm(qq, kk, _NT, p) * d, q, k, decay)
    for ci in cis:
        u_ref[ci] = uw[ci][:, :DN_HEAD_DIM]
        w_ref[ci] = uw[ci][:, DN_HEAD_DIM:]
        in_ref[ci] = intra[ci]
        qd_ref[ci] = q[ci] * e_g[ci]
        kd_ref[ci] = k[ci] * jnp.exp(g_last[ci] - g[ci])
        gl_ref[ci] = jnp.concatenate(
            [jnp.exp(g_last[ci][h * c:h * c + 1]) for h in range(DN_HEADS)] +
            [jnp.zeros((SUBLANES - DN_HEADS, LANES), F32)], axis=0)


def _dn_scan_body(u_ref, w_ref, qd_ref, kd_ref, in_ref, gl_ref, o_ref, state_ref):
    c = DN_CHUNK
    p = DN_SCAN_PASSES

    @pl.when(pl.program_id(0) == 0)
    def _():
        state_ref[...] = jnp.zeros_like(state_ref)

    for b in range(u_ref.shape[0]):
        u = u_ref[b, 0]
        w = w_ref[b, 0]
        qd = qd_ref[b, 0]
        kd = kd_ref[b, 0]
        gl = gl_ref[b, 0]
        v_new, q_state = [], []
        for h in range(DN_HEADS):
            rows = slice(h * c, (h + 1) * c)
            ws = _mm(jnp.concatenate([w[rows], qd[rows]], axis=0), state_ref[b, h], _NN, p)
            v_new.append(u[rows] - ws[:c])
            q_state.append(ws[c:])
        o = jnp.concatenate(q_state, axis=0) + _mm(in_ref[b, 0], jnp.concatenate(v_new, axis=0),
                                                   _NN, p)
        for h in range(DN_HEADS):
            rows = slice(h * c, (h + 1) * c)
            state_ref[b, h] = (state_ref[b, h] * gl[h:h + 1] +
                               _mm(kd[rows], v_new[h], _TN, p))
        o_ref[b] = jnp.concatenate([o[h * c:(h + 1) * c] for h in range(DN_HEADS)], axis=1)


def _delta_rule(q, k, v, gb, batch, seq):
    t, w = q.shape
    nc = seq // DN_CHUNK
    n = DN_HEADS * DN_CHUNK
    cps = _pick_tile(batch * nc, DN_LOCAL_CHUNKS)
    tok = lambda width: pl.BlockSpec((cps * DN_CHUNK, width), lambda i: (i, 0))
    per_chunk = lambda rows, width: pl.BlockSpec((cps, rows, width), lambda i: (i, 0, 0))
    f = lambda rows, width: jax.ShapeDtypeStruct((batch * nc, rows, width), F32)
    u, wm, qd, kd, intra, gl = pl.pallas_call(
        _dn_local_body,
        grid=(batch * nc // cps,),
        in_specs=[tok(w), tok(w), tok(w), tok(LANES)],
        out_specs=[per_chunk(n, DN_HEAD_DIM)] * 4 + [per_chunk(n, n), per_chunk(SUBLANES, LANES)],
        out_shape=[f(n, DN_HEAD_DIM)] * 4 + [f(n, n), f(SUBLANES, LANES)],
        compiler_params=_params("parallel"),
        name="dn_local",
    )(q, k, v, gb)

    seq_blk = lambda rows, width: pl.BlockSpec((batch, 1, rows, width), lambda i: (0, i, 0, 0))
    by_batch = lambda x: x.reshape(batch, nc, *x.shape[1:])
    o = pl.pallas_call(
        _dn_scan_body,
        grid=(nc,),
        in_specs=[seq_blk(n, DN_HEAD_DIM)] * 4 + [seq_blk(n, n), seq_blk(SUBLANES, LANES)],
        out_specs=pl.BlockSpec((batch, DN_CHUNK, w), lambda i: (0, i, 0)),
        out_shape=jax.ShapeDtypeStruct((batch, seq, w), F32),
        scratch_shapes=[pltpu.VMEM((batch, DN_HEADS, DN_HEAD_DIM, DN_HEAD_DIM), F32)],
        compiler_params=_params("arbitrary"),
        name="dn_scan",
    )(by_batch(u), by_batch(wm), by_batch(qd), by_batch(kd), by_batch(intra), by_batch(gl))
    return o.reshape(t, w)


def _mixer_out_body(x_ref, yc_ref, o_ref, z_ref, dng_ref, wa_ref, wb_ref, h_ref):
    o = o_ref[...]
    z = z_ref[...]
    gain = dng_ref[...]
    parts = []
    for h in range(DN_HEADS):
        sl = slice(h * DN_HEAD_DIM, (h + 1) * DN_HEAD_DIM)
        zh = z[:, sl]
        parts.append(_rms(o[:, sl], gain) * (zh * jax.nn.sigmoid(zh)))
    y_dn = jnp.concatenate(parts, axis=-1)
    h_ref[...] = (x_ref[...] + _bdot(yc_ref[...], wa_ref[...]) + _bdot(y_dn, wb_ref[...]))


def _mixer_out(x, y_conv, o_dn, proj, dn_norm, w_out, tm):
    t, d = x.shape
    cw = y_conv.shape[1]
    tok = lambda width, col=0: pl.BlockSpec((tm, width), lambda i, col=col: (i, col))
    full = lambda shape: pl.BlockSpec(shape, lambda i: (0, 0))
    return pl.pallas_call(
        _mixer_out_body,
        grid=(t // tm,),
        in_specs=[tok(d), tok(cw), tok(cw), tok(cw, 6), full((1, DN_HEAD_DIM)),
                  full((cw, d)), full((cw, d))],
        out_specs=tok(d),
        out_shape=jax.ShapeDtypeStruct((t, d), F32),
        compiler_params=_params("parallel"),
        name="mixer_out",
    )(x, y_conv, o_dn, proj, dn_norm.reshape(1, DN_HEAD_DIM),
      w_out[:cw].astype(BF16), w_out[cw:].astype(BF16))


def _cross_attn_body(h_ref, g_ref, wq_ref, kv_ref, wo_ref, o_ref):
    h_in = h_ref[...]
    d = h_in.shape[1]
    dh = d // X_HEADS
    q = _bdot(_rms(h_in, g_ref[...]), wq_ref[...])
    outs = []
    for hd in range(X_HEADS):
        qh = q[:, hd * dh:(hd + 1) * dh].astype(BF16)
        kh = kv_ref[:, hd * dh:(hd + 1) * dh]
        vh = kv_ref[:, d + hd * dh:d + (hd + 1) * dh]
        s = lax.dot_general(qh, kh, (((1,), (1,)), ((), ())),
                            preferred_element_type=F32) * (dh ** -0.5)
        s = s - jnp.max(s, axis=-1, keepdims=True)
        e = jnp.exp(s)
        p = e / jnp.sum(e, axis=-1, keepdims=True)
        outs.append(jnp.dot(p.astype(BF16), vh, preferred_element_type=F32))
    o = jnp.concatenate(outs, axis=-1)
    o_ref[...] = h_in + _bdot(o, wo_ref[...])


def _cross_attn(h, gain, w_q, kv, w_o, batch, seq, mem_len, tm):
    t, d = h.shape
    nblk = seq // tm
    tok = pl.BlockSpec((tm, d), lambda b, i: (b * nblk + i, 0))
    full = lambda shape: pl.BlockSpec(shape, lambda b, i: (0, 0))
    return pl.pallas_call(
        _cross_attn_body,
        grid=(batch, nblk),
        in_specs=[tok, full((1, d)), full((d, d)),
                  pl.BlockSpec((mem_len, 2 * d), lambda b, i: (b, 0)), full((d, d))],
        out_specs=tok,
        out_shape=jax.ShapeDtypeStruct((t, d), F32),
        compiler_params=_params("parallel", "parallel"),
        name="cross_attn",
    )(h, gain.reshape(1, d), w_q.astype(BF16), kv.astype(BF16), w_o.astype(BF16))


def _top16_rows(s, payload=None):
    n = s.shape[0]
    row = lax.broadcasted_iota(jnp.int32, s.shape, 0).astype(F32)
    vals, pays = [], []
    for _ in range(PEER_TOPK):
        m = jnp.max(s, axis=0, keepdims=True)
        pos = jnp.min(jnp.where(s == m, row, float(n)), axis=0, keepdims=True)
        hit = row == pos
        vals.append(m)
        if payload is None:
            pays.append(pos)
        else:
            pays.append(jnp.sum(jnp.where(hit, payload, 0.0), axis=0, keepdims=True))
        s = jnp.where(hit, -jnp.inf, s)
    return jnp.concatenate(vals, axis=0), jnp.concatenate(pays, axis=0)


def _pair_candidates(v1, v2, pad):
    sub = lax.broadcasted_iota(jnp.int32, (SUBLANES, v1.shape[1]), 0)
    row = lambda v, a: jnp.broadcast_to(v[a:a + 1], sub.shape)
    lo8 = v2[:SUBLANES]
    blocks = [
        row(v1, 0) + lo8,
        row(v1, 0) + v2[SUBLANES:],
        row(v1, 1) + lo8,
        jnp.where(sub < 5, row(v1, 2) + lo8, pad),
        jnp.where(sub < 7,
                  jnp.where(sub < 4, row(v1, 3), row(v1, 4)) +
                  jnp.where(sub < 4, lo8, pltpu.roll(lo8, 4, 0)), pad),
        jnp.where(sub < 6,
                  jnp.where(sub < 2, row(v1, 5), jnp.where(sub < 4, row(v1, 6), row(v1, 7))) +
                  jnp.where((sub & 1) == 0, row(v2, 0), row(v2, 1)), pad),
        v1[SUBLANES:] + row(v2, 0),
    ]
    return jnp.concatenate(blocks, axis=0)


def _peer_route_body(h_ref, g_ref, wq_ref, keys_ref, xn_ref, idx_ref, gate_ref,
                     sc_ref, sel_ref, gt_ref):
    xn = _rms(h_ref[...], g_ref[...])
    xn_ref[...] = xn
    pq = _bdot(xn, wq_ref[...])
    ngrp = pq.shape[0] // LANES
    for j in range(2 * PEER_HEADS):
        qs = pq[:, j * PEER_HALF:(j + 1) * PEER_HALF].astype(BF16)
        sc = lax.dot_general(keys_ref[j], qs, (((1,), (1,)), ((), ())),
                             preferred_element_type=F32)
        for g in range(ngrp):
            sc_ref[j, g] = sc[:, g * LANES:(g + 1) * LANES]

    def head(h, carry):
        for g in range(ngrp):
            s1, i1 = _top16_rows(sc_ref[2 * h, g])
            s2, i2 = _top16_rows(sc_ref[2 * h + 1, g])
            cand = _pair_candidates(s1, s2, -jnp.inf)
            cand_idx = _pair_candidates(i1 * float(PEER_KEYS), i2, 0.0)
            best, sel = _top16_rows(cand, cand_idx)
            e = jnp.exp(best - jnp.max(best, axis=0, keepdims=True))
            sel_ref[g, h] = sel
            gt_ref[g, h] = e / jnp.sum(e, axis=0, keepdims=True)
        return carry

    lax.fori_loop(0, PEER_HEADS, head, 0)

    nsel = PEER_HEADS * PEER_TOPK
    for g in range(ngrp):
        rows = slice(g * LANES, (g + 1) * LANES)
        idx_ref[rows, :] = (sel_ref[g].reshape(nsel, LANES).T * float(PACK_ROWS)).astype(jnp.int32)
        gate_ref[rows, :] = gt_ref[g].reshape(nsel, LANES).T


def _peer_route(h, gain, w_pq, peer_keys, tm):
    t, d = h.shape
    nk = PEER_HEADS * 2
    ngrp = tm // LANES
    tok = lambda width: pl.BlockSpec((tm, width), lambda i: (i, 0))
    full = lambda shape: pl.BlockSpec(shape, lambda i: (0,) * len(shape))
    return pl.pallas_call(
        _peer_route_body,
        grid=(t // tm,),
        in_specs=[tok(d), full((1, d)), full((d, d)), full((nk, PEER_KEYS, PEER_HALF))],
        out_specs=[tok(d), tok(LANES), tok(LANES)],
        out_shape=[jax.ShapeDtypeStruct((t, d), F32),
                   jax.ShapeDtypeStruct((t, LANES), jnp.int32),
                   jax.ShapeDtypeStruct((t, LANES), F32)],
        scratch_shapes=[pltpu.VMEM((nk, ngrp, PEER_KEYS, LANES), F32),
                        pltpu.VMEM((ngrp, PEER_HEADS, PEER_TOPK, LANES), F32),
                        pltpu.VMEM((ngrp, PEER_HEADS, PEER_TOPK, LANES), F32)],
        compiler_params=_params("parallel"),
        name="peer_route",
    )(h, gain.reshape(1, d), w_pq.astype(BF16),
      peer_keys.reshape(nk, PEER_KEYS, PEER_HALF).astype(BF16))


PACK_ROWS = 4


def _pack_table(tbl):
    e, d = tbl.shape
    half = d // 2
    b = tbl.astype(BF16)
    lo = lax.bitcast_convert_type(b[:, :half], jnp.uint16).astype(jnp.uint32)
    hi = lax.bitcast_convert_type(b[:, half:], jnp.uint16).astype(jnp.uint32)
    return ((hi << 16) | lo).reshape(e * PACK_ROWS, LANES)


def _unpack(slab):
    lo = pltpu.bitcast(slab << 16, F32)
    hi = pltpu.bitcast(slab & jnp.uint32(0xFFFF0000), F32)
    return lo, hi


STAGE_STEPS = 4


def _stage_rows(idx_ref, t, tbl_ref, g_ref, step=None):
    nsel = idx_ref.shape[1]
    per = nsel // STAGE_STEPS
    ms = range(nsel) if step is None else range(step * per, (step + 1) * per)
    for m in ms:
        row = pl.multiple_of(idx_ref[t, m], PACK_ROWS)
        g_ref[m * PACK_ROWS:(m + 1) * PACK_ROWS, :] = tbl_ref[pl.ds(row, PACK_ROWS), :]


def _bf16_pieces(x, n):
    pieces = []
    for _ in range(n - 1):
        p = x.astype(BF16).astype(F32)
        pieces.append(p)
        x = x - p
    pieces.append(x.astype(BF16).astype(F32))
    return pieces


def _staged_group(idx_ref, tbl_ref, bufs, tb, base, step_fn):
    for k in range(SUBLANES):
        t = base + k
        nxt = t + 1 if k + 1 < SUBLANES else jnp.minimum(t + 1, tb - 1)
        for step in range(STAGE_STEPS):
            _stage_rows(idx_ref, nxt, tbl_ref, bufs[(k + 1) % 2], step)
            step_fn(t, k, step, bufs[k % 2])


def _peer_dot_body(idx_ref, x_ref, tbl_ref, act_ref, g0_ref, g1_ref):
    tb = idx_ref.shape[0]
    nsel = idx_ref.shape[1]
    nslab = SUBLANES * SUBLANES
    lane = lax.broadcasted_iota(jnp.int32, (LANES, LANES), 1)
    col_slab = lane & (nslab - 1)
    _stage_rows(idx_ref, 0, tbl_ref, g0_ref)

    def group(i, carry):
        base = pl.multiple_of(i * SUBLANES, SUBLANES)
        pieces = _bf16_pieces(x_ref[i].reshape(nslab, LANES), LANES // nslab)
        xt = jnp.concatenate(pieces, axis=0).T.astype(BF16)
        zero = jnp.zeros_like(xt)
        acc = [jnp.zeros((nsel, LANES), F32)]
        mats = {}

        def step_fn(t, k, j, g_ref):
            lo, hi = _unpack(g_ref[pl.ds(j, nsel, stride=PACK_ROWS), :])
            mats[j] = lo.astype(BF16)
            mats[PACK_ROWS + j] = hi.astype(BF16)
            if j == PACK_ROWS - 1:
                lhs = jnp.concatenate([mats[r] for r in range(SUBLANES)], axis=1)
                rhs = jnp.concatenate(
                    [jnp.where(col_slab == SUBLANES * r + k, xt, zero) for r in range(SUBLANES)],
                    axis=0)
                acc[0] = acc[0] + jnp.dot(lhs, rhs, preferred_element_type=F32)

        _staged_group(idx_ref, tbl_ref, (g0_ref, g1_ref), tb, base, step_fn)
        out_t = acc[0].T
        act_ref[pl.ds(base, SUBLANES), :] = jnp.sum(
            out_t.reshape(LANES // SUBLANES, SUBLANES, nsel), axis=0)
        return carry

    lax.fori_loop(0, tb // SUBLANES, group, 0)


def _peer_dot(idx, xr, tbl, tb):
    t, nsel = idx.shape
    gbuf = pltpu.VMEM((nsel * PACK_ROWS, LANES), jnp.uint32)
    return pl.pallas_call(
        _peer_dot_body,
        grid=(t // tb,),
        in_specs=[pl.BlockSpec((tb, nsel), lambda i: (i, 0), memory_space=pltpu.SMEM),
                  pl.BlockSpec((tb // SUBLANES, SUBLANES, SUBLANES, LANES),
                               lambda i: (i, 0, 0, 0)),
                  pl.BlockSpec(tbl.shape, lambda i: (0, 0), pipeline_mode=pl.Buffered(1))],
        out_specs=pl.BlockSpec((tb, nsel), lambda i: (i, 0)),
        out_shape=jax.ShapeDtypeStruct((t, nsel), F32),
        scratch_shapes=[gbuf, gbuf],
        compiler_params=_params("arbitrary"),
        name="peer_dot",
    )(idx, xr, tbl)


def _peer_weights_body(act_ref, gate_ref, rep_ref, w_ref):
    a = act_ref[...]
    w = gate_ref[...] * (0.5 * a * (1.0 + lax.erf(a * (2.0 ** -0.5))))
    rep = rep_ref[...]
    w_ref[...] = sum(jnp.dot(p.astype(BF16), rep, preferred_element_type=F32)
                     for p in _bf16_pieces(w, 3))


def _peer_weights(act, gates, tm):
    t, n = act.shape
    nrow = 2 * PACK_ROWS * n
    rep = (jnp.arange(n)[:, None] == (jnp.arange(nrow) // (2 * PACK_ROWS))[None, :]).astype(BF16)
    tok = lambda width: pl.BlockSpec((tm, width), lambda i: (i, 0))
    return pl.pallas_call(
        _peer_weights_body,
        grid=(t // tm,),
        in_specs=[tok(n), tok(n), pl.BlockSpec(rep.shape, lambda i: (0, 0))],
        out_specs=tok(nrow),
        out_shape=jax.ShapeDtypeStruct((t, nrow), F32),
        compiler_params=_params("parallel"),
        name="peer_weights",
    )(act, gates, rep)


def _peer_sum_body(idx_ref, w_ref, mask_ref, tbl_ref, y_ref, g0_ref, g1_ref):
    tb = y_ref.shape[0]
    mask = mask_ref[...]
    nrow = g0_ref.shape[0] // STAGE_STEPS
    ncol = mask.shape[1] // STAGE_STEPS
    _stage_rows(idx_ref, 0, tbl_ref, g0_ref)

    def group(i, carry):
        base = pl.multiple_of(i * SUBLANES, SUBLANES)
        w_rep = w_ref[pl.ds(base, SUBLANES), :]
        acc = [None]

        def step_fn(t, k, step, g_ref):
            cols = slice(step * ncol, (step + 1) * ncol)
            wb = jnp.broadcast_to(w_rep[k:k + 1, cols], (SUBLANES, ncol)) * mask[:, cols]
            a_hi = wb.astype(BF16)
            a_lo = (wb - a_hi.astype(F32)).astype(BF16)
            lhs = jnp.concatenate([a_hi, a_lo], axis=0)
            staged = pltpu.bitcast(g_ref[step * nrow:(step + 1) * nrow, :], BF16)
            part = jnp.dot(lhs, staged, preferred_element_type=F32)
            acc[0] = part if step == 0 else acc[0] + part
            if step == STAGE_STEPS - 1:
                y_ref[t] = acc[0][:SUBLANES] + acc[0][SUBLANES:]

        _staged_group(idx_ref, tbl_ref, (g0_ref, g1_ref), tb, base, step_fn)
        return carry

    lax.fori_loop(0, tb // SUBLANES, group, 0)


def _peer_sum(idx, w, tbl, tb):
    t, nsel = idx.shape
    nrow = w.shape[1]
    c = jnp.arange(nrow)
    out_row = (c % 2) * PACK_ROWS + (c % (2 * PACK_ROWS)) // 2
    mask = (jnp.arange(SUBLANES)[:, None] == out_row[None, :]).astype(F32)
    gbuf = pltpu.VMEM((nsel * PACK_ROWS, LANES), jnp.uint32)
    const = lambda shape: pl.BlockSpec(shape, lambda i: (0, 0))
    return pl.pallas_call(
        _peer_sum_body,
        grid=(t // tb,),
        in_specs=[pl.BlockSpec((tb, nsel), lambda i: (i, 0), memory_space=pltpu.SMEM),
                  pl.BlockSpec((tb, nrow), lambda i: (i, 0)),
                  const(mask.shape),
                  pl.BlockSpec(tbl.shape, lambda i: (0, 0), pipeline_mode=pl.Buffered(1))],
        out_specs=pl.BlockSpec((tb, SUBLANES, LANES), lambda i: (i, 0, 0)),
        out_shape=jax.ShapeDtypeStruct((t, SUBLANES, LANES), F32),
        scratch_shapes=[gbuf, gbuf],
        compiler_params=_params("arbitrary"),
        name="peer_sum",
    )(idx, w, mask, tbl)


def _final_body(h_ref, y_ref, g_ref, o_ref):
    o_ref[...] = _rms(h_ref[...] + y_ref[...], g_ref[...])


def _final(h, y, gain, tm):
    t, d = h.shape
    tok = pl.BlockSpec((tm, d), lambda i: (i, 0))
    return pl.pallas_call(
        _final_body,
        grid=(t // tm,),
        in_specs=[tok, tok, pl.BlockSpec((1, d), lambda i: (0, 0))],
        out_specs=tok,
        out_shape=jax.ShapeDtypeStruct((t, d), F32),
        compiler_params=_params("parallel"),
        name="final_norm",
    )(h, y, gain.reshape(1, d))


def _pick_tile(n, pref):
    tile = min(n, pref)
    while n % tile:
        tile //= 2
    return tile


def _layer(h, mem, batch, seq, norm_mix, w_in, conv_w, dn_conv_w, dn_a_log, dn_dt_bias, dn_norm,
           w_out, norm_x, norm_mem, w_xq, w_xkv, w_xo, norm_ffn, w_pq, peer_keys,
           expert_u, expert_v):
    t, d = h.shape
    mem_len = mem.shape[0] // batch
    in_cols = w_in.shape[1]
    pad = (-in_cols) % LANES
    w_in_p = jnp.pad(w_in, ((0, 0), (0, pad))).astype(BF16)

    proj = _norm_matmul(h, norm_mix, w_in_p, _pick_tile(t, 256), "in_proj")
    y_conv, q, k, v, gb = _mixer_pre(proj, conv_w, dn_conv_w, dn_a_log, dn_dt_bias,
                                     batch, seq, _pick_tile(seq, 256))
    o_dn = _delta_rule(q, k, v, gb, batch, seq)
    h = _mixer_out(h, y_conv, o_dn, proj, dn_norm, w_out, _pick_tile(t, 512))

    kv = _norm_matmul(mem, norm_mem, w_xkv.astype(BF16), _pick_tile(mem.shape[0], 256), "kv_proj")
    h = _cross_attn(h, norm_x, w_xq, kv, w_xo, batch, seq, mem_len, _pick_tile(seq, 512))

    xn, idx, gates = _peer_route(h, norm_ffn, w_pq, peer_keys, _pick_tile(t, 512))
    tb = _pick_tile(t, 64)
    xr = xn.reshape(t // SUBLANES, SUBLANES, SUBLANES, LANES).transpose(0, 2, 1, 3)
    act = _peer_dot(idx, xr, _pack_table(expert_u), tb)
    w = _peer_weights(act, gates, _pick_tile(t, 1024))
    y = _peer_sum(idx, w, _pack_table(expert_v), tb)
    return h, y.reshape(t, d)


def kernel(x, mem, norm_mix, w_in, conv_w, dn_conv_w, dn_a_log, dn_dt_bias, dn_norm, w_out,
           norm_x, norm_mem, w_xq, w_xkv, w_xo, norm_ffn, w_pq, peer_keys, expert_u, expert_v,
           norm_final):
    batch, seq, d = x.shape
    depth = norm_mix.shape[0]
    h = x.reshape(batch * seq, d)
    mem2 = mem.reshape(batch * mem.shape[1], d)
    y = jnp.zeros_like(h)
    for layer in range(depth):
        if layer:
            h = h + y
        h, y = _layer(h, mem2, batch, seq, norm_mix[layer], w_in[layer], conv_w[layer],
                      dn_conv_w[layer], dn_a_log[layer], dn_dt_bias[layer], dn_norm[layer],
                      w_out[layer], norm_x[layer], norm_mem[layer], w_xq[layer], w_xkv[layer],
                      w_xo[layer], norm_ffn[layer], w_pq[layer], peer_keys[layer],
                      expert_u[layer], expert_v[layer])
    out = _final(h, y, norm_final, _pick_tile(batch * seq, 512))
    return out.reshape(batch, seq, d)
```

```python
import functools

import jax
import jax.numpy as jnp
from jax import lax
from jax.experimental import pallas as pl
from jax.experimental.pallas import tpu as pltpu

EPS = 1e-6
F32 = jnp.float32
BF16 = jnp.bfloat16
HIGHEST = lax.Precision.HIGHEST

LANES = 128
SUBLANES = 8
VMEM_LIMIT = 56 * 1024 * 1024

CONV_GROUP_K = 3
DN_CONV_K = 4
DN_HEADS = 4
DN_HEAD_DIM = 128
DN_CHUNK = 64
X_HEADS = 4
PEER_HEADS = 8
PEER_KEYS = 128
PEER_TOPK = 16
PEER_HALF = 64


def _params(*sem):
    return pltpu.CompilerParams(dimension_semantics=sem, vmem_limit_bytes=VMEM_LIMIT)


def _rms(x, gain):
    ms = jnp.mean(x * x, axis=-1, keepdims=True)
    return x * lax.rsqrt(ms + EPS) * gain


def _bdot(a, b):
    return jnp.dot(a.astype(BF16), b.astype(BF16), preferred_element_type=F32)


def _hdot(a, b):
    return jnp.dot(a, b, preferred_element_type=F32, precision=HIGHEST)


def _hdot_nt(a, b):
    return lax.dot_general(a, b, (((1,), (1,)), ((), ())),
                           preferred_element_type=F32, precision=HIGHEST)


def _hdot_tn(a, b):
    return lax.dot_general(a, b, (((0,), (0,)), ((), ())),
                           preferred_element_type=F32, precision=HIGHEST)


def _norm_matmul_body(x_ref, g_ref, w_ref, o_ref):
    xn = _rms(x_ref[...], g_ref[...])
    o_ref[...] = jnp.dot(xn.astype(BF16), w_ref[...], preferred_element_type=F32)


def _norm_matmul(x, gain, w, tm, name):
    t, d = x.shape
    n = w.shape[1]
    return pl.pallas_call(
        _norm_matmul_body,
        grid=(t // tm,),
        in_specs=[pl.BlockSpec((tm, d), lambda i: (i, 0)),
                  pl.BlockSpec((1, d), lambda i: (0, 0)),
                  pl.BlockSpec((d, n), lambda i: (0, 0))],
        out_specs=pl.BlockSpec((tm, n), lambda i: (i, 0)),
        out_shape=jax.ShapeDtypeStruct((t, n), F32),
        compiler_params=_params("parallel"),
        name=name,
    )(x, gain.reshape(1, d), w)


def _shift_rows(x, halo, k):
    rolled = pltpu.roll(x, k, 0)
    hal = pltpu.roll(halo, k, 0)
    row = lax.broadcasted_iota(jnp.int32, (SUBLANES, x.shape[1]), 0)
    head = jnp.where(row < k, hal, rolled[:SUBLANES])
    return jnp.concatenate([head, rolled[SUBLANES:]], axis=0)


def _causal_conv(x, halo, w_ref):
    kk = w_ref.shape[0]
    y = x * w_ref[kk - 1:kk, :]
    for j in range(1, kk):
        y = y + _shift_rows(x, halo, j) * w_ref[kk - 1 - j:kk - j, :]
    return y


def _l2norm_heads(x):
    outs = []
    for h in range(DN_HEADS):
        xh = x[:, h * DN_HEAD_DIM:(h + 1) * DN_HEAD_DIM]
        outs.append(xh * lax.rsqrt(jnp.sum(xh * xh, axis=-1, keepdims=True) + EPS))
    return jnp.concatenate(outs, axis=-1)


def _mixer_pre_body(b_ref, c_ref, h_ref, q_ref, k_ref, v_ref, t_ref,
                    ch_ref, hh_ref, qh_ref, kh_ref, vh_ref,
                    cw_ref, qw_ref, kw_ref, vw_ref, alog_ref, dtb_ref, tril_ref,
                    yc_ref, qo_ref, ko_ref, vo_ref, gb_ref):
    first = pl.program_id(1) == 0
    keep = jnp.where(first, 0.0, 1.0).astype(F32)

    u = c_ref[...] * h_ref[...]
    uh = ch_ref[...] * hh_ref[...] * keep
    yc_ref[...] = b_ref[...] * _causal_conv(u, uh, cw_ref)

    def dn_branch(x_ref, xh_ref, w_ref):
        y = _causal_conv(x_ref[...], xh_ref[...] * keep, w_ref)
        return y * jax.nn.sigmoid(y)

    qo_ref[...] = _l2norm_heads(dn_branch(q_ref, qh_ref, qw_ref))
    ko_ref[...] = _l2norm_heads(dn_branch(k_ref, kh_ref, kw_ref))
    vo_ref[...] = dn_branch(v_ref, vh_ref, vw_ref)

    tail = t_ref[...]
    beta = jax.nn.sigmoid(tail)
    g = -jnp.exp(alog_ref[...]) * jax.nn.softplus(tail + dtb_ref[...])
    gcum = _hdot(tril_ref[...], g)
    lane = lax.broadcasted_iota(jnp.int32, tail.shape, 1)
    gb_ref[...] = jnp.where(lane < DN_HEADS, beta, gcum)


def _mixer_pre(proj, conv_w, dn_conv_w, dn_a_log, dn_dt_bias, batch, seq, ts):
    t = proj.shape[0]
    cw = 512
    nblk = seq // ts
    hb = ts // SUBLANES

    def cur(col):
        return pl.BlockSpec((ts, cw if col < 7 else LANES),
                            lambda b, i, col=col: (b * nblk + i, col))

    def halo(col):
        return pl.BlockSpec(
            (SUBLANES, cw),
            lambda b, i, col=col: (jnp.maximum((b * nblk + i) * hb - 1, 0), col))

    def full(shape):
        return pl.BlockSpec(shape, lambda b, i: (0,) * len(shape))

    qw, kw, vw = (dn_conv_w[:, j * cw:(j + 1) * cw] for j in range(3))
    lane_pad = jnp.zeros((LANES - 2 * DN_HEADS,), F32)
    alog = jnp.concatenate([jnp.zeros((DN_HEADS,), F32), dn_a_log, lane_pad]).reshape(1, LANES)
    dtb = jnp.concatenate([jnp.zeros((DN_HEADS,), F32), dn_dt_bias, lane_pad]).reshape(1, LANES)
    r = jnp.arange(ts)
    tril = ((r[:, None] >= r[None, :]) &
            (r[:, None] // DN_CHUNK == r[None, :] // DN_CHUNK)).astype(F32)

    tail_spec = pl.BlockSpec((ts, LANES), lambda b, i: (b * nblk + i, 7 * cw // LANES))
    out_tok = lambda w: pl.BlockSpec((ts, w), lambda b, i: (b * nblk + i, 0))
    return pl.pallas_call(
        _mixer_pre_body,
        grid=(batch, nblk),
        in_specs=[cur(0), cur(1), cur(2), cur(3), cur(4), cur(5), tail_spec,
                  halo(1), halo(2), halo(3), halo(4), halo(5),
                  full((CONV_GROUP_K, cw)), full((DN_CONV_K, cw)), full((DN_CONV_K, cw)),
                  full((DN_CONV_K, cw)), full((1, LANES)), full((1, LANES)), full((ts, ts))],
        out_specs=[out_tok(cw), out_tok(cw), out_tok(cw), out_tok(cw), out_tok(LANES)],
        out_shape=[jax.ShapeDtypeStruct((t, cw), F32)] * 4 + [jax.ShapeDtypeStruct((t, LANES), F32)],
        compiler_params=_params("parallel", "parallel"),
        name="mixer_pre",
    )(proj, proj, proj, proj, proj, proj, proj, proj, proj, proj, proj, proj,
      conv_w, qw, kw, vw, alog, dtb, tril)


_NN = (((1,), (0,)), ((), ()))
_NT = (((1,), (1,)), ((), ()))
_TN = (((0,), (0,)), ((), ()))
DN_LOCAL_PASSES = 1
DN_SCAN_PASSES = 1


def _mm(a, b, dims, passes):
    dot = lambda x, y: lax.dot_general(x, y, dims, preferred_element_type=F32)
    if passes == 6:
        return lax.dot_general(a, b, dims, preferred_element_type=F32, precision=HIGHEST)
    a_hi = a.astype(BF16)
    b_hi = b.astype(BF16)
    if passes == 1:
        return dot(a_hi, b_hi)
    a_lo = (a - a_hi.astype(F32)).astype(BF16)
    b_lo = (b - b_hi.astype(F32)).astype(BF16)
    return dot(a_hi, b_hi) + (dot(a_hi, b_lo) + dot(a_lo, b_hi))


DN_LOCAL_CHUNKS = 4


def _dn_local_body(q_ref, k_ref, v_ref, gb_ref, u_ref, w_ref, qd_ref, kd_ref, in_ref, gl_ref):
    c = DN_CHUNK
    n = DN_HEADS * c
    p = DN_LOCAL_PASSES
    cis = range(q_ref.shape[0] // c)
    each = lambda f, *xs: [f(*a) for a in zip(*xs)]
    lanes = lambda x: jnp.broadcast_to(x, (x.shape[0], LANES))

    def stack(ref, ci):
        return jnp.concatenate([ref[ci * c:(ci + 1) * c, h * DN_HEAD_DIM:(h + 1) * DN_HEAD_DIM]
                                for h in range(DN_HEADS)], axis=0)

    def col(gb, j):
        return jnp.concatenate([lanes(gb[:, j + h:j + h + 1]) for h in range(DN_HEADS)], axis=0)

    gb = [gb_ref[ci * c:(ci + 1) * c, :] for ci in cis]
    q = [stack(q_ref, ci) * (DN_HEAD_DIM ** -0.5) for ci in cis]
    k = [stack(k_ref, ci) for ci in cis]
    v = [stack(v_ref, ci) for ci in cis]
    beta = [col(x, 0) for x in gb]
    g = [col(x, DN_HEADS) for x in gb]
    g_last = [jnp.concatenate(
        [jnp.broadcast_to(x[c - 1:c, DN_HEADS + h:DN_HEADS + h + 1], (c, LANES))
         for h in range(DN_HEADS)], axis=0) for x in gb]

    row = lax.broadcasted_iota(jnp.int32, (n, n), 0)
    cl = lax.broadcasted_iota(jnp.int32, (n, n), 1)
    shift = c.bit_length() - 1
    same_head = (row >> shift) == (cl >> shift)
    causal = same_head & (row >= cl)
    strict = same_head & (row > cl)
    eye = (row == cl).astype(F32)
    decay = [jnp.where(causal, jnp.exp(jnp.where(causal, x[:, 0:1] - x.T[0:1, :], 0.0)), 0.0)
             for x in g]
    k_beta = each(lambda a, b: a * b, k, beta)
    v_beta = each(lambda a, b: a * b, v, beta)
    a = each(lambda kb, kk, d: jnp.where(strict, _mm(kb, kk, _NT, p) * d, 0.0), k_beta, k, decay)

    t_mat = [eye - x for x in a]
    pw = [_mm(x, x, _NN, p) for x in a]
    for _ in range(4):
        t_mat = each(lambda t, w_: t + _mm(t, w_, _NN, p), t_mat, pw)
        pw = [_mm(x, x, _NN, p) for x in pw]
    t_mat = each(lambda t, w_: t + _mm(t, w_, _NN, p), t_mat, pw)

    e_g = [jnp.exp(x) for x in g]
    uw = each(lambda t, vb, kb, e: _mm(t, jnp.concatenate([vb, kb * e], axis=1), _NN, p),
              t_mat, v_beta, k_beta, e_g)
    intra = each(lambda qq, kk, d: _mm(qq, kk, _NT, p) * d, q, k, decay)
    for ci in cis:
        u_ref[ci] = uw[ci][:, :DN_HEAD_DIM]
        w_ref[ci] = uw[ci][:, DN_HEAD_DIM:]
        in_ref[ci] = intra[ci]
        qd_ref[ci] = q[ci] * e_g[ci]
        kd_ref[ci] = k[ci] * jnp.exp(g_last[ci] - g[ci])
        gl_ref[ci] = jnp.concatenate(
            [jnp.exp(g_last[ci][h * c:h * c + 1]) for h in range(DN_HEADS)] +
            [jnp.zeros((SUBLANES - DN_HEADS, LANES), F32)], axis=0)


def _dn_scan_body(u_ref, w_ref, qd_ref, kd_ref, in_ref, gl_ref, o_ref, state_ref):
    c = DN_CHUNK
    p = DN_SCAN_PASSES

    @pl.when(pl.program_id(0) == 0)
    def _():
        state_ref[...] = jnp.zeros_like(state_ref)

    for b in range(u_ref.shape[0]):
        u = u_ref[b, 0]
        w = w_ref[b, 0]
        qd = qd_ref[b, 0]
        kd = kd_ref[b, 0]
        gl = gl_ref[b, 0]
        v_new, q_state = [], []
        for h in range(DN_HEADS):
            rows = slice(h * c, (h + 1) * c)
            ws = _mm(jnp.concatenate([w[rows], qd[rows]], axis=0), state_ref[b, h], _NN, p)
            v_new.append(u[rows] - ws[:c])
            q_state.append(ws[c:])
        o = jnp.concatenate(q_state, axis=0) + _mm(in_ref[b, 0], jnp.concatenate(v_new, axis=0),
                                                   _NN, p)
        for h in range(DN_HEADS):
            rows = slice(h * c, (h + 1) * c)
            state_ref[b, h] = (state_ref[b, h] * gl[h:h + 1] +
                               _mm(kd[rows], v_new[h], _TN, p))
        o_ref[b] = jnp.concatenate([o[h * c:(h + 1) * c] for h in range(DN_HEADS)], axis=1)


def _delta_rule(q, k, v, gb, batch, seq):
    t, w = q.shape
    nc = seq // DN_CHUNK
    n = DN_HEADS * DN_CHUNK
    cps = _pick_tile(batch * nc, DN_LOCAL_CHUNKS)
    tok = lambda width: pl.BlockSpec((cps * DN_CHUNK, width), lambda i: (i, 0))
    per_chunk = lambda rows, width: pl.BlockSpec((cps, rows, width), lambda i: (i, 0, 0))
    f = lambda rows, width: jax.ShapeDtypeStruct((batch * nc, rows, width), F32)
    u, wm, qd, kd, intra, gl = pl.pallas_call(
        _dn_local_body,
        grid=(batch * nc // cps,),
        in_specs=[tok(w), tok(w), tok(w), tok(LANES)],
        out_specs=[per_chunk(n, DN_HEAD_DIM)] * 4 + [per_chunk(n, n), per_chunk(SUBLANES, LANES)],
        out_shape=[f(n, DN_HEAD_DIM)] * 4 + [f(n, n), f(SUBLANES, LANES)],
        compiler_params=_params("parallel"),
        name="dn_local",
    )(q, k, v, gb)

    seq_blk = lambda rows, width: pl.BlockSpec((batch, 1, rows, width), lambda i: (0, i, 0, 0))
    by_batch = lambda x: x.reshape(batch, nc, *x.shape[1:])
    o = pl.pallas_call(
        _dn_scan_body,
        grid=(nc,),
        in_specs=[seq_blk(n, DN_HEAD_DIM)] * 4 + [seq_blk(n, n), seq_blk(SUBLANES, LANES)],
        out_specs=pl.BlockSpec((batch, DN_CHUNK, w), lambda i: (0, i, 0)),
        out_shape=jax.ShapeDtypeStruct((batch, seq, w), F32),
        scratch_shapes=[pltpu.VMEM((batch, DN_HEADS, DN_HEAD_DIM, DN_HEAD_DIM), F32)],
        compiler_params=_params("arbitrary"),
        name="dn_scan",
    )(by_batch(u), by_batch(wm), by_batch(qd), by_batch(kd), by_batch(intra), by_batch(gl))
    return o.reshape(t, w)


def _mixer_out_body(x_ref, yc_ref, o_ref, z_ref, dng_ref, wa_ref, wb_ref, h_ref):
    o = o_ref[...]
    z = z_ref[...]
    gain = dng_ref[...]
    parts = []
    for h in range(DN_HEADS):
        sl = slice(h * DN_HEAD_DIM, (h + 1) * DN_HEAD_DIM)
        zh = z[:, sl]
        parts.append(_rms(o[:, sl], gain) * (zh * jax.nn.sigmoid(zh)))
    y_dn = jnp.concatenate(parts, axis=-1)
    h_ref[...] = (x_ref[...] + _bdot(yc_ref[...], wa_ref[...]) + _bdot(y_dn, wb_ref[...]))


def _mixer_out(x, y_conv, o_dn, proj, dn_norm, w_out, tm):
    t, d = x.shape
    cw = y_conv.shape[1]
    tok = lambda width, col=0: pl.BlockSpec((tm, width), lambda i, col=col: (i, col))
    full = lambda shape: pl.BlockSpec(shape, lambda i: (0, 0))
    return pl.pallas_call(
        _mixer_out_body,
        grid=(t // tm,),
        in_specs=[tok(d), tok(cw), tok(cw), tok(cw, 6), full((1, DN_HEAD_DIM)),
                  full((cw, d)), full((cw, d))],
        out_specs=tok(d),
        out_shape=jax.ShapeDtypeStruct((t, d), F32),
        compiler_params=_params("parallel"),
        name="mixer_out",
    )(x, y_conv, o_dn, proj, dn_norm.reshape(1, DN_HEAD_DIM),
      w_out[:cw].astype(BF16), w_out[cw:].astype(BF16))


def _cross_attn_body(h_ref, g_ref, wq_ref, kv_ref, wo_ref, o_ref):
    h_in = h_ref[...]
    d = h_in.shape[1]
    dh = d // X_HEADS
    q = _bdot(_rms(h_in, g_ref[...]), wq_ref[...])
    outs = []
    for hd in range(X_HEADS):
        qh = q[:, hd * dh:(hd + 1) * dh].astype(BF16)
        kh = kv_ref[:, hd * dh:(hd + 1) * dh]
        vh = kv_ref[:, d + hd * dh:d + (hd + 1) * dh]
        s = lax.dot_general(qh, kh, (((1,), (1,)), ((), ())),
                            preferred_element_type=F32) * (dh ** -0.5)
        s = s - jnp.max(s, axis=-1, keepdims=True)
        e = jnp.exp(s)
        p = e / jnp.sum(e, axis=-1, keepdims=True)
        outs.append(jnp.dot(p.astype(BF16), vh, preferred_element_type=F32))
    o = jnp.concatenate(outs, axis=-1)
    o_ref[...] = h_in + _bdot(o, wo_ref[...])


def _cross_attn(h, gain, w_q, kv, w_o, batch, seq, mem_len, tm):
    t, d = h.shape
    nblk = seq // tm
    tok = pl.BlockSpec((tm, d), lambda b, i: (b * nblk + i, 0))
    full = lambda shape: pl.BlockSpec(shape, lambda b, i: (0, 0))
    return pl.pallas_call(
        _cross_attn_body,
        grid=(batch, nblk),
        in_specs=[tok, full((1, d)), full((d, d)),
                  pl.BlockSpec((mem_len, 2 * d), lambda b, i: (b, 0)), full((d, d))],
        out_specs=tok,
        out_shape=jax.ShapeDtypeStruct((t, d), F32),
        compiler_params=_params("parallel", "parallel"),
        name="cross_attn",
    )(h, gain.reshape(1, d), w_q.astype(BF16), kv.astype(BF16), w_o.astype(BF16))


def _top16_rows(s, payload=None):
    n = s.shape[0]
    row = lax.broadcasted_iota(jnp.int32, s.shape, 0).astype(F32)
    vals, pays = [], []
    for _ in range(PEER_TOPK):
        m = jnp.max(s, axis=0, keepdims=True)
        pos = jnp.min(jnp.where(s == m, row, float(n)), axis=0, keepdims=True)
        hit = row == pos
        vals.append(m)
        if payload is None:
            pays.append(pos)
        else:
            pays.append(jnp.sum(jnp.where(hit, payload, 0.0), axis=0, keepdims=True))
        s = jnp.where(hit, -jnp.inf, s)
    return jnp.concatenate(vals, axis=0), jnp.concatenate(pays, axis=0)


def _pair_candidates(v1, v2, pad):
    sub = lax.broadcasted_iota(jnp.int32, (SUBLANES, v1.shape[1]), 0)
    row = lambda v, a: jnp.broadcast_to(v[a:a + 1], sub.shape)
    lo8 = v2[:SUBLANES]
    blocks = [
        row(v1, 0) + lo8,
        row(v1, 0) + v2[SUBLANES:],
        row(v1, 1) + lo8,
        jnp.where(sub < 5, row(v1, 2) + lo8, pad),
        jnp.where(sub < 7,
                  jnp.where(sub < 4, row(v1, 3), row(v1, 4)) +
                  jnp.where(sub < 4, lo8, pltpu.roll(lo8, 4, 0)), pad),
        jnp.where(sub < 6,
                  jnp.where(sub < 2, row(v1, 5), jnp.where(sub < 4, row(v1, 6), row(v1, 7))) +
                  jnp.where((sub & 1) == 0, row(v2, 0), row(v2, 1)), pad),
        v1[SUBLANES:] + row(v2, 0),
    ]
    return jnp.concatenate(blocks, axis=0)


def _peer_route_body(h_ref, g_ref, wq_ref, keys_ref, xn_ref, idx_ref, gate_ref,
                     sc_ref, sel_ref, gt_ref):
    xn = _rms(h_ref[...], g_ref[...])
    xn_ref[...] = xn
    pq = _bdot(xn, wq_ref[...])
    ngrp = pq.shape[0] // LANES
    for j in range(2 * PEER_HEADS):
        qs = pq[:, j * PEER_HALF:(j + 1) * PEER_HALF].astype(BF16)
        sc = lax.dot_general(keys_ref[j], qs, (((1,), (1,)), ((), ())),
                             preferred_element_type=F32)
        for g in range(ngrp):
            sc_ref[j, g] = sc[:, g * LANES:(g + 1) * LANES]

    def head(h, carry):
        for g in range(ngrp):
            s1, i1 = _top16_rows(sc_ref[2 * h, g])
            s2, i2 = _top16_rows(sc_ref[2 * h + 1, g])
            cand = _pair_candidates(s1, s2, -jnp.inf)
            cand_idx = _pair_candidates(i1 * float(PEER_KEYS), i2, 0.0)
            best, sel = _top16_rows(cand, cand_idx)
            e = jnp.exp(best - jnp.max(best, axis=0, keepdims=True))
            sel_ref[g, h] = sel
            gt_ref[g, h] = e / jnp.sum(e, axis=0, keepdims=True)
        return carry

    lax.fori_loop(0, PEER_HEADS, head, 0)

    nsel = PEER_HEADS * PEER_TOPK
    for g in range(ngrp):
        rows = slice(g * LANES, (g + 1) * LANES)
        idx_ref[rows, :] = (sel_ref[g].reshape(nsel, LANES).T * float(PACK_ROWS)).astype(jnp.int32)
        gate_ref[rows, :] = gt_ref[g].reshape(nsel, LANES).T


def _peer_route(h, gain, w_pq, peer_keys, tm):
    t, d = h.shape
    nk = PEER_HEADS * 2
    ngrp = tm // LANES
    tok = lambda width: pl.BlockSpec((tm, width), lambda i: (i, 0))
    full = lambda shape: pl.BlockSpec(shape, lambda i: (0,) * len(shape))
    return pl.pallas_call(
        _peer_route_body,
        grid=(t // tm,),
        in_specs=[tok(d), full((1, d)), full((d, d)), full((nk, PEER_KEYS, PEER_HALF))],
        out_specs=[tok(d), tok(LANES), tok(LANES)],
        out_shape=[jax.ShapeDtypeStruct((t, d), F32),
                   jax.ShapeDtypeStruct((t, LANES), jnp.int32),
                   jax.ShapeDtypeStruct((t, LANES), F32)],
        scratch_shapes=[pltpu.VMEM((nk, ngrp, PEER_KEYS, LANES), F32),
                        pltpu.VMEM((ngrp, PEER_HEADS, PEER_TOPK, LANES), F32),
                        pltpu.VMEM((ngrp, PEER_HEADS, PEER_TOPK, LANES), F32)],
        compiler_params=_params("parallel"),
        name="peer_route",
    )(h, gain.reshape(1, d), w_pq.astype(BF16),
      peer_keys.reshape(nk, PEER_KEYS, PEER_HALF).astype(BF16))


PACK_ROWS = 4


def _pack_table(tbl):
    e, d = tbl.shape
    half = d // 2
    b = tbl.astype(BF16)
    lo = lax.bitcast_convert_type(b[:, :half], jnp.uint16).astype(jnp.uint32)
    hi = lax.bitcast_convert_type(b[:, half:], jnp.uint16).astype(jnp.uint32)
    return ((hi << 16) | lo).reshape(e * PACK_ROWS, LANES)


def _unpack(slab):
    lo = pltpu.bitcast(slab << 16, F32)
    hi = pltpu.bitcast(slab & jnp.uint32(0xFFFF0000), F32)
    return lo, hi


STAGE_STEPS = 4
PEER_SEL = PEER_HEADS * PEER_TOPK
IDX_REFS = 8


def _split_idx(idx):
    return tuple(idx[:, r::IDX_REFS].reshape(-1) for r in range(IDX_REFS))


def _idx_specs(tb):
    n = tb * PEER_SEL // IDX_REFS
    return [pl.BlockSpec((n,), lambda i: (i,), memory_space=pltpu.SMEM) for _ in range(IDX_REFS)]


def _stage_rows(idx_refs, t, tbl_ref, g_ref, step=None):
    ncol = PEER_SEL // IDX_REFS
    per = PEER_SEL // STAGE_STEPS
    ms = range(PEER_SEL) if step is None else range(step * per, (step + 1) * per)
    offs = {}
    for m in ms:
        q, r = divmod(m, IDX_REFS)
        if q not in offs:
            offs[q] = t * ncol + q
        row = pl.multiple_of(idx_refs[r][offs[q]], PACK_ROWS)
        g_ref[m * PACK_ROWS:(m + 1) * PACK_ROWS, :] = tbl_ref[pl.ds(row, PACK_ROWS), :]


def _bf16_pieces(x, n):
    pieces = []
    for _ in range(n - 1):
        p = x.astype(BF16).astype(F32)
        pieces.append(p)
        x = x - p
    pieces.append(x.astype(BF16).astype(F32))
    return pieces


def _staged_group(idx_ref, tbl_ref, bufs, tb, base, step_fn):
    for k in range(SUBLANES):
        t = base + k
        nxt = t + 1 if k + 1 < SUBLANES else jnp.minimum(t + 1, tb - 1)
        for step in range(STAGE_STEPS):
            _stage_rows(idx_ref, nxt, tbl_ref, bufs[(k + 1) % 2], step)
            step_fn(t, k, step, bufs[k % 2])


def _peer_dot_body(*refs):
    idx_ref = refs[:IDX_REFS]
    x_ref, tbl_ref, act_ref, g0_ref, g1_ref = refs[IDX_REFS:]
    tb, nsel = act_ref.shape
    nslab = SUBLANES * SUBLANES
    lane = lax.broadcasted_iota(jnp.int32, (LANES, LANES), 1)
    col_slab = lane & (nslab - 1)
    _stage_rows(idx_ref, 0, tbl_ref, g0_ref)

    def group(i, carry):
        base = pl.multiple_of(i * SUBLANES, SUBLANES)
        pieces = _bf16_pieces(x_ref[i].reshape(nslab, LANES), LANES // nslab)
        xt = jnp.concatenate(pieces, axis=0).T.astype(BF16)
        zero = jnp.zeros_like(xt)
        acc = [jnp.zeros((nsel, LANES), F32)]

        def step_fn(t, k, j, g_ref):
            lo, hi = _unpack(g_ref[pl.ds(j, nsel, stride=PACK_ROWS), :])
            lhs = jnp.concatenate([lo.astype(BF16), hi.astype(BF16)], axis=1)
            rhs = jnp.concatenate(
                [jnp.where(col_slab == SUBLANES * r + k, xt, zero) for r in (j, PACK_ROWS + j)],
                axis=0)
            acc[0] = acc[0] + jnp.dot(lhs, rhs, preferred_element_type=F32)

        _staged_group(idx_ref, tbl_ref, (g0_ref, g1_ref), tb, base, step_fn)
        out_t = acc[0].T
        act_ref[pl.ds(base, SUBLANES), :] = jnp.sum(
            out_t.reshape(LANES // SUBLANES, SUBLANES, nsel), axis=0)
        return carry

    lax.fori_loop(0, tb // SUBLANES, group, 0)


def _peer_dot(idx_split, xr, tbl, tb):
    t = xr.shape[0] * SUBLANES
    nsel = PEER_SEL
    gbuf = pltpu.VMEM((nsel * PACK_ROWS, LANES), jnp.uint32)
    return pl.pallas_call(
        _peer_dot_body,
        grid=(t // tb,),
        in_specs=_idx_specs(tb) + [
            pl.BlockSpec((tb // SUBLANES, SUBLANES, SUBLANES, LANES), lambda i: (i, 0, 0, 0)),
            pl.BlockSpec(tbl.shape, lambda i: (0, 0), pipeline_mode=pl.Buffered(1))],
        out_specs=pl.BlockSpec((tb, nsel), lambda i: (i, 0)),
        out_shape=jax.ShapeDtypeStruct((t, nsel), F32),
        scratch_shapes=[gbuf, gbuf],
        compiler_params=_params("arbitrary"),
        name="peer_dot",
    )(*idx_split, xr, tbl)


def _peer_weights_body(act_ref, gate_ref, rep_ref, w_ref):
    a = act_ref[...]
    w = gate_ref[...] * (0.5 * a * (1.0 + lax.erf(a * (2.0 ** -0.5))))
    rep = rep_ref[...]
    w_ref[...] = sum(jnp.dot(p.astype(BF16), rep, preferred_element_type=F32)
                     for p in _bf16_pieces(w, 3))


def _peer_weights(act, gates, tm):
    t, n = act.shape
    nrow = 2 * PACK_ROWS * n
    rep = (jnp.arange(n)[:, None] == (jnp.arange(nrow) // (2 * PACK_ROWS))[None, :]).astype(BF16)
    tok = lambda width: pl.BlockSpec((tm, width), lambda i: (i, 0))
    return pl.pallas_call(
        _peer_weights_body,
        grid=(t // tm,),
        in_specs=[tok(n), tok(n), pl.BlockSpec(rep.shape, lambda i: (0, 0))],
        out_specs=tok(nrow),
        out_shape=jax.ShapeDtypeStruct((t, nrow), F32),
        compiler_params=_params("parallel"),
        name="peer_weights",
    )(act, gates, rep)


def _peer_sum_body(*refs):
    idx_ref = refs[:IDX_REFS]
    w_ref, mask_ref, tbl_ref, y_ref, g0_ref, g1_ref = refs[IDX_REFS:]
    tb = y_ref.shape[0]
    mask = mask_ref[...]
    nrow = g0_ref.shape[0] // STAGE_STEPS
    ncol = mask.shape[1] // STAGE_STEPS
    _stage_rows(idx_ref, 0, tbl_ref, g0_ref)

    def group(i, carry):
        base = pl.multiple_of(i * SUBLANES, SUBLANES)
        w_rep = w_ref[pl.ds(base, SUBLANES), :]
        acc = [None]

        def step_fn(t, k, step, g_ref):
            cols = slice(step * ncol, (step + 1) * ncol)
            wb = jnp.broadcast_to(w_rep[k:k + 1, cols], (SUBLANES, ncol)) * mask[:, cols]
            a_hi = wb.astype(BF16)
            a_lo = (wb - a_hi.astype(F32)).astype(BF16)
            lhs = jnp.concatenate([a_hi, a_lo], axis=0)
            staged = pltpu.bitcast(g_ref[step * nrow:(step + 1) * nrow, :], BF16)
            part = jnp.dot(lhs, staged, preferred_element_type=F32)
            acc[0] = part if step == 0 else acc[0] + part
            if step == STAGE_STEPS - 1:
                y_ref[t] = acc[0][:SUBLANES] + acc[0][SUBLANES:]

        _staged_group(idx_ref, tbl_ref, (g0_ref, g1_ref), tb, base, step_fn)
        return carry

    lax.fori_loop(0, tb // SUBLANES, group, 0)


def _peer_sum(idx_split, w, tbl, tb):
    t, nrow = w.shape
    nsel = PEER_SEL
    c = jnp.arange(nrow)
    out_row = (c % 2) * PACK_ROWS + (c % (2 * PACK_ROWS)) // 2
    mask = (jnp.arange(SUBLANES)[:, None] == out_row[None, :]).astype(F32)
    gbuf = pltpu.VMEM((nsel * PACK_ROWS, LANES), jnp.uint32)
    const = lambda shape: pl.BlockSpec(shape, lambda i: (0, 0))
    return pl.pallas_call(
        _peer_sum_body,
        grid=(t // tb,),
        in_specs=_idx_specs(tb) + [
            pl.BlockSpec((tb, nrow), lambda i: (i, 0)),
            const(mask.shape),
            pl.BlockSpec(tbl.shape, lambda i: (0, 0), pipeline_mode=pl.Buffered(1))],
        out_specs=pl.BlockSpec((tb, SUBLANES, LANES), lambda i: (i, 0, 0)),
        out_shape=jax.ShapeDtypeStruct((t, SUBLANES, LANES), F32),
        scratch_shapes=[gbuf, gbuf],
        compiler_params=_params("arbitrary"),
        name="peer_sum",
    )(*idx_split, w, mask, tbl)


def _final_body(h_ref, y_ref, g_ref, o_ref):
    o_ref[...] = _rms(h_ref[...] + y_ref[...], g_ref[...])


def _final(h, y, gain, tm):
    t, d = h.shape
    tok = pl.BlockSpec((tm, d), lambda i: (i, 0))
    return pl.pallas_call(
        _final_body,
        grid=(t // tm,),
        in_specs=[tok, tok, pl.BlockSpec((1, d), lambda i: (0, 0))],
        out_specs=tok,
        out_shape=jax.ShapeDtypeStruct((t, d), F32),
        compiler_params=_params("parallel"),
        name="final_norm",
    )(h, y, gain.reshape(1, d))


def _pick_tile(n, pref):
    tile = min(n, pref)
    while n % tile:
        tile //= 2
    return tile


def _layer(h, mem, batch, seq, norm_mix, w_in, conv_w, dn_conv_w, dn_a_log, dn_dt_bias, dn_norm,
           w_out, norm_x, norm_mem, w_xq, w_xkv, w_xo, norm_ffn, w_pq, peer_keys,
           expert_u, expert_v):
    t, d = h.shape
    mem_len = mem.shape[0] // batch
    in_cols = w_in.shape[1]
    pad = (-in_cols) % LANES
    w_in_p = jnp.pad(w_in, ((0, 0), (0, pad))).astype(BF16)

    proj = _norm_matmul(h, norm_mix, w_in_p, _pick_tile(t, 256), "in_proj")
    y_conv, q, k, v, gb = _mixer_pre(proj, conv_w, dn_conv_w, dn_a_log, dn_dt_bias,
                                     batch, seq, _pick_tile(seq, 256))
    o_dn = _delta_rule(q, k, v, gb, batch, seq)
    h = _mixer_out(h, y_conv, o_dn, proj, dn_norm, w_out, _pick_tile(t, 512))

    kv = _norm_matmul(mem, norm_mem, w_xkv.astype(BF16), _pick_tile(mem.shape[0], 256), "kv_proj")
    h = _cross_attn(h, norm_x, w_xq, kv, w_xo, batch, seq, mem_len, _pick_tile(seq, 512))

    xn, idx, gates = _peer_route(h, norm_ffn, w_pq, peer_keys, _pick_tile(t, 512))
    tb = _pick_tile(t, 64)
    xr = xn.reshape(t // SUBLANES, SUBLANES, SUBLANES, LANES).transpose(0, 2, 1, 3)
    idx_split = _split_idx(idx)
    act = _peer_dot(idx_split, xr, _pack_table(expert_u), tb)
    w = _peer_weights(act, gates, _pick_tile(t, 1024))
    y = _peer_sum(idx_split, w, _pack_table(expert_v), tb)
    return h, y.reshape(t, d)


def kernel(x, mem, norm_mix, w_in, conv_w, dn_conv_w, dn_a_log, dn_dt_bias, dn_norm, w_out,
           norm_x, norm_mem, w_xq, w_xkv, w_xo, norm_ffn, w_pq, peer_keys, expert_u, expert_v,
           norm_final):
    batch, seq, d = x.shape
    depth = norm_mix.shape[0]
    h = x.reshape(batch * seq, d)
    mem2 = mem.reshape(batch * mem.shape[1], d)
    y = jnp.zeros_like(h)
    for layer in range(depth):
        if layer:
            h = h + y
        h, y = _layer(h, mem2, batch, seq, norm_mix[layer], w_in[layer], conv_w[layer],
                      dn_conv_w[layer], dn_a_log[layer], dn_dt_bias[layer], dn_norm[layer],
                      w_out[layer], norm_x[layer], norm_mem[layer], w_xq[layer], w_xkv[layer],
                      w_xo[layer], norm_ffn[layer], w_pq[layer], peer_keys[layer],
                      expert_u[layer], expert_v[layer])
    out = _final(h, y, norm_final, _pick_tile(batch * seq, 512))
    return out.reshape(batch, seq, d)
```

```python
import functools

import jax
import jax.numpy as jnp
from jax import lax
from jax.experimental import pallas as pl
from jax.experimental.pallas import tpu as pltpu

EPS = 1e-6
F32 = jnp.float32
BF16 = jnp.bfloat16
HIGHEST = lax.Precision.HIGHEST

LANES = 128
SUBLANES = 8
VMEM_LIMIT = 56 * 1024 * 1024

CONV_GROUP_K = 3
DN_CONV_K = 4
DN_HEADS = 4
DN_HEAD_DIM = 128
DN_CHUNK = 64
X_HEADS = 4
PEER_HEADS = 8
PEER_KEYS = 128
PEER_TOPK = 16
PEER_HALF = 64


def _params(*sem):
    return pltpu.CompilerParams(dimension_semantics=sem, vmem_limit_bytes=VMEM_LIMIT)


def _rms(x, gain):
    ms = jnp.mean(x * x, axis=-1, keepdims=True)
    return x * lax.rsqrt(ms + EPS) * gain


def _bdot(a, b):
    return jnp.dot(a.astype(BF16), b.astype(BF16), preferred_element_type=F32)


def _hdot(a, b):
    return jnp.dot(a, b, preferred_element_type=F32, precision=HIGHEST)


def _hdot_nt(a, b):
    return lax.dot_general(a, b, (((1,), (1,)), ((), ())),
                           preferred_element_type=F32, precision=HIGHEST)


def _hdot_tn(a, b):
    return lax.dot_general(a, b, (((0,), (0,)), ((), ())),
                           preferred_element_type=F32, precision=HIGHEST)


def _norm_matmul_body(x_ref, g_ref, w_ref, o_ref):
    xn = _rms(x_ref[...], g_ref[...])
    o_ref[...] = jnp.dot(xn.astype(BF16), w_ref[...], preferred_element_type=F32)


def _norm_matmul(x, gain, w, tm, name):
    t, d = x.shape
    n = w.shape[1]
    return pl.pallas_call(
        _norm_matmul_body,
        grid=(t // tm,),
        in_specs=[pl.BlockSpec((tm, d), lambda i: (i, 0)),
                  pl.BlockSpec((1, d), lambda i: (0, 0)),
                  pl.BlockSpec((d, n), lambda i: (0, 0))],
        out_specs=pl.BlockSpec((tm, n), lambda i: (i, 0)),
        out_shape=jax.ShapeDtypeStruct((t, n), F32),
        compiler_params=_params("parallel"),
        name=name,
    )(x, gain.reshape(1, d), w)


def _shift_rows(x, halo, k):
    rolled = pltpu.roll(x, k, 0)
    hal = pltpu.roll(halo, k, 0)
    row = lax.broadcasted_iota(jnp.int32, (SUBLANES, x.shape[1]), 0)
    head = jnp.where(row < k, hal, rolled[:SUBLANES])
    return jnp.concatenate([head, rolled[SUBLANES:]], axis=0)


def _causal_conv(x, halo, w_ref):
    kk = w_ref.shape[0]
    y = x * w_ref[kk - 1:kk, :]
    for j in range(1, kk):
        y = y + _shift_rows(x, halo, j) * w_ref[kk - 1 - j:kk - j, :]
    return y


def _l2norm_heads(x):
    outs = []
    for h in range(DN_HEADS):
        xh = x[:, h * DN_HEAD_DIM:(h + 1) * DN_HEAD_DIM]
        outs.append(xh * lax.rsqrt(jnp.sum(xh * xh, axis=-1, keepdims=True) + EPS))
    return jnp.concatenate(outs, axis=-1)


def _mixer_pre_body(b_ref, c_ref, h_ref, q_ref, k_ref, v_ref, t_ref,
                    ch_ref, hh_ref, qh_ref, kh_ref, vh_ref,
                    cw_ref, qw_ref, kw_ref, vw_ref, alog_ref, dtb_ref, tril_ref,
                    yc_ref, qo_ref, ko_ref, vo_ref, gb_ref):
    first = pl.program_id(1) == 0
    keep = jnp.where(first, 0.0, 1.0).astype(F32)

    u = c_ref[...] * h_ref[...]
    uh = ch_ref[...] * hh_ref[...] * keep
    yc_ref[...] = b_ref[...] * _causal_conv(u, uh, cw_ref)

    def dn_branch(x_ref, xh_ref, w_ref):
        y = _causal_conv(x_ref[...], xh_ref[...] * keep, w_ref)
        return y * jax.nn.sigmoid(y)

    qo_ref[...] = _l2norm_heads(dn_branch(q_ref, qh_ref, qw_ref))
    ko_ref[...] = _l2norm_heads(dn_branch(k_ref, kh_ref, kw_ref))
    vo_ref[...] = dn_branch(v_ref, vh_ref, vw_ref)

    tail = t_ref[...]
    beta = jax.nn.sigmoid(tail)
    g = -jnp.exp(alog_ref[...]) * jax.nn.softplus(tail + dtb_ref[...])
    gcum = _hdot(tril_ref[...], g)
    lane = lax.broadcasted_iota(jnp.int32, tail.shape, 1)
    gb_ref[...] = jnp.where(lane < DN_HEADS, beta, gcum)


def _mixer_pre(proj, conv_w, dn_conv_w, dn_a_log, dn_dt_bias, batch, seq, ts):
    t = proj.shape[0]
    cw = 512
    nblk = seq // ts
    hb = ts // SUBLANES

    def cur(col):
        return pl.BlockSpec((ts, cw if col < 7 else LANES),
                            lambda b, i, col=col: (b * nblk + i, col))

    def halo(col):
        return pl.BlockSpec(
            (SUBLANES, cw),
            lambda b, i, col=col: (jnp.maximum((b * nblk + i) * hb - 1, 0), col))

    def full(shape):
        return pl.BlockSpec(shape, lambda b, i: (0,) * len(shape))

    qw, kw, vw = (dn_conv_w[:, j * cw:(j + 1) * cw] for j in range(3))
    lane_pad = jnp.zeros((LANES - 2 * DN_HEADS,), F32)
    alog = jnp.concatenate([jnp.zeros((DN_HEADS,), F32), dn_a_log, lane_pad]).reshape(1, LANES)
    dtb = jnp.concatenate([jnp.zeros((DN_HEADS,), F32), dn_dt_bias, lane_pad]).reshape(1, LANES)
    r = jnp.arange(ts)
    tril = ((r[:, None] >= r[None, :]) &
            (r[:, None] // DN_CHUNK == r[None, :] // DN_CHUNK)).astype(F32)

    tail_spec = pl.BlockSpec((ts, LANES), lambda b, i: (b * nblk + i, 7 * cw // LANES))
    out_tok = lambda w: pl.BlockSpec((ts, w), lambda b, i: (b * nblk + i, 0))
    return pl.pallas_call(
        _mixer_pre_body,
        grid=(batch, nblk),
        in_specs=[cur(0), cur(1), cur(2), cur(3), cur(4), cur(5), tail_spec,
                  halo(1), halo(2), halo(3), halo(4), halo(5),
                  full((CONV_GROUP_K, cw)), full((DN_CONV_K, cw)), full((DN_CONV_K, cw)),
                  full((DN_CONV_K, cw)), full((1, LANES)), full((1, LANES)), full((ts, ts))],
        out_specs=[out_tok(cw), out_tok(cw), out_tok(cw), out_tok(cw), out_tok(LANES)],
        out_shape=[jax.ShapeDtypeStruct((t, cw), F32)] * 4 + [jax.ShapeDtypeStruct((t, LANES), F32)],
        compiler_params=_params("parallel", "parallel"),
        name="mixer_pre",
    )(proj, proj, proj, proj, proj, proj, proj, proj, proj, proj, proj, proj,
      conv_w, qw, kw, vw, alog, dtb, tril)


_NN = (((1,), (0,)), ((), ()))
_NT = (((1,), (1,)), ((), ()))
_TN = (((0,), (0,)), ((), ()))
DN_LOCAL_PASSES = 1
DN_SCAN_PASSES = 1


def _mm(a, b, dims, passes):
    dot = lambda x, y: lax.dot_general(x, y, dims, preferred_element_type=F32)
    if passes == 6:
        return lax.dot_general(a, b, dims, preferred_element_type=F32, precision=HIGHEST)
    a_hi = a.astype(BF16)
    b_hi = b.astype(BF16)
    if passes == 1:
        return dot(a_hi, b_hi)
    a_lo = (a - a_hi.astype(F32)).astype(BF16)
    b_lo = (b - b_hi.astype(F32)).astype(BF16)
    return dot(a_hi, b_hi) + (dot(a_hi, b_lo) + dot(a_lo, b_hi))


DN_LOCAL_CHUNKS = 4


def _dn_local_body(q_ref, k_ref, v_ref, gb_ref, u_ref, w_ref, qd_ref, kd_ref, in_ref, gl_ref):
    c = DN_CHUNK
    n = DN_HEADS * c
    p = DN_LOCAL_PASSES
    cis = range(q_ref.shape[0] // c)
    each = lambda f, *xs: [f(*a) for a in zip(*xs)]
    lanes = lambda x: jnp.broadcast_to(x, (x.shape[0], LANES))

    def stack(ref, ci):
        return jnp.concatenate([ref[ci * c:(ci + 1) * c, h * DN_HEAD_DIM:(h + 1) * DN_HEAD_DIM]
                                for h in range(DN_HEADS)], axis=0)

    def col(gb, j):
        return jnp.concatenate([lanes(gb[:, j + h:j + h + 1]) for h in range(DN_HEADS)], axis=0)

    gb = [gb_ref[ci * c:(ci + 1) * c, :] for ci in cis]
    q = [stack(q_ref, ci) * (DN_HEAD_DIM ** -0.5) for ci in cis]
    k = [stack(k_ref, ci) for ci in cis]
    v = [stack(v_ref, ci) for ci in cis]
    beta = [col(x, 0) for x in gb]
    g = [col(x, DN_HEADS) for x in gb]
    g_last = [jnp.concatenate(
        [jnp.broadcast_to(x[c - 1:c, DN_HEADS + h:DN_HEADS + h + 1], (c, LANES))
         for h in range(DN_HEADS)], axis=0) for x in gb]

    row = lax.broadcasted_iota(jnp.int32, (n, n), 0)
    cl = lax.broadcasted_iota(jnp.int32, (n, n), 1)
    shift = c.bit_length() - 1
    same_head = (row >> shift) == (cl >> shift)
    causal = same_head & (row >= cl)
    strict = same_head & (row > cl)
    eye = (row == cl).astype(F32)
    decay = [jnp.where(causal, jnp.exp(jnp.where(causal, x[:, 0:1] - x.T[0:1, :], 0.0)), 0.0)
             for x in g]
    k_beta = each(lambda a, b: a * b, k, beta)
    v_beta = each(lambda a, b: a * b, v, beta)
    a = each(lambda kb, kk, d: jnp.where(strict, _mm(kb, kk, _NT, p) * d, 0.0), k_beta, k, decay)

    t_mat = [eye - x for x in a]
    pw = [_mm(x, x, _NN, p) for x in a]
    for _ in range(4):
        t_mat = each(lambda t, w_: t + _mm(t, w_, _NN, p), t_mat, pw)
        pw = [_mm(x, x, _NN, p) for x in pw]
    t_mat = each(lambda t, w_: t + _mm(t, w_, _NN, p), t_mat, pw)

    e_g = [jnp.exp(x) for x in g]
    uw = each(lambda t, vb, kb, e: _mm(t, jnp.concatenate([vb, kb * e], axis=1), _NN, p),
              t_mat, v_beta, k_beta, e_g)
    intra = each(lambda qq, kk, d: _mm(qq, kk, _NT, p) * d, q, k, decay)
    for ci in cis:
        u_ref[ci] = uw[ci][:, :DN_HEAD_DIM]
        w_ref[ci] = uw[ci][:, DN_HEAD_DIM:]
        in_ref[ci] = intra[ci]
        qd_ref[ci] = q[ci] * e_g[ci]
        kd_ref[ci] = k[ci] * jnp.exp(g_last[ci] - g[ci])
        gl_ref[ci] = jnp.concatenate(
            [jnp.exp(g_last[ci][h * c:h * c + 1]) for h in range(DN_HEADS)] +
            [jnp.zeros((SUBLANES - DN_HEADS, LANES), F32)], axis=0)


def _dn_scan_body(u_ref, w_ref, qd_ref, kd_ref, in_ref, gl_ref, o_ref, state_ref):
    c = DN_CHUNK
    p = DN_SCAN_PASSES

    @pl.when(pl.program_id(0) == 0)
    def _():
        state_ref[...] = jnp.zeros_like(state_ref)

    for b in range(u_ref.shape[0]):
        u = u_ref[b, 0]
        w = w_ref[b, 0]
        qd = qd_ref[b, 0]
        kd = kd_ref[b, 0]
        gl = gl_ref[b, 0]
        v_new, q_state = [], []
        for h in range(DN_HEADS):
            rows = slice(h * c, (h + 1) * c)
            ws = _mm(jnp.concatenate([w[rows], qd[rows]], axis=0), state_ref[b, h], _NN, p)
            v_new.append(u[rows] - ws[:c])
            q_state.append(ws[c:])
        o = jnp.concatenate(q_state, axis=0) + _mm(in_ref[b, 0], jnp.concatenate(v_new, axis=0),
                                                   _NN, p)
        for h in range(DN_HEADS):
            rows = slice(h * c, (h + 1) * c)
            state_ref[b, h] = (state_ref[b, h] * gl[h:h + 1] +
                               _mm(kd[rows], v_new[h], _TN, p))
        o_ref[b] = jnp.concatenate([o[h * c:(h + 1) * c] for h in range(DN_HEADS)], axis=1)


def _delta_rule(q, k, v, gb, batch, seq):
    t, w = q.shape
    nc = seq // DN_CHUNK
    n = DN_HEADS * DN_CHUNK
    cps = _pick_tile(batch * nc, DN_LOCAL_CHUNKS)
    tok = lambda width: pl.BlockSpec((cps * DN_CHUNK, width), lambda i: (i, 0))
    per_chunk = lambda rows, width: pl.BlockSpec((cps, rows, width), lambda i: (i, 0, 0))
    f = lambda rows, width: jax.ShapeDtypeStruct((batch * nc, rows, width), F32)
    u, wm, qd, kd, intra, gl = pl.pallas_call(
        _dn_local_body,
        grid=(batch * nc // cps,),
        in_specs=[tok(w), tok(w), tok(w), tok(LANES)],
        out_specs=[per_chunk(n, DN_HEAD_DIM)] * 4 + [per_chunk(n, n), per_chunk(SUBLANES, LANES)],
        out_shape=[f(n, DN_HEAD_DIM)] * 4 + [f(n, n), f(SUBLANES, LANES)],
        compiler_params=_params("parallel"),
        name="dn_local",
    )(q, k, v, gb)

    seq_blk = lambda rows, width: pl.BlockSpec((batch, 1, rows, width), lambda i: (0, i, 0, 0))
    by_batch = lambda x: x.reshape(batch, nc, *x.shape[1:])
    o = pl.pallas_call(
        _dn_scan_body,
        grid=(nc,),
        in_specs=[seq_blk(n, DN_HEAD_DIM)] * 4 + [seq_blk(n, n), seq_blk(SUBLANES, LANES)],
        out_specs=pl.BlockSpec((batch, DN_CHUNK, w), lambda i: (0, i, 0)),
        out_shape=jax.ShapeDtypeStruct((batch, seq, w), F32),
        scratch_shapes=[pltpu.VMEM((batch, DN_HEADS, DN_HEAD_DIM, DN_HEAD_DIM), F32)],
        compiler_params=_params("arbitrary"),
        name="dn_scan",
    )(by_batch(u), by_batch(wm), by_batch(qd), by_batch(kd), by_batch(intra), by_batch(gl))
    return o.reshape(t, w)


def _mixer_out_body(x_ref, yc_ref, o_ref, z_ref, dng_ref, wa_ref, wb_ref, h_ref):
    o = o_ref[...]
    z = z_ref[...]
    gain = dng_ref[...]
    parts = []
    for h in range(DN_HEADS):
        sl = slice(h * DN_HEAD_DIM, (h + 1) * DN_HEAD_DIM)
        zh = z[:, sl]
        parts.append(_rms(o[:, sl], gain) * (zh * jax.nn.sigmoid(zh)))
    y_dn = jnp.concatenate(parts, axis=-1)
    h_ref[...] = (x_ref[...] + _bdot(yc_ref[...], wa_ref[...]) + _bdot(y_dn, wb_ref[...]))


def _mixer_out(x, y_conv, o_dn, proj, dn_norm, w_out, tm):
    t, d = x.shape
    cw = y_conv.shape[1]
    tok = lambda width, col=0: pl.BlockSpec((tm, width), lambda i, col=col: (i, col))
    full = lambda shape: pl.BlockSpec(shape, lambda i: (0, 0))
    return pl.pallas_call(
        _mixer_out_body,
        grid=(t // tm,),
        in_specs=[tok(d), tok(cw), tok(cw), tok(cw, 6), full((1, DN_HEAD_DIM)),
                  full((cw, d)), full((cw, d))],
        out_specs=tok(d),
        out_shape=jax.ShapeDtypeStruct((t, d), F32),
        compiler_params=_params("parallel"),
        name="mixer_out",
    )(x, y_conv, o_dn, proj, dn_norm.reshape(1, DN_HEAD_DIM),
      w_out[:cw].astype(BF16), w_out[cw:].astype(BF16))


def _cross_attn_body(h_ref, g_ref, wq_ref, kv_ref, wo_ref, o_ref):
    h_in = h_ref[...]
    d = h_in.shape[1]
    dh = d // X_HEADS
    q = _bdot(_rms(h_in, g_ref[...]), wq_ref[...])
    outs = []
    for hd in range(X_HEADS):
        qh = q[:, hd * dh:(hd + 1) * dh].astype(BF16)
        kh = kv_ref[:, hd * dh:(hd + 1) * dh]
        vh = kv_ref[:, d + hd * dh:d + (hd + 1) * dh]
        s = lax.dot_general(qh, kh, (((1,), (1,)), ((), ())),
                            preferred_element_type=F32) * (dh ** -0.5)
        s = s - jnp.max(s, axis=-1, keepdims=True)
        e = jnp.exp(s)
        p = e / jnp.sum(e, axis=-1, keepdims=True)
        outs.append(jnp.dot(p.astype(BF16), vh, preferred_element_type=F32))
    o = jnp.concatenate(outs, axis=-1)
    o_ref[...] = h_in + _bdot(o, wo_ref[...])


def _cross_attn(h, gain, w_q, kv, w_o, batch, seq, mem_len, tm):
    t, d = h.shape
    nblk = seq // tm
    tok = pl.BlockSpec((tm, d), lambda b, i: (b * nblk + i, 0))
    full = lambda shape: pl.BlockSpec(shape, lambda b, i: (0, 0))
    return pl.pallas_call(
        _cross_attn_body,
        grid=(batch, nblk),
        in_specs=[tok, full((1, d)), full((d, d)),
                  pl.BlockSpec((mem_len, 2 * d), lambda b, i: (b, 0)), full((d, d))],
        out_specs=tok,
        out_shape=jax.ShapeDtypeStruct((t, d), F32),
        compiler_params=_params("parallel", "parallel"),
        name="cross_attn",
    )(h, gain.reshape(1, d), w_q.astype(BF16), kv.astype(BF16), w_o.astype(BF16))


def _top16_rows(s, payload=None):
    n = s.shape[0]
    row = lax.broadcasted_iota(jnp.int32, s.shape, 0).astype(F32)
    vals, pays = [], []
    for _ in range(PEER_TOPK):
        m = jnp.max(s, axis=0, keepdims=True)
        pos = jnp.min(jnp.where(s == m, row, float(n)), axis=0, keepdims=True)
        hit = row == pos
        vals.append(m)
        if payload is None:
            pays.append(pos)
        else:
            pays.append(jnp.sum(jnp.where(hit, payload, 0.0), axis=0, keepdims=True))
        s = jnp.where(hit, -jnp.inf, s)
    return jnp.concatenate(vals, axis=0), jnp.concatenate(pays, axis=0)


def _pair_candidates(v1, v2, pad):
    sub = lax.broadcasted_iota(jnp.int32, (SUBLANES, v1.shape[1]), 0)
    row = lambda v, a: jnp.broadcast_to(v[a:a + 1], sub.shape)
    lo8 = v2[:SUBLANES]
    blocks = [
        row(v1, 0) + lo8,
        row(v1, 0) + v2[SUBLANES:],
        row(v1, 1) + lo8,
        jnp.where(sub < 5, row(v1, 2) + lo8, pad),
        jnp.where(sub < 7,
                  jnp.where(sub < 4, row(v1, 3), row(v1, 4)) +
                  jnp.where(sub < 4, lo8, pltpu.roll(lo8, 4, 0)), pad),
        jnp.where(sub < 6,
                  jnp.where(sub < 2, row(v1, 5), jnp.where(sub < 4, row(v1, 6), row(v1, 7))) +
                  jnp.where((sub & 1) == 0, row(v2, 0), row(v2, 1)), pad),
        v1[SUBLANES:] + row(v2, 0),
    ]
    return jnp.concatenate(blocks, axis=0)


def _peer_route_body(h_ref, g_ref, wq_ref, keys_ref, xn_ref, idx_ref, gate_ref,
                     sc_ref, sel_ref, gt_ref):
    xn = _rms(h_ref[...], g_ref[...])
    xn_ref[...] = xn
    pq = _bdot(xn, wq_ref[...])
    ngrp = pq.shape[0] // LANES
    for j in range(2 * PEER_HEADS):
        qs = pq[:, j * PEER_HALF:(j + 1) * PEER_HALF].astype(BF16)
        sc = lax.dot_general(keys_ref[j], qs, (((1,), (1,)), ((), ())),
                             preferred_element_type=F32)
        for g in range(ngrp):
            sc_ref[j, g] = sc[:, g * LANES:(g + 1) * LANES]

    def head(h, carry):
        for g in range(ngrp):
            s1, i1 = _top16_rows(sc_ref[2 * h, g])
            s2, i2 = _top16_rows(sc_ref[2 * h + 1, g])
            cand = _pair_candidates(s1, s2, -jnp.inf)
            cand_idx = _pair_candidates(i1 * float(PEER_KEYS), i2, 0.0)
            best, sel = _top16_rows(cand, cand_idx)
            e = jnp.exp(best - jnp.max(best, axis=0, keepdims=True))
            sel_ref[g, h] = sel
            gt_ref[g, h] = e / jnp.sum(e, axis=0, keepdims=True)
        return carry

    lax.fori_loop(0, PEER_HEADS, head, 0)

    nsel = PEER_HEADS * PEER_TOPK
    for g in range(ngrp):
        rows = slice(g * LANES, (g + 1) * LANES)
        idx_ref[rows, :] = (sel_ref[g].reshape(nsel, LANES).T * float(PACK_ROWS)).astype(jnp.int32)
        gate_ref[rows, :] = gt_ref[g].reshape(nsel, LANES).T


def _peer_route(h, gain, w_pq, peer_keys, tm):
    t, d = h.shape
    nk = PEER_HEADS * 2
    ngrp = tm // LANES
    tok = lambda width: pl.BlockSpec((tm, width), lambda i: (i, 0))
    full = lambda shape: pl.BlockSpec(shape, lambda i: (0,) * len(shape))
    return pl.pallas_call(
        _peer_route_body,
        grid=(t // tm,),
        in_specs=[tok(d), full((1, d)), full((d, d)), full((nk, PEER_KEYS, PEER_HALF))],
        out_specs=[tok(d), tok(LANES), tok(LANES)],
        out_shape=[jax.ShapeDtypeStruct((t, d), F32),
                   jax.ShapeDtypeStruct((t, LANES), jnp.int32),
                   jax.ShapeDtypeStruct((t, LANES), F32)],
        scratch_shapes=[pltpu.VMEM((nk, ngrp, PEER_KEYS, LANES), F32),
                        pltpu.VMEM((ngrp, PEER_HEADS, PEER_TOPK, LANES), F32),
                        pltpu.VMEM((ngrp, PEER_HEADS, PEER_TOPK, LANES), F32)],
        compiler_params=_params("parallel"),
        name="peer_route",
    )(h, gain.reshape(1, d), w_pq.astype(BF16),
      peer_keys.reshape(nk, PEER_KEYS, PEER_HALF).astype(BF16))


PACK_ROWS = 4


def _pack_table(tbl):
    e, d = tbl.shape
    half = d // 2
    b = tbl.astype(BF16)
    lo = lax.bitcast_convert_type(b[:, :half], jnp.uint16).astype(jnp.uint32)
    hi = lax.bitcast_convert_type(b[:, half:], jnp.uint16).astype(jnp.uint32)
    return ((hi << 16) | lo).reshape(e * PACK_ROWS, LANES)


def _unpack(slab):
    lo = pltpu.bitcast(slab << 16, F32)
    hi = pltpu.bitcast(slab & jnp.uint32(0xFFFF0000), F32)
    return lo, hi


STAGE_STEPS = 4
PEER_SEL = PEER_HEADS * PEER_TOPK
IDX_REFS = 8


def _split_idx(idx):
    t, nsel = idx.shape
    ncol = nsel // IDX_REFS
    parts = idx.reshape(t, IDX_REFS, ncol).transpose(1, 0, 2).reshape(IDX_REFS, t * ncol)
    return tuple(parts[r] for r in range(IDX_REFS))


def _idx_specs(tb):
    n = tb * PEER_SEL // IDX_REFS
    return [pl.BlockSpec((n,), lambda i: (i,), memory_space=pltpu.SMEM) for _ in range(IDX_REFS)]


def _stage_rows(idx_refs, t, tbl_ref, g_ref, step=None):
    ncol = PEER_SEL // IDX_REFS
    per = ncol // STAGE_STEPS
    qs = range(ncol) if step is None else range(step * per, (step + 1) * per)
    for q in qs:
        off = t * ncol + q
        for r in range(IDX_REFS):
            m = r * ncol + q
            row = pl.multiple_of(idx_refs[r][off], PACK_ROWS)
            g_ref[m * PACK_ROWS:(m + 1) * PACK_ROWS, :] = tbl_ref[pl.ds(row, PACK_ROWS), :]


def _bf16_pieces(x, n):
    pieces = []
    for _ in range(n - 1):
        p = x.astype(BF16).astype(F32)
        pieces.append(p)
        x = x - p
    pieces.append(x.astype(BF16).astype(F32))
    return pieces


def _staged_group(idx_ref, tbl_ref, bufs, tb, base, step_fn):
    for k in range(SUBLANES):
        t = base + k
        nxt = t + 1 if k + 1 < SUBLANES else jnp.minimum(t + 1, tb - 1)
        for step in range(STAGE_STEPS):
            _stage_rows(idx_ref, nxt, tbl_ref, bufs[(k + 1) % 2], step)
            step_fn(t, k, step, bufs[k % 2])


def _peer_dot_body(*refs):
    idx_ref = refs[:IDX_REFS]
    x_ref, tbl_ref, act_ref, g0_ref, g1_ref = refs[IDX_REFS:]
    tb, nsel = act_ref.shape
    nslab = SUBLANES * SUBLANES
    lane = lax.broadcasted_iota(jnp.int32, (LANES, LANES), 1)
    col_slab = lane & (nslab - 1)
    _stage_rows(idx_ref, 0, tbl_ref, g0_ref)

    def group(i, carry):
        base = pl.multiple_of(i * SUBLANES, SUBLANES)
        pieces = _bf16_pieces(x_ref[i].reshape(nslab, LANES), LANES // nslab)
        xt = jnp.concatenate(pieces, axis=0).T.astype(BF16)
        zero = jnp.zeros_like(xt)
        acc = [jnp.zeros((nsel, LANES), F32)]

        def step_fn(t, k, j, g_ref):
            lo, hi = _unpack(g_ref[pl.ds(j, nsel, stride=PACK_ROWS), :])
            lhs = jnp.concatenate([lo.astype(BF16), hi.astype(BF16)], axis=1)
            rhs = jnp.concatenate(
                [jnp.where(col_slab == SUBLANES * r + k, xt, zero) for r in (j, PACK_ROWS + j)],
                axis=0)
            acc[0] = acc[0] + jnp.dot(lhs, rhs, preferred_element_type=F32)

        _staged_group(idx_ref, tbl_ref, (g0_ref, g1_ref), tb, base, step_fn)
        out_t = acc[0].T
        act_ref[pl.ds(base, SUBLANES), :] = jnp.sum(
            out_t.reshape(LANES // SUBLANES, SUBLANES, nsel), axis=0)
        return carry

    lax.fori_loop(0, tb // SUBLANES, group, 0)


def _peer_dot(idx_split, xr, tbl, tb):
    t = xr.shape[0] * SUBLANES
    nsel = PEER_SEL
    gbuf = pltpu.VMEM((nsel * PACK_ROWS, LANES), jnp.uint32)
    return pl.pallas_call(
        _peer_dot_body,
        grid=(t // tb,),
        in_specs=_idx_specs(tb) + [
            pl.BlockSpec((tb // SUBLANES, SUBLANES, SUBLANES, LANES), lambda i: (i, 0, 0, 0)),
            pl.BlockSpec(tbl.shape, lambda i: (0, 0), pipeline_mode=pl.Buffered(1))],
        out_specs=pl.BlockSpec((tb, nsel), lambda i: (i, 0)),
        out_shape=jax.ShapeDtypeStruct((t, nsel), F32),
        scratch_shapes=[gbuf, gbuf],
        compiler_params=_params("arbitrary"),
        name="peer_dot",
    )(*idx_split, xr, tbl)


def _peer_weights_body(act_ref, gate_ref, rep_ref, w_ref):
    a = act_ref[...]
    w = gate_ref[...] * (0.5 * a * (1.0 + lax.erf(a * (2.0 ** -0.5))))
    rep = rep_ref[...]
    w_ref[...] = sum(jnp.dot(p.astype(BF16), rep, preferred_element_type=F32)
                     for p in _bf16_pieces(w, 3))


def _peer_weights(act, gates, tm):
    t, n = act.shape
    nrow = 2 * PACK_ROWS * n
    rep = (jnp.arange(n)[:, None] == (jnp.arange(nrow) // (2 * PACK_ROWS))[None, :]).astype(BF16)
    tok = lambda width: pl.BlockSpec((tm, width), lambda i: (i, 0))
    return pl.pallas_call(
        _peer_weights_body,
        grid=(t // tm,),
        in_specs=[tok(n), tok(n), pl.BlockSpec(rep.shape, lambda i: (0, 0))],
        out_specs=tok(nrow),
        out_shape=jax.ShapeDtypeStruct((t, nrow), F32),
        compiler_params=_params("parallel"),
        name="peer_weights",
    )(act, gates, rep)


def _peer_sum_body(*refs):
    idx_ref = refs[:IDX_REFS]
    w_ref, mask_ref, tbl_ref, y_ref, g0_ref, g1_ref = refs[IDX_REFS:]
    tb = y_ref.shape[0]
    mask = mask_ref[...]
    nrow = g0_ref.shape[0] // STAGE_STEPS
    ncol = mask.shape[1] // STAGE_STEPS
    _stage_rows(idx_ref, 0, tbl_ref, g0_ref)

    def group(i, carry):
        base = pl.multiple_of(i * SUBLANES, SUBLANES)
        w_rep = w_ref[pl.ds(base, SUBLANES), :]
        acc = [None]

        def step_fn(t, k, step, g_ref):
            cols = slice(step * ncol, (step + 1) * ncol)
            wb = jnp.broadcast_to(w_rep[k:k + 1, cols], (SUBLANES, ncol)) * mask[:, cols]
            a_hi = wb.astype(BF16)
            a_lo = (wb - a_hi.astype(F32)).astype(BF16)
            lhs = jnp.concatenate([a_hi, a_lo], axis=0)
            staged = pltpu.bitcast(g_ref[step * nrow:(step + 1) * nrow, :], BF16)
            part = jnp.dot(lhs, staged, preferred_element_type=F32)
            acc[0] = part if step == 0 else acc[0] + part
            if step == STAGE_STEPS - 1:
                y_ref[t] = acc[0][:SUBLANES] + acc[0][SUBLANES:]

        _staged_group(idx_ref, tbl_ref, (g0_ref, g1_ref), tb, base, step_fn)
        return carry

    lax.fori_loop(0, tb // SUBLANES, group, 0)


def _peer_sum(idx_split, w, tbl, tb):
    t, nrow = w.shape
    nsel = PEER_SEL
    c = jnp.arange(nrow)
    out_row = (c % 2) * PACK_ROWS + (c % (2 * PACK_ROWS)) // 2
    mask = (jnp.arange(SUBLANES)[:, None] == out_row[None, :]).astype(F32)
    gbuf = pltpu.VMEM((nsel * PACK_ROWS, LANES), jnp.uint32)
    const = lambda shape: pl.BlockSpec(shape, lambda i: (0, 0))
    return pl.pallas_call(
        _peer_sum_body,
        grid=(t // tb,),
        in_specs=_idx_specs(tb) + [
            pl.BlockSpec((tb, nrow), lambda i: (i, 0)),
            const(mask.shape),
            pl.BlockSpec(tbl.shape, lambda i: (0, 0), pipeline_mode=pl.Buffered(1))],
        out_specs=pl.BlockSpec((tb, SUBLANES, LANES), lambda i: (i, 0, 0)),
        out_shape=jax.ShapeDtypeStruct((t, SUBLANES, LANES), F32),
        scratch_shapes=[gbuf, gbuf],
        compiler_params=_params("arbitrary"),
        name="peer_sum",
    )(*idx_split, w, mask, tbl)


def _final_body(h_ref, y_ref, g_ref, o_ref):
    o_ref[...] = _rms(h_ref[...] + y_ref[...], g_ref[...])


def _final(h, y, gain, tm):
    t, d = h.shape
    tok = pl.BlockSpec((tm, d), lambda i: (i, 0))
    return pl.pallas_call(
        _final_body,
        grid=(t // tm,),
        in_specs=[tok, tok, pl.BlockSpec((1, d), lambda i: (0, 0))],
        out_specs=tok,
        out_shape=jax.ShapeDtypeStruct((t, d), F32),
        compiler_params=_params("parallel"),
        name="final_norm",
    )(h, y, gain.reshape(1, d))


def _pick_tile(n, pref):
    tile = min(n, pref)
    while n % tile:
        tile //= 2
    return tile


def _layer(h, mem, batch, seq, norm_mix, w_in, conv_w, dn_conv_w, dn_a_log, dn_dt_bias, dn_norm,
           w_out, norm_x, norm_mem, w_xq, w_xkv, w_xo, norm_ffn, w_pq, peer_keys,
           expert_u, expert_v):
    t, d = h.shape
    mem_len = mem.shape[0] // batch
    in_cols = w_in.shape[1]
    pad = (-in_cols) % LANES
    w_in_p = jnp.pad(w_in, ((0, 0), (0, pad))).astype(BF16)

    proj = _norm_matmul(h, norm_mix, w_in_p, _pick_tile(t, 256), "in_proj")
    y_conv, q, k, v, gb = _mixer_pre(proj, conv_w, dn_conv_w, dn_a_log, dn_dt_bias,
                                     batch, seq, _pick_tile(seq, 256))
    o_dn = _delta_rule(q, k, v, gb, batch, seq)
    h = _mixer_out(h, y_conv, o_dn, proj, dn_norm, w_out, _pick_tile(t, 512))

    kv = _norm_matmul(mem, norm_mem, w_xkv.astype(BF16), _pick_tile(mem.shape[0], 256), "kv_proj")
    h = _cross_attn(h, norm_x, w_xq, kv, w_xo, batch, seq, mem_len, _pick_tile(seq, 512))

    xn, idx, gates = _peer_route(h, norm_ffn, w_pq, peer_keys, _pick_tile(t, 512))
    tb = _pick_tile(t, 64)
    xr = xn.reshape(t // SUBLANES, SUBLANES, SUBLANES, LANES).transpose(0, 2, 1, 3)
    idx_split = _split_idx(idx)
    act = _peer_dot(idx_split, xr, _pack_table(expert_u), tb)
    w = _peer_weights(act, gates, _pick_tile(t, 1024))
    y = _peer_sum(idx_split, w, _pack_table(expert_v), tb)
    return h, y.reshape(t, d)


def kernel(x, mem, norm_mix, w_in, conv_w, dn_conv_w, dn_a_log, dn_dt_bias, dn_norm, w_out,
           norm_x, norm_mem, w_xq, w_xkv, w_xo, norm_ffn, w_pq, peer_keys, expert_u, expert_v,
           norm_final):
    batch, seq, d = x.shape
    depth = norm_mix.shape[0]
    h = x.reshape(batch * seq, d)
    mem2 = mem.reshape(batch * mem.shape[1], d)
    y = jnp.zeros_like(h)
    for layer in range(depth):
        if layer:
            h = h + y
        h, y = _layer(h, mem2, batch, seq, norm_mix[layer], w_in[layer], conv_w[layer],
                      dn_conv_w[layer], dn_a_log[layer], dn_dt_bias[layer], dn_norm[layer],
                      w_out[layer], norm_x[layer], norm_mem[layer], w_xq[layer], w_xkv[layer],
                      w_xo[layer], norm_ffn[layer], w_pq[layer], peer_keys[layer],
                      expert_u[layer], expert_v[layer])
    out = _final(h, y, norm_final, _pick_tile(batch * seq, 512))
    return out.reshape(batch, seq, d)
```

```python
import functools

import jax
import jax.numpy as jnp
from jax import lax
from jax.experimental import pallas as pl
from jax.experimental.pallas import tpu as pltpu

EPS = 1e-6
F32 = jnp.float32
BF16 = jnp.bfloat16
HIGHEST = lax.Precision.HIGHEST

LANES = 128
SUBLANES = 8
VMEM_LIMIT = 56 * 1024 * 1024

CONV_GROUP_K = 3
DN_CONV_K = 4
DN_HEADS = 4
DN_HEAD_DIM = 128
DN_CHUNK = 64
X_HEADS = 4
PEER_HEADS = 8
PEER_KEYS = 128
PEER_TOPK = 16
PEER_HALF = 64


def _params(*sem):
    return pltpu.CompilerParams(dimension_semantics=sem, vmem_limit_bytes=VMEM_LIMIT)


def _rms(x, gain):
    ms = jnp.mean(x * x, axis=-1, keepdims=True)
    return x * lax.rsqrt(ms + EPS) * gain


def _bdot(a, b):
    return jnp.dot(a.astype(BF16), b.astype(BF16), preferred_element_type=F32)


def _hdot(a, b):
    return jnp.dot(a, b, preferred_element_type=F32, precision=HIGHEST)


def _hdot_nt(a, b):
    return lax.dot_general(a, b, (((1,), (1,)), ((), ())),
                           preferred_element_type=F32, precision=HIGHEST)


def _hdot_tn(a, b):
    return lax.dot_general(a, b, (((0,), (0,)), ((), ())),
                           preferred_element_type=F32, precision=HIGHEST)


def _norm_matmul_body(x_ref, g_ref, w_ref, o_ref):
    xn = _rms(x_ref[...], g_ref[...])
    o_ref[...] = jnp.dot(xn.astype(BF16), w_ref[...], preferred_element_type=F32)


def _norm_matmul(x, gain, w, tm, name):
    t, d = x.shape
    n = w.shape[1]
    return pl.pallas_call(
        _norm_matmul_body,
        grid=(t // tm,),
        in_specs=[pl.BlockSpec((tm, d), lambda i: (i, 0)),
                  pl.BlockSpec((1, d), lambda i: (0, 0)),
                  pl.BlockSpec((d, n), lambda i: (0, 0))],
        out_specs=pl.BlockSpec((tm, n), lambda i: (i, 0)),
        out_shape=jax.ShapeDtypeStruct((t, n), F32),
        compiler_params=_params("parallel"),
        name=name,
    )(x, gain.reshape(1, d), w)


def _shift_rows(x, halo, k):
    rolled = pltpu.roll(x, k, 0)
    hal = pltpu.roll(halo, k, 0)
    row = lax.broadcasted_iota(jnp.int32, (SUBLANES, x.shape[1]), 0)
    head = jnp.where(row < k, hal, rolled[:SUBLANES])
    return jnp.concatenate([head, rolled[SUBLANES:]], axis=0)


def _causal_conv(x, halo, w_ref):
    kk = w_ref.shape[0]
    y = x * w_ref[kk - 1:kk, :]
    for j in range(1, kk):
        y = y + _shift_rows(x, halo, j) * w_ref[kk - 1 - j:kk - j, :]
    return y


def _l2norm_heads(x):
    outs = []
    for h in range(DN_HEADS):
        xh = x[:, h * DN_HEAD_DIM:(h + 1) * DN_HEAD_DIM]
        outs.append(xh * lax.rsqrt(jnp.sum(xh * xh, axis=-1, keepdims=True) + EPS))
    return jnp.concatenate(outs, axis=-1)


def _mixer_pre_body(b_ref, c_ref, h_ref, q_ref, k_ref, v_ref, t_ref,
                    ch_ref, hh_ref, qh_ref, kh_ref, vh_ref,
                    cw_ref, qw_ref, kw_ref, vw_ref, alog_ref, dtb_ref, tril_ref,
                    yc_ref, qo_ref, ko_ref, vo_ref, gb_ref):
    first = pl.program_id(1) == 0
    keep = jnp.where(first, 0.0, 1.0).astype(F32)

    u = c_ref[...] * h_ref[...]
    uh = ch_ref[...] * hh_ref[...] * keep
    yc_ref[...] = b_ref[...] * _causal_conv(u, uh, cw_ref)

    def dn_branch(x_ref, xh_ref, w_ref):
        y = _causal_conv(x_ref[...], xh_ref[...] * keep, w_ref)
        return y * jax.nn.sigmoid(y)

    qo_ref[...] = _l2norm_heads(dn_branch(q_ref, qh_ref, qw_ref))
    ko_ref[...] = _l2norm_heads(dn_branch(k_ref, kh_ref, kw_ref))
    vo_ref[...] = dn_branch(v_ref, vh_ref, vw_ref)

    tail = t_ref[...]
    beta = jax.nn.sigmoid(tail)
    g = -jnp.exp(alog_ref[...]) * jax.nn.softplus(tail + dtb_ref[...])
    gcum = _hdot(tril_ref[...], g)
    lane = lax.broadcasted_iota(jnp.int32, tail.shape, 1)
    gb_ref[...] = jnp.where(lane < DN_HEADS, beta, gcum)


def _mixer_pre(proj, conv_w, dn_conv_w, dn_a_log, dn_dt_bias, batch, seq, ts):
    t = proj.shape[0]
    cw = 512
    nblk = seq // ts
    hb = ts // SUBLANES

    def cur(col):
        return pl.BlockSpec((ts, cw if col < 7 else LANES),
                            lambda b, i, col=col: (b * nblk + i, col))

    def halo(col):
        return pl.BlockSpec(
            (SUBLANES, cw),
            lambda b, i, col=col: (jnp.maximum((b * nblk + i) * hb - 1, 0), col))

    def full(shape):
        return pl.BlockSpec(shape, lambda b, i: (0,) * len(shape))

    qw, kw, vw = (dn_conv_w[:, j * cw:(j + 1) * cw] for j in range(3))
    lane_pad = jnp.zeros((LANES - 2 * DN_HEADS,), F32)
    alog = jnp.concatenate([jnp.zeros((DN_HEADS,), F32), dn_a_log, lane_pad]).reshape(1, LANES)
    dtb = jnp.concatenate([jnp.zeros((DN_HEADS,), F32), dn_dt_bias, lane_pad]).reshape(1, LANES)
    r = jnp.arange(ts)
    tril = ((r[:, None] >= r[None, :]) &
            (r[:, None] // DN_CHUNK == r[None, :] // DN_CHUNK)).astype(F32)

    tail_spec = pl.BlockSpec((ts, LANES), lambda b, i: (b * nblk + i, 7 * cw // LANES))
    out_tok = lambda w: pl.BlockSpec((ts, w), lambda b, i: (b * nblk + i, 0))
    return pl.pallas_call(
        _mixer_pre_body,
        grid=(batch, nblk),
        in_specs=[cur(0), cur(1), cur(2), cur(3), cur(4), cur(5), tail_spec,
                  halo(1), halo(2), halo(3), halo(4), halo(5),
                  full((CONV_GROUP_K, cw)), full((DN_CONV_K, cw)), full((DN_CONV_K, cw)),
                  full((DN_CONV_K, cw)), full((1, LANES)), full((1, LANES)), full((ts, ts))],
        out_specs=[out_tok(cw), out_tok(cw), out_tok(cw), out_tok(cw), out_tok(LANES)],
        out_shape=[jax.ShapeDtypeStruct((t, cw), F32)] * 4 + [jax.ShapeDtypeStruct((t, LANES), F32)],
        compiler_params=_params("parallel", "parallel"),
        name="mixer_pre",
    )(proj, proj, proj, proj, proj, proj, proj, proj, proj, proj, proj, proj,
      conv_w, qw, kw, vw, alog, dtb, tril)


_NN = (((1,), (0,)), ((), ()))
_NT = (((1,), (1,)), ((), ()))
_TN = (((0,), (0,)), ((), ()))
DN_LOCAL_PASSES = 1
DN_SCAN_PASSES = 1


def _mm(a, b, dims, passes):
    dot = lambda x, y: lax.dot_general(x, y, dims, preferred_element_type=F32)
    if passes == 6:
        return lax.dot_general(a, b, dims, preferred_element_type=F32, precision=HIGHEST)
    a_hi = a.astype(BF16)
    b_hi = b.astype(BF16)
    if passes == 1:
        return dot(a_hi, b_hi)
    a_lo = (a - a_hi.astype(F32)).astype(BF16)
    b_lo = (b - b_hi.astype(F32)).astype(BF16)
    return dot(a_hi, b_hi) + (dot(a_hi, b_lo) + dot(a_lo, b_hi))


DN_LOCAL_CHUNKS = 4


def _dn_local_body(q_ref, k_ref, v_ref, gb_ref, u_ref, w_ref, qd_ref, kd_ref, in_ref, gl_ref):
    c = DN_CHUNK
    n = DN_HEADS * c
    p = DN_LOCAL_PASSES
    cis = range(q_ref.shape[0] // c)
    each = lambda f, *xs: [f(*a) for a in zip(*xs)]
    lanes = lambda x: jnp.broadcast_to(x, (x.shape[0], LANES))

    def stack(ref, ci):
        return jnp.concatenate([ref[ci * c:(ci + 1) * c, h * DN_HEAD_DIM:(h + 1) * DN_HEAD_DIM]
                                for h in range(DN_HEADS)], axis=0)

    def col(gb, j):
        return jnp.concatenate([lanes(gb[:, j + h:j + h + 1]) for h in range(DN_HEADS)], axis=0)

    gb = [gb_ref[ci * c:(ci + 1) * c, :] for ci in cis]
    q = [stack(q_ref, ci) * (DN_HEAD_DIM ** -0.5) for ci in cis]
    k = [stack(k_ref, ci) for ci in cis]
    v = [stack(v_ref, ci) for ci in cis]
    beta = [col(x, 0) for x in gb]
    g = [col(x, DN_HEADS) for x in gb]
    g_last = [jnp.concatenate(
        [jnp.broadcast_to(x[c - 1:c, DN_HEADS + h:DN_HEADS + h + 1], (c, LANES))
         for h in range(DN_HEADS)], axis=0) for x in gb]

    row = lax.broadcasted_iota(jnp.int32, (n, n), 0)
    cl = lax.broadcasted_iota(jnp.int32, (n, n), 1)
    shift = c.bit_length() - 1
    same_head = (row >> shift) == (cl >> shift)
    causal = same_head & (row >= cl)
    strict = same_head & (row > cl)
    eye = (row == cl).astype(F32)
    decay = [jnp.where(causal, jnp.exp(jnp.where(causal, x[:, 0:1] - x.T[0:1, :], 0.0)), 0.0)
             for x in g]
    k_beta = each(lambda a, b: a * b, k, beta)
    v_beta = each(lambda a, b: a * b, v, beta)
    a = each(lambda kb, kk, d: jnp.where(strict, _mm(kb, kk, _NT, p) * d, 0.0), k_beta, k, decay)

    t_mat = [eye - x for x in a]
    pw = [_mm(x, x, _NN, p) for x in a]
    for _ in range(4):
        t_mat = each(lambda t, w_: t + _mm(t, w_, _NN, p), t_mat, pw)
        pw = [_mm(x, x, _NN, p) for x in pw]
    t_mat = each(lambda t, w_: t + _mm(t, w_, _NN, p), t_mat, pw)

    e_g = [jnp.exp(x) for x in g]
    uw = each(lambda t, vb, kb, e: _mm(t, jnp.concatenate([vb, kb * e], axis=1), _NN, p),
              t_mat, v_beta, k_beta, e_g)
    intra = each(lambda qq, kk, d: _mm(qq, kk, _NT, p) * d, q, k, decay)
    for ci in cis:
        u_ref[ci] = uw[ci][:, :DN_HEAD_DIM]
        w_ref[ci] = uw[ci][:, DN_HEAD_DIM:]
        in_ref[ci] = intra[ci]
        qd_ref[ci] = q[ci] * e_g[ci]
        kd_ref[ci] = k[ci] * jnp.exp(g_last[ci] - g[ci])
        gl_ref[ci] = jnp.concatenate(
            [jnp.exp(g_last[ci][h * c:h * c + 1]) for h in range(DN_HEADS)] +
            [jnp.zeros((SUBLANES - DN_HEADS, LANES), F32)], axis=0)


def _dn_scan_body(u_ref, w_ref, qd_ref, kd_ref, in_ref, gl_ref, o_ref, state_ref):
    c = DN_CHUNK
    p = DN_SCAN_PASSES

    @pl.when(pl.program_id(0) == 0)
    def _():
        state_ref[...] = jnp.zeros_like(state_ref)

    for b in range(u_ref.shape[0]):
        u = u_ref[b, 0]
        w = w_ref[b, 0]
        qd = qd_ref[b, 0]
        kd = kd_ref[b, 0]
        gl = gl_ref[b, 0]
        v_new, q_state = [], []
        for h in range(DN_HEADS):
            rows = slice(h * c, (h + 1) * c)
            ws = _mm(jnp.concatenate([w[rows], qd[rows]], axis=0), state_ref[b, h], _NN, p)
            v_new.append(u[rows] - ws[:c])
            q_state.append(ws[c:])
        o = jnp.concatenate(q_state, axis=0) + _mm(in_ref[b, 0], jnp.concatenate(v_new, axis=0),
                                                   _NN, p)
        for h in range(DN_HEADS):
            rows = slice(h * c, (h + 1) * c)
            state_ref[b, h] = (state_ref[b, h] * gl[h:h + 1] +
                               _mm(kd[rows], v_new[h], _TN, p))
        o_ref[b] = jnp.concatenate([o[h * c:(h + 1) * c] for h in range(DN_HEADS)], axis=1)


def _delta_rule(q, k, v, gb, batch, seq):
    t, w = q.shape
    nc = seq // DN_CHUNK
    n = DN_HEADS * DN_CHUNK
    cps = _pick_tile(batch * nc, DN_LOCAL_CHUNKS)
    tok = lambda width: pl.BlockSpec((cps * DN_CHUNK, width), lambda i: (i, 0))
    per_chunk = lambda rows, width: pl.BlockSpec((cps, rows, width), lambda i: (i, 0, 0))
    f = lambda rows, width: jax.ShapeDtypeStruct((batch * nc, rows, width), F32)
    u, wm, qd, kd, intra, gl = pl.pallas_call(
        _dn_local_body,
        grid=(batch * nc // cps,),
        in_specs=[tok(w), tok(w), tok(w), tok(LANES)],
        out_specs=[per_chunk(n, DN_HEAD_DIM)] * 4 + [per_chunk(n, n), per_chunk(SUBLANES, LANES)],
        out_shape=[f(n, DN_HEAD_DIM)] * 4 + [f(n, n), f(SUBLANES, LANES)],
        compiler_params=_params("parallel"),
        name="dn_local",
    )(q, k, v, gb)

    seq_blk = lambda rows, width: pl.BlockSpec((batch, 1, rows, width), lambda i: (0, i, 0, 0))
    by_batch = lambda x: x.reshape(batch, nc, *x.shape[1:])
    o = pl.pallas_call(
        _dn_scan_body,
        grid=(nc,),
        in_specs=[seq_blk(n, DN_HEAD_DIM)] * 4 + [seq_blk(n, n), seq_blk(SUBLANES, LANES)],
        out_specs=pl.BlockSpec((batch, DN_CHUNK, w), lambda i: (0, i, 0)),
        out_shape=jax.ShapeDtypeStruct((batch, seq, w), F32),
        scratch_shapes=[pltpu.VMEM((batch, DN_HEADS, DN_HEAD_DIM, DN_HEAD_DIM), F32)],
        compiler_params=_params("arbitrary"),
        name="dn_scan",
    )(by_batch(u), by_batch(wm), by_batch(qd), by_batch(kd), by_batch(intra), by_batch(gl))
    return o.reshape(t, w)


def _mixer_out_body(x_ref, yc_ref, o_ref, z_ref, dng_ref, wa_ref, wb_ref, h_ref):
    o = o_ref[...]
    z = z_ref[...]
    gain = dng_ref[...]
    parts = []
    for h in range(DN_HEADS):
        sl = slice(h * DN_HEAD_DIM, (h + 1) * DN_HEAD_DIM)
        zh = z[:, sl]
        parts.append(_rms(o[:, sl], gain) * (zh * jax.nn.sigmoid(zh)))
    y_dn = jnp.concatenate(parts, axis=-1)
    h_ref[...] = (x_ref[...] + _bdot(yc_ref[...], wa_ref[...]) + _bdot(y_dn, wb_ref[...]))


def _mixer_out(x, y_conv, o_dn, proj, dn_norm, w_out, tm):
    t, d = x.shape
    cw = y_conv.shape[1]
    tok = lambda width, col=0: pl.BlockSpec((tm, width), lambda i, col=col: (i, col))
    full = lambda shape: pl.BlockSpec(shape, lambda i: (0, 0))
    return pl.pallas_call(
        _mixer_out_body,
        grid=(t // tm,),
        in_specs=[tok(d), tok(cw), tok(cw), tok(cw, 6), full((1, DN_HEAD_DIM)),
                  full((cw, d)), full((cw, d))],
        out_specs=tok(d),
        out_shape=jax.ShapeDtypeStruct((t, d), F32),
        compiler_params=_params("parallel"),
        name="mixer_out",
    )(x, y_conv, o_dn, proj, dn_norm.reshape(1, DN_HEAD_DIM),
      w_out[:cw].astype(BF16), w_out[cw:].astype(BF16))


def _cross_attn_body(h_ref, g_ref, wq_ref, kv_ref, wo_ref, o_ref):
    h_in = h_ref[...]
    d = h_in.shape[1]
    dh = d // X_HEADS
    q = _bdot(_rms(h_in, g_ref[...]), wq_ref[...])
    outs = []
    for hd in range(X_HEADS):
        qh = q[:, hd * dh:(hd + 1) * dh].astype(BF16)
        kh = kv_ref[:, hd * dh:(hd + 1) * dh]
        vh = kv_ref[:, d + hd * dh:d + (hd + 1) * dh]
        s = lax.dot_general(qh, kh, (((1,), (1,)), ((), ())),
                            preferred_element_type=F32) * (dh ** -0.5)
        s = s - jnp.max(s, axis=-1, keepdims=True)
        e = jnp.exp(s)
        p = e / jnp.sum(e, axis=-1, keepdims=True)
        outs.append(jnp.dot(p.astype(BF16), vh, preferred_element_type=F32))
    o = jnp.concatenate(outs, axis=-1)
    o_ref[...] = h_in + _bdot(o, wo_ref[...])


def _cross_attn(h, gain, w_q, kv, w_o, batch, seq, mem_len, tm):
    t, d = h.shape
    nblk = seq // tm
    tok = pl.BlockSpec((tm, d), lambda b, i: (b * nblk + i, 0))
    full = lambda shape: pl.BlockSpec(shape, lambda b, i: (0, 0))
    return pl.pallas_call(
        _cross_attn_body,
        grid=(batch, nblk),
        in_specs=[tok, full((1, d)), full((d, d)),
                  pl.BlockSpec((mem_len, 2 * d), lambda b, i: (b, 0)), full((d, d))],
        out_specs=tok,
        out_shape=jax.ShapeDtypeStruct((t, d), F32),
        compiler_params=_params("parallel", "parallel"),
        name="cross_attn",
    )(h, gain.reshape(1, d), w_q.astype(BF16), kv.astype(BF16), w_o.astype(BF16))


def _top16_rows(s, payload=None):
    n = s.shape[0]
    row = lax.broadcasted_iota(jnp.int32, s.shape, 0).astype(F32)
    vals, pays = [], []
    for _ in range(PEER_TOPK):
        m = jnp.max(s, axis=0, keepdims=True)
        pos = jnp.min(jnp.where(s == m, row, float(n)), axis=0, keepdims=True)
        hit = row == pos
        vals.append(m)
        if payload is None:
            pays.append(pos)
        else:
            pays.append(jnp.sum(jnp.where(hit, payload, 0.0), axis=0, keepdims=True))
        s = jnp.where(hit, -jnp.inf, s)
    return jnp.concatenate(vals, axis=0), jnp.concatenate(pays, axis=0)


def _pair_candidates(v1, v2, pad):
    sub = lax.broadcasted_iota(jnp.int32, (SUBLANES, v1.shape[1]), 0)
    row = lambda v, a: jnp.broadcast_to(v[a:a + 1], sub.shape)
    lo8 = v2[:SUBLANES]
    blocks = [
        row(v1, 0) + lo8,
        row(v1, 0) + v2[SUBLANES:],
        row(v1, 1) + lo8,
        jnp.where(sub < 5, row(v1, 2) + lo8, pad),
        jnp.where(sub < 7,
                  jnp.where(sub < 4, row(v1, 3), row(v1, 4)) +
                  jnp.where(sub < 4, lo8, pltpu.roll(lo8, 4, 0)), pad),
        jnp.where(sub < 6,
                  jnp.where(sub < 2, row(v1, 5), jnp.where(sub < 4, row(v1, 6), row(v1, 7))) +
                  jnp.where((sub & 1) == 0, row(v2, 0), row(v2, 1)), pad),
        v1[SUBLANES:] + row(v2, 0),
    ]
    return jnp.concatenate(blocks, axis=0)


def _peer_route_body(h_ref, g_ref, wq_ref, keys_ref, xn_ref, idx_ref, gate_ref,
                     sc_ref, sel_ref, gt_ref):
    xn = _rms(h_ref[...], g_ref[...])
    for r in range(SUBLANES):
        xn_ref[:, r] = xn[:, r * LANES:(r + 1) * LANES].reshape(-1, SUBLANES, LANES)
    pq = _bdot(xn, wq_ref[...])
    ngrp = pq.shape[0] // LANES
    for j in range(2 * PEER_HEADS):
        qs = pq[:, j * PEER_HALF:(j + 1) * PEER_HALF].astype(BF16)
        sc = lax.dot_general(keys_ref[j], qs, (((1,), (1,)), ((), ())),
                             preferred_element_type=F32)
        for g in range(ngrp):
            sc_ref[j, g] = sc[:, g * LANES:(g + 1) * LANES]

    def head(h, carry):
        for g in range(ngrp):
            s1, i1 = _top16_rows(sc_ref[2 * h, g])
            s2, i2 = _top16_rows(sc_ref[2 * h + 1, g])
            cand = _pair_candidates(s1, s2, -jnp.inf)
            cand_idx = _pair_candidates(i1 * float(PEER_KEYS), i2, 0.0)
            best, sel = _top16_rows(cand, cand_idx)
            e = jnp.exp(best - jnp.max(best, axis=0, keepdims=True))
            sel_ref[g, h] = sel
            gt_ref[g, h] = e / jnp.sum(e, axis=0, keepdims=True)
        return carry

    lax.fori_loop(0, PEER_HEADS, head, 0)

    nsel = PEER_HEADS * PEER_TOPK
    for g in range(ngrp):
        rows = slice(g * LANES, (g + 1) * LANES)
        idx_ref[rows, :] = (sel_ref[g].reshape(nsel, LANES).T * float(PACK_ROWS)).astype(jnp.int32)
        gate_ref[rows, :] = gt_ref[g].reshape(nsel, LANES).T


def _peer_route(h, gain, w_pq, peer_keys, tm):
    t, d = h.shape
    nk = PEER_HEADS * 2
    ngrp = tm // LANES
    tok = lambda width: pl.BlockSpec((tm, width), lambda i: (i, 0))
    full = lambda shape: pl.BlockSpec(shape, lambda i: (0,) * len(shape))
    return pl.pallas_call(
        _peer_route_body,
        grid=(t // tm,),
        in_specs=[tok(d), full((1, d)), full((d, d)), full((nk, PEER_KEYS, PEER_HALF))],
        out_specs=[pl.BlockSpec((tm // SUBLANES, SUBLANES, SUBLANES, LANES), lambda i: (i, 0, 0, 0)),
                   tok(LANES), tok(LANES)],
        out_shape=[jax.ShapeDtypeStruct((t // SUBLANES, SUBLANES, SUBLANES, LANES), F32),
                   jax.ShapeDtypeStruct((t, LANES), jnp.int32),
                   jax.ShapeDtypeStruct((t, LANES), F32)],
        scratch_shapes=[pltpu.VMEM((nk, ngrp, PEER_KEYS, LANES), F32),
                        pltpu.VMEM((ngrp, PEER_HEADS, PEER_TOPK, LANES), F32),
                        pltpu.VMEM((ngrp, PEER_HEADS, PEER_TOPK, LANES), F32)],
        compiler_params=_params("parallel"),
        name="peer_route",
    )(h, gain.reshape(1, d), w_pq.astype(BF16),
      peer_keys.reshape(nk, PEER_KEYS, PEER_HALF).astype(BF16))


PACK_ROWS = 4


def _pack_table(tbl):
    e, d = tbl.shape
    half = d // 2
    b = tbl.astype(BF16)
    lo = lax.bitcast_convert_type(b[:, :half], jnp.uint16).astype(jnp.uint32)
    hi = lax.bitcast_convert_type(b[:, half:], jnp.uint16).astype(jnp.uint32)
    return ((hi << 16) | lo).reshape(e * PACK_ROWS, LANES)


def _unpack(slab):
    lo = pltpu.bitcast(slab << 16, F32)
    hi = pltpu.bitcast(slab & jnp.uint32(0xFFFF0000), F32)
    return lo, hi


STAGE_STEPS = 4
PEER_SEL = PEER_HEADS * PEER_TOPK
IDX_REFS = 8


def _split_idx(idx):
    t, nsel = idx.shape
    ncol = nsel // IDX_REFS
    parts = idx.reshape(t, IDX_REFS, ncol).transpose(1, 0, 2).reshape(IDX_REFS, t * ncol)
    return tuple(parts[r] for r in range(IDX_REFS))


def _idx_specs(tb):
    n = tb * PEER_SEL // IDX_REFS
    return [pl.BlockSpec((n,), lambda i: (i,), memory_space=pltpu.SMEM) for _ in range(IDX_REFS)]


def _stage_rows(idx_refs, t, tbl_ref, g_ref, step=None):
    ncol = PEER_SEL // IDX_REFS
    per = ncol // STAGE_STEPS
    qs = range(ncol) if step is None else range(step * per, (step + 1) * per)
    for q in qs:
        off = t * ncol + q
        for r in range(IDX_REFS):
            m = r * ncol + q
            row = pl.multiple_of(idx_refs[r][off], PACK_ROWS)
            g_ref[m * PACK_ROWS:(m + 1) * PACK_ROWS, :] = tbl_ref[pl.ds(row, PACK_ROWS), :]


def _bf16_pieces(x, n):
    pieces = []
    for _ in range(n - 1):
        p = x.astype(BF16).astype(F32)
        pieces.append(p)
        x = x - p
    pieces.append(x.astype(BF16).astype(F32))
    return pieces


def _staged_group(idx_ref, tbl_ref, bufs, tb, base, step_fn):
    for k in range(SUBLANES):
        t = base + k
        nxt = t + 1 if k + 1 < SUBLANES else jnp.minimum(t + 1, tb - 1)
        for step in range(STAGE_STEPS):
            _stage_rows(idx_ref, nxt, tbl_ref, bufs[(k + 1) % 2], step)
            step_fn(t, k, step, bufs[k % 2])


def _peer_dot_body(*refs):
    idx_ref = refs[:IDX_REFS]
    x_ref, tbl_ref, act_ref, g0_ref, g1_ref = refs[IDX_REFS:]
    tb, nsel = act_ref.shape
    nslab = SUBLANES * SUBLANES
    lane = lax.broadcasted_iota(jnp.int32, (LANES, LANES), 1)
    col_slab = lane & (nslab - 1)
    _stage_rows(idx_ref, 0, tbl_ref, g0_ref)

    def group(i, carry):
        base = pl.multiple_of(i * SUBLANES, SUBLANES)
        pieces = _bf16_pieces(x_ref[i].reshape(nslab, LANES), LANES // nslab)
        xt = jnp.concatenate(pieces, axis=0).T.astype(BF16)
        zero = jnp.zeros_like(xt)
        acc = [jnp.zeros((nsel, LANES), F32)]

        def step_fn(t, k, j, g_ref):
            lo, hi = _unpack(g_ref[pl.ds(j, nsel, stride=PACK_ROWS), :])
            lhs = jnp.concatenate([lo.astype(BF16), hi.astype(BF16)], axis=1)
            rhs = jnp.concatenate(
                [jnp.where(col_slab == SUBLANES * r + k, xt, zero) for r in (j, PACK_ROWS + j)],
                axis=0)
            acc[0] = acc[0] + jnp.dot(lhs, rhs, preferred_element_type=F32)

        _staged_group(idx_ref, tbl_ref, (g0_ref, g1_ref), tb, base, step_fn)
        out_t = acc[0].T
        act_ref[pl.ds(base, SUBLANES), :] = jnp.sum(
            out_t.reshape(LANES // SUBLANES, SUBLANES, nsel), axis=0)
        return carry

    lax.fori_loop(0, tb // SUBLANES, group, 0)


def _peer_dot(idx_split, xr, tbl, tb):
    t = xr.shape[0] * SUBLANES
    nsel = PEER_SEL
    gbuf = pltpu.VMEM((nsel * PACK_ROWS, LANES), jnp.uint32)
    return pl.pallas_call(
        _peer_dot_body,
        grid=(t // tb,),
        in_specs=_idx_specs(tb) + [
            pl.BlockSpec((tb // SUBLANES, SUBLANES, SUBLANES, LANES), lambda i: (i, 0, 0, 0)),
            pl.BlockSpec(tbl.shape, lambda i: (0, 0), pipeline_mode=pl.Buffered(1))],
        out_specs=pl.BlockSpec((tb, nsel), lambda i: (i, 0)),
        out_shape=jax.ShapeDtypeStruct((t, nsel), F32),
        scratch_shapes=[gbuf, gbuf],
        compiler_params=_params("arbitrary"),
        name="peer_dot",
    )(*idx_split, xr, tbl)


def _peer_weights_body(act_ref, gate_ref, rep_ref, w_ref):
    a = act_ref[...]
    w = gate_ref[...] * (0.5 * a * (1.0 + lax.erf(a * (2.0 ** -0.5))))
    rep = rep_ref[...]
    w_ref[...] = sum(jnp.dot(p.astype(BF16), rep, preferred_element_type=F32)
                     for p in _bf16_pieces(w, 3))


def _peer_weights(act, gates, tm):
    t, n = act.shape
    nrow = 2 * PACK_ROWS * n
    rep = (jnp.arange(n)[:, None] == (jnp.arange(nrow) // (2 * PACK_ROWS))[None, :]).astype(BF16)
    tok = lambda width: pl.BlockSpec((tm, width), lambda i: (i, 0))
    return pl.pallas_call(
        _peer_weights_body,
        grid=(t // tm,),
        in_specs=[tok(n), tok(n), pl.BlockSpec(rep.shape, lambda i: (0, 0))],
        out_specs=tok(nrow),
        out_shape=jax.ShapeDtypeStruct((t, nrow), F32),
        compiler_params=_params("parallel"),
        name="peer_weights",
    )(act, gates, rep)


def _peer_sum_body(has_gain, *refs):
    idx_ref = refs[:IDX_REFS]
    w_ref, h_ref, mask_ref = refs[IDX_REFS:IDX_REFS + 3]
    gain_ref = refs[IDX_REFS + 3] if has_gain else None
    tbl_ref, y_ref, g0_ref, g1_ref = refs[IDX_REFS + 3 + has_gain:]
    tb = y_ref.shape[0]
    mask = mask_ref[...]
    nrow = g0_ref.shape[0] // STAGE_STEPS
    ncol = mask.shape[1] // STAGE_STEPS
    _stage_rows(idx_ref, 0, tbl_ref, g0_ref)

    def finish(i, rows):
        base = pl.multiple_of(i * SUBLANES, SUBLANES)
        for k, s in enumerate(rows):
            if gain_ref is not None:
                ms = jnp.sum(jnp.sum(s * s, axis=1, keepdims=True), axis=0, keepdims=True)
                s = s * lax.rsqrt(ms * (1.0 / s.size) + EPS) * gain_ref[...]
            y_ref[base + k] = s

    def group(i, prev_rows):
        finish(jnp.maximum(i - 1, 0), prev_rows)
        base = pl.multiple_of(i * SUBLANES, SUBLANES)
        w_rep = w_ref[pl.ds(base, SUBLANES), :]
        acc = [None]

        def step_fn(t, k, step, g_ref):
            cols = slice(step * ncol, (step + 1) * ncol)
            wb = jnp.broadcast_to(w_rep[k:k + 1, cols], (SUBLANES, ncol)) * mask[:, cols]
            a_hi = wb.astype(BF16)
            a_lo = (wb - a_hi.astype(F32)).astype(BF16)
            lhs = jnp.concatenate([a_hi, a_lo], axis=0)
            staged = pltpu.bitcast(g_ref[step * nrow:(step + 1) * nrow, :], BF16)
            part = jnp.dot(lhs, staged, preferred_element_type=F32)
            acc[0] = part if step == 0 else acc[0] + part
            if step == STAGE_STEPS - 1:
                rows.append(h_ref[t] + (acc[0][:SUBLANES] + acc[0][SUBLANES:]))

        rows = []
        _staged_group(idx_ref, tbl_ref, (g0_ref, g1_ref), tb, base, step_fn)
        return tuple(rows)

    ngroup = tb // SUBLANES
    zeros = tuple(jnp.zeros((SUBLANES, LANES), F32) for _ in range(SUBLANES))
    finish(ngroup - 1, lax.fori_loop(0, ngroup, group, zeros))


def _peer_sum(idx_split, w, h, gain, tbl, tb):
    t, nrow = w.shape
    nsel = PEER_SEL
    tile = pl.BlockSpec((tb, SUBLANES, LANES), lambda i: (i, 0, 0))
    extra = [] if gain is None else [gain.reshape(SUBLANES, LANES)]
    c = jnp.arange(nrow)
    out_row = (c % 2) * PACK_ROWS + (c % (2 * PACK_ROWS)) // 2
    mask = (jnp.arange(SUBLANES)[:, None] == out_row[None, :]).astype(F32)
    gbuf = pltpu.VMEM((nsel * PACK_ROWS, LANES), jnp.uint32)
    const = lambda shape: pl.BlockSpec(shape, lambda i: (0, 0))
    return pl.pallas_call(
        functools.partial(_peer_sum_body, len(extra)),
        grid=(t // tb,),
        in_specs=_idx_specs(tb) + [
            pl.BlockSpec((tb, nrow), lambda i: (i, 0)),
            tile,
            const(mask.shape)] + [const((SUBLANES, LANES))] * len(extra) + [
            pl.BlockSpec(tbl.shape, lambda i: (0, 0), pipeline_mode=pl.Buffered(1))],
        out_specs=tile,
        out_shape=jax.ShapeDtypeStruct((t, SUBLANES, LANES), F32),
        scratch_shapes=[gbuf, gbuf],
        compiler_params=_params("arbitrary"),
        name="peer_sum",
    )(*idx_split, w, h.reshape(t, SUBLANES, LANES), mask, *extra, tbl)


def _pick_tile(n, pref):
    tile = min(n, pref)
    while n % tile:
        tile //= 2
    return tile


def _layer(h, mem, batch, seq, norm_mix, w_in, conv_w, dn_conv_w, dn_a_log, dn_dt_bias, dn_norm,
           w_out, norm_x, norm_mem, w_xq, w_xkv, w_xo, norm_ffn, w_pq, peer_keys,
           expert_u, expert_v, final_gain):
    t, d = h.shape
    mem_len = mem.shape[0] // batch
    in_cols = w_in.shape[1]
    pad = (-in_cols) % LANES
    w_in_p = jnp.pad(w_in, ((0, 0), (0, pad))).astype(BF16)

    proj = _norm_matmul(h, norm_mix, w_in_p, _pick_tile(t, 256), "in_proj")
    y_conv, q, k, v, gb = _mixer_pre(proj, conv_w, dn_conv_w, dn_a_log, dn_dt_bias,
                                     batch, seq, _pick_tile(seq, 256))
    o_dn = _delta_rule(q, k, v, gb, batch, seq)
    h = _mixer_out(h, y_conv, o_dn, proj, dn_norm, w_out, _pick_tile(t, 512))

    kv = _norm_matmul(mem, norm_mem, w_xkv.astype(BF16), _pick_tile(mem.shape[0], 256), "kv_proj")
    h = _cross_attn(h, norm_x, w_xq, kv, w_xo, batch, seq, mem_len, _pick_tile(seq, 512))

    xr, idx, gates = _peer_route(h, norm_ffn, w_pq, peer_keys, _pick_tile(t, 512))
    tb = _pick_tile(t, 64)
    idx_split = _split_idx(idx)
    act = _peer_dot(idx_split, xr, _pack_table(expert_u), tb)
    w = _peer_weights(act, gates, _pick_tile(t, 1024))
    out = _peer_sum(idx_split, w, h, final_gain, _pack_table(expert_v), tb)
    return out.reshape(t, d)


def kernel(x, mem, norm_mix, w_in, conv_w, dn_conv_w, dn_a_log, dn_dt_bias, dn_norm, w_out,
           norm_x, norm_mem, w_xq, w_xkv, w_xo, norm_ffn, w_pq, peer_keys, expert_u, expert_v,
           norm_final):
    batch, seq, d = x.shape
    depth = norm_mix.shape[0]
    h = x.reshape(batch * seq, d)
    mem2 = mem.reshape(batch * mem.shape[1], d)
    for layer in range(depth):
        h = _layer(h, mem2, batch, seq, norm_mix[layer], w_in[layer], conv_w[layer],
                   dn_conv_w[layer], dn_a_log[layer], dn_dt_bias[layer], dn_norm[layer],
                   w_out[layer], norm_x[layer], norm_mem[layer], w_xq[layer], w_xkv[layer],
                   w_xo[layer], norm_ffn[layer], w_pq[layer], peer_keys[layer],
                   expert_u[layer], expert_v[layer],
                   norm_final if layer == depth - 1 else None)
    return h.reshape(batch, seq, d)
```

```python
import functools

import jax
import jax.numpy as jnp
from jax import lax
from jax.experimental import pallas as pl
from jax.experimental.pallas import tpu as pltpu

EPS = 1e-6
F32 = jnp.float32
BF16 = jnp.bfloat16
HIGHEST = lax.Precision.HIGHEST

LANES = 128
SUBLANES = 8
VMEM_LIMIT = 56 * 1024 * 1024

CONV_GROUP_K = 3
DN_CONV_K = 4
DN_HEADS = 4
DN_HEAD_DIM = 128
DN_CHUNK = 64
X_HEADS = 4
PEER_HEADS = 8
PEER_KEYS = 128
PEER_TOPK = 16
PEER_HALF = 64


def _params(*sem):
    return pltpu.CompilerParams(dimension_semantics=sem, vmem_limit_bytes=VMEM_LIMIT)


def _rms(x, gain):
    ms = jnp.mean(x * x, axis=-1, keepdims=True)
    return x * lax.rsqrt(ms + EPS) * gain


def _bdot(a, b):
    return jnp.dot(a.astype(BF16), b.astype(BF16), preferred_element_type=F32)


def _hdot(a, b):
    return jnp.dot(a, b, preferred_element_type=F32, precision=HIGHEST)


def _hdot_nt(a, b):
    return lax.dot_general(a, b, (((1,), (1,)), ((), ())),
                           preferred_element_type=F32, precision=HIGHEST)


def _hdot_tn(a, b):
    return lax.dot_general(a, b, (((0,), (0,)), ((), ())),
                           preferred_element_type=F32, precision=HIGHEST)


def _norm_matmul_body(x_ref, g_ref, w_ref, o_ref):
    xn = _rms(x_ref[...], g_ref[...])
    o_ref[...] = jnp.dot(xn.astype(BF16), w_ref[...], preferred_element_type=F32)


def _norm_matmul(x, gain, w, tm, name):
    t, d = x.shape
    n = w.shape[1]
    return pl.pallas_call(
        _norm_matmul_body,
        grid=(t // tm,),
        in_specs=[pl.BlockSpec((tm, d), lambda i: (i, 0)),
                  pl.BlockSpec((1, d), lambda i: (0, 0)),
                  pl.BlockSpec((d, n), lambda i: (0, 0))],
        out_specs=pl.BlockSpec((tm, n), lambda i: (i, 0)),
        out_shape=jax.ShapeDtypeStruct((t, n), F32),
        compiler_params=_params("parallel"),
        name=name,
    )(x, gain.reshape(1, d), w)


def _shift_rows(x, halo, k):
    rolled = pltpu.roll(x, k, 0)
    hal = pltpu.roll(halo, k, 0)
    row = lax.broadcasted_iota(jnp.int32, (SUBLANES, x.shape[1]), 0)
    head = jnp.where(row < k, hal, rolled[:SUBLANES])
    return jnp.concatenate([head, rolled[SUBLANES:]], axis=0)


def _causal_conv(x, halo, w_ref):
    kk = w_ref.shape[0]
    y = x * w_ref[kk - 1:kk, :]
    for j in range(1, kk):
        y = y + _shift_rows(x, halo, j) * w_ref[kk - 1 - j:kk - j, :]
    return y


def _l2norm_heads(x):
    outs = []
    for h in range(DN_HEADS):
        xh = x[:, h * DN_HEAD_DIM:(h + 1) * DN_HEAD_DIM]
        outs.append(xh * lax.rsqrt(jnp.sum(xh * xh, axis=-1, keepdims=True) + EPS))
    return jnp.concatenate(outs, axis=-1)


def _mixer_pre_body(b_ref, c_ref, h_ref, q_ref, k_ref, v_ref, t_ref,
                    ch_ref, hh_ref, qh_ref, kh_ref, vh_ref,
                    cw_ref, qw_ref, kw_ref, vw_ref, alog_ref, dtb_ref, tril_ref,
                    yc_ref, qo_ref, ko_ref, vo_ref, gb_ref):
    first = pl.program_id(1) == 0
    keep = jnp.where(first, 0.0, 1.0).astype(F32)

    u = c_ref[...] * h_ref[...]
    uh = ch_ref[...] * hh_ref[...] * keep
    yc_ref[...] = b_ref[...] * _causal_conv(u, uh, cw_ref)

    def dn_branch(x_ref, xh_ref, w_ref):
        y = _causal_conv(x_ref[...], xh_ref[...] * keep, w_ref)
        return y * jax.nn.sigmoid(y)

    qo_ref[...] = _l2norm_heads(dn_branch(q_ref, qh_ref, qw_ref))
    ko_ref[...] = _l2norm_heads(dn_branch(k_ref, kh_ref, kw_ref))
    vo_ref[...] = dn_branch(v_ref, vh_ref, vw_ref)

    tail = t_ref[...]
    beta = jax.nn.sigmoid(tail)
    g = -jnp.exp(alog_ref[...]) * jax.nn.softplus(tail + dtb_ref[...])
    gcum = _hdot(tril_ref[...], g)
    lane = lax.broadcasted_iota(jnp.int32, tail.shape, 1)
    gb_ref[...] = jnp.where(lane < DN_HEADS, beta, gcum)


def _mixer_pre(proj, conv_w, dn_conv_w, dn_a_log, dn_dt_bias, batch, seq, ts):
    t = proj.shape[0]
    cw = 512
    nblk = seq // ts
    hb = ts // SUBLANES

    def cur(col):
        return pl.BlockSpec((ts, cw if col < 7 else LANES),
                            lambda b, i, col=col: (b * nblk + i, col))

    def halo(col):
        return pl.BlockSpec(
            (SUBLANES, cw),
            lambda b, i, col=col: (jnp.maximum((b * nblk + i) * hb - 1, 0), col))

    def full(shape):
        return pl.BlockSpec(shape, lambda b, i: (0,) * len(shape))

    qw, kw, vw = (dn_conv_w[:, j * cw:(j + 1) * cw] for j in range(3))
    lane_pad = jnp.zeros((LANES - 2 * DN_HEADS,), F32)
    alog = jnp.concatenate([jnp.zeros((DN_HEADS,), F32), dn_a_log, lane_pad]).reshape(1, LANES)
    dtb = jnp.concatenate([jnp.zeros((DN_HEADS,), F32), dn_dt_bias, lane_pad]).reshape(1, LANES)
    r = jnp.arange(ts)
    tril = ((r[:, None] >= r[None, :]) &
            (r[:, None] // DN_CHUNK == r[None, :] // DN_CHUNK)).astype(F32)

    tail_spec = pl.BlockSpec((ts, LANES), lambda b, i: (b * nblk + i, 7 * cw // LANES))
    out_tok = lambda w: pl.BlockSpec((ts, w), lambda b, i: (b * nblk + i, 0))
    return pl.pallas_call(
        _mixer_pre_body,
        grid=(batch, nblk),
        in_specs=[cur(0), cur(1), cur(2), cur(3), cur(4), cur(5), tail_spec,
                  halo(1), halo(2), halo(3), halo(4), halo(5),
                  full((CONV_GROUP_K, cw)), full((DN_CONV_K, cw)), full((DN_CONV_K, cw)),
                  full((DN_CONV_K, cw)), full((1, LANES)), full((1, LANES)), full((ts, ts))],
        out_specs=[out_tok(cw), out_tok(cw), out_tok(cw), out_tok(cw), out_tok(LANES)],
        out_shape=[jax.ShapeDtypeStruct((t, cw), F32)] * 4 + [jax.ShapeDtypeStruct((t, LANES), F32)],
        compiler_params=_params("parallel", "parallel"),
        name="mixer_pre",
    )(proj, proj, proj, proj, proj, proj, proj, proj, proj, proj, proj, proj,
      conv_w, qw, kw, vw, alog, dtb, tril)


_NN = (((1,), (0,)), ((), ()))
_NT = (((1,), (1,)), ((), ()))
_TN = (((0,), (0,)), ((), ()))
DN_LOCAL_PASSES = 1
DN_SCAN_PASSES = 1


def _mm(a, b, dims, passes):
    dot = lambda x, y: lax.dot_general(x, y, dims, preferred_element_type=F32)
    if passes == 6:
        return lax.dot_general(a, b, dims, preferred_element_type=F32, precision=HIGHEST)
    a_hi = a.astype(BF16)
    b_hi = b.astype(BF16)
    if passes == 1:
        return dot(a_hi, b_hi)
    a_lo = (a - a_hi.astype(F32)).astype(BF16)
    b_lo = (b - b_hi.astype(F32)).astype(BF16)
    return dot(a_hi, b_hi) + (dot(a_hi, b_lo) + dot(a_lo, b_hi))


DN_LOCAL_CHUNKS = 4


def _dn_local_body(q_ref, k_ref, v_ref, gb_ref, u_ref, w_ref, qd_ref, kd_ref, in_ref, gl_ref):
    c = DN_CHUNK
    n = DN_HEADS * c
    p = DN_LOCAL_PASSES
    cis = range(q_ref.shape[0] // c)
    each = lambda f, *xs: [f(*a) for a in zip(*xs)]
    lanes = lambda x: jnp.broadcast_to(x, (x.shape[0], LANES))

    def stack(ref, ci):
        return jnp.concatenate([ref[ci * c:(ci + 1) * c, h * DN_HEAD_DIM:(h + 1) * DN_HEAD_DIM]
                                for h in range(DN_HEADS)], axis=0)

    def col(gb, j):
        return jnp.concatenate([lanes(gb[:, j + h:j + h + 1]) for h in range(DN_HEADS)], axis=0)

    gb = [gb_ref[ci * c:(ci + 1) * c, :] for ci in cis]
    q = [stack(q_ref, ci) * (DN_HEAD_DIM ** -0.5) for ci in cis]
    k = [stack(k_ref, ci) for ci in cis]
    v = [stack(v_ref, ci) for ci in cis]
    beta = [col(x, 0) for x in gb]
    g = [col(x, DN_HEADS) for x in gb]
    g_last = [jnp.concatenate(
        [jnp.broadcast_to(x[c - 1:c, DN_HEADS + h:DN_HEADS + h + 1], (c, LANES))
         for h in range(DN_HEADS)], axis=0) for x in gb]

    row = lax.broadcasted_iota(jnp.int32, (n, n), 0)
    cl = lax.broadcasted_iota(jnp.int32, (n, n), 1)
    shift = c.bit_length() - 1
    same_head = (row >> shift) == (cl >> shift)
    causal = same_head & (row >= cl)
    strict = same_head & (row > cl)
    eye = (row == cl).astype(F32)
    decay = [jnp.where(causal, jnp.exp(jnp.where(causal, x[:, 0:1] - x.T[0:1, :], 0.0)), 0.0)
             for x in g]
    k_beta = each(lambda a, b: a * b, k, beta)
    v_beta = each(lambda a, b: a * b, v, beta)
    a = each(lambda kb, kk, d: jnp.where(strict, _mm(kb, kk, _NT, p) * d, 0.0), k_beta, k, decay)

    t_mat = [eye - x for x in a]
    pw = [_mm(x, x, _NN, p) for x in a]
    for _ in range(4):
        t_mat = each(lambda t, w_: t + _mm(t, w_, _NN, p), t_mat, pw)
        pw = [_mm(x, x, _NN, p) for x in pw]
    t_mat = each(lambda t, w_: t + _mm(t, w_, _NN, p), t_mat, pw)

    e_g = [jnp.exp(x) for x in g]
    uw = each(lambda t, vb, kb, e: _mm(t, jnp.concatenate([vb, kb * e], axis=1), _NN, p),
              t_mat, v_beta, k_beta, e_g)
    intra = each(lambda qq, kk, d: _mm(qq, kk, _NT, p) * d, q, k, decay)
    for ci in cis:
        u_ref[ci] = uw[ci][:, :DN_HEAD_DIM]
        w_ref[ci] = uw[ci][:, DN_HEAD_DIM:]
        in_ref[ci] = intra[ci]
        qd_ref[ci] = q[ci] * e_g[ci]
        kd_ref[ci] = k[ci] * jnp.exp(g_last[ci] - g[ci])
        gl_ref[ci] = jnp.concatenate(
            [jnp.exp(g_last[ci][h * c:h * c + 1]) for h in range(DN_HEADS)] +
            [jnp.zeros((SUBLANES - DN_HEADS, LANES), F32)], axis=0)


def _dn_scan_body(u_ref, w_ref, qd_ref, kd_ref, in_ref, gl_ref, o_ref, state_ref):
    c = DN_CHUNK
    p = DN_SCAN_PASSES

    @pl.when(pl.program_id(0) == 0)
    def _():
        state_ref[...] = jnp.zeros_like(state_ref)

    for b in range(u_ref.shape[0]):
        u = u_ref[b, 0]
        w = w_ref[b, 0]
        qd = qd_ref[b, 0]
        kd = kd_ref[b, 0]
        gl = gl_ref[b, 0]
        v_new, q_state = [], []
        for h in range(DN_HEADS):
            rows = slice(h * c, (h + 1) * c)
            ws = _mm(jnp.concatenate([w[rows], qd[rows]], axis=0), state_ref[b, h], _NN, p)
            v_new.append(u[rows] - ws[:c])
            q_state.append(ws[c:])
        o = jnp.concatenate(q_state, axis=0) + _mm(in_ref[b, 0], jnp.concatenate(v_new, axis=0),
                                                   _NN, p)
        for h in range(DN_HEADS):
            rows = slice(h * c, (h + 1) * c)
            state_ref[b, h] = (state_ref[b, h] * gl[h:h + 1] +
                               _mm(kd[rows], v_new[h], _TN, p))
        o_ref[b] = jnp.concatenate([o[h * c:(h + 1) * c] for h in range(DN_HEADS)], axis=1)


def _delta_rule(q, k, v, gb, batch, seq):
    t, w = q.shape
    nc = seq // DN_CHUNK
    n = DN_HEADS * DN_CHUNK
    cps = _pick_tile(batch * nc, DN_LOCAL_CHUNKS)
    tok = lambda width: pl.BlockSpec((cps * DN_CHUNK, width), lambda i: (i, 0))
    per_chunk = lambda rows, width: pl.BlockSpec((cps, rows, width), lambda i: (i, 0, 0))
    f = lambda rows, width: jax.ShapeDtypeStruct((batch * nc, rows, width), F32)
    u, wm, qd, kd, intra, gl = pl.pallas_call(
        _dn_local_body,
        grid=(batch * nc // cps,),
        in_specs=[tok(w), tok(w), tok(w), tok(LANES)],
        out_specs=[per_chunk(n, DN_HEAD_DIM)] * 4 + [per_chunk(n, n), per_chunk(SUBLANES, LANES)],
        out_shape=[f(n, DN_HEAD_DIM)] * 4 + [f(n, n), f(SUBLANES, LANES)],
        compiler_params=_params("parallel"),
        name="dn_local",
    )(q, k, v, gb)

    seq_blk = lambda rows, width: pl.BlockSpec((batch, 1, rows, width), lambda i: (0, i, 0, 0))
    by_batch = lambda x: x.reshape(batch, nc, *x.shape[1:])
    o = pl.pallas_call(
        _dn_scan_body,
        grid=(nc,),
        in_specs=[seq_blk(n, DN_HEAD_DIM)] * 4 + [seq_blk(n, n), seq_blk(SUBLANES, LANES)],
        out_specs=pl.BlockSpec((batch, DN_CHUNK, w), lambda i: (0, i, 0)),
        out_shape=jax.ShapeDtypeStruct((batch, seq, w), F32),
        scratch_shapes=[pltpu.VMEM((batch, DN_HEADS, DN_HEAD_DIM, DN_HEAD_DIM), F32)],
        compiler_params=_params("arbitrary"),
        name="dn_scan",
    )(by_batch(u), by_batch(wm), by_batch(qd), by_batch(kd), by_batch(intra), by_batch(gl))
    return o.reshape(t, w)


def _mixer_out_body(x_ref, yc_ref, o_ref, z_ref, dng_ref, wa_ref, wb_ref, h_ref):
    o = o_ref[...]
    z = z_ref[...]
    gain = dng_ref[...]
    parts = []
    for h in range(DN_HEADS):
        sl = slice(h * DN_HEAD_DIM, (h + 1) * DN_HEAD_DIM)
        zh = z[:, sl]
        parts.append(_rms(o[:, sl], gain) * (zh * jax.nn.sigmoid(zh)))
    y_dn = jnp.concatenate(parts, axis=-1)
    h_ref[...] = (x_ref[...] + _bdot(yc_ref[...], wa_ref[...]) + _bdot(y_dn, wb_ref[...]))


def _mixer_out(x, y_conv, o_dn, proj, dn_norm, w_out, tm):
    t, d = x.shape
    cw = y_conv.shape[1]
    tok = lambda width, col=0: pl.BlockSpec((tm, width), lambda i, col=col: (i, col))
    full = lambda shape: pl.BlockSpec(shape, lambda i: (0, 0))
    return pl.pallas_call(
        _mixer_out_body,
        grid=(t // tm,),
        in_specs=[tok(d), tok(cw), tok(cw), tok(cw, 6), full((1, DN_HEAD_DIM)),
                  full((cw, d)), full((cw, d))],
        out_specs=tok(d),
        out_shape=jax.ShapeDtypeStruct((t, d), F32),
        compiler_params=_params("parallel"),
        name="mixer_out",
    )(x, y_conv, o_dn, proj, dn_norm.reshape(1, DN_HEAD_DIM),
      w_out[:cw].astype(BF16), w_out[cw:].astype(BF16))


def _cross_attn_body(h_ref, g_ref, wq_ref, kv_ref, wo_ref, o_ref):
    h_in = h_ref[...]
    d = h_in.shape[1]
    dh = d // X_HEADS
    q = _bdot(_rms(h_in, g_ref[...]), wq_ref[...])
    outs = []
    for hd in range(X_HEADS):
        qh = q[:, hd * dh:(hd + 1) * dh].astype(BF16)
        kh = kv_ref[:, hd * dh:(hd + 1) * dh]
        vh = kv_ref[:, d + hd * dh:d + (hd + 1) * dh]
        s = lax.dot_general(qh, kh, (((1,), (1,)), ((), ())),
                            preferred_element_type=F32) * (dh ** -0.5)
        s = s - jnp.max(s, axis=-1, keepdims=True)
        e = jnp.exp(s)
        p = e / jnp.sum(e, axis=-1, keepdims=True)
        outs.append(jnp.dot(p.astype(BF16), vh, preferred_element_type=F32))
    o = jnp.concatenate(outs, axis=-1)
    o_ref[...] = h_in + _bdot(o, wo_ref[...])


def _cross_attn(h, gain, w_q, kv, w_o, batch, seq, mem_len, tm):
    t, d = h.shape
    nblk = seq // tm
    tok = pl.BlockSpec((tm, d), lambda b, i: (b * nblk + i, 0))
    full = lambda shape: pl.BlockSpec(shape, lambda b, i: (0, 0))
    return pl.pallas_call(
        _cross_attn_body,
        grid=(batch, nblk),
        in_specs=[tok, full((1, d)), full((d, d)),
                  pl.BlockSpec((mem_len, 2 * d), lambda b, i: (b, 0)), full((d, d))],
        out_specs=tok,
        out_shape=jax.ShapeDtypeStruct((t, d), F32),
        compiler_params=_params("parallel", "parallel"),
        name="cross_attn",
    )(h, gain.reshape(1, d), w_q.astype(BF16), kv.astype(BF16), w_o.astype(BF16))


def _top16_rows(s, payload=None):
    n = s.shape[0]
    row = lax.broadcasted_iota(jnp.int32, s.shape, 0).astype(F32)
    vals, pays = [], []
    for _ in range(PEER_TOPK):
        m = jnp.max(s, axis=0, keepdims=True)
        pos = jnp.min(jnp.where(s == m, row, float(n)), axis=0, keepdims=True)
        hit = row == pos
        vals.append(m)
        if payload is None:
            pays.append(pos)
        else:
            pays.append(jnp.sum(jnp.where(hit, payload, 0.0), axis=0, keepdims=True))
        s = jnp.where(hit, -jnp.inf, s)
    return jnp.concatenate(vals, axis=0), jnp.concatenate(pays, axis=0)


def _pair_candidates(v1, v2, pad):
    sub = lax.broadcasted_iota(jnp.int32, (SUBLANES, v1.shape[1]), 0)
    row = lambda v, a: jnp.broadcast_to(v[a:a + 1], sub.shape)
    lo8 = v2[:SUBLANES]
    blocks = [
        row(v1, 0) + lo8,
        row(v1, 0) + v2[SUBLANES:],
        row(v1, 1) + lo8,
        jnp.where(sub < 5, row(v1, 2) + lo8, pad),
        jnp.where(sub < 7,
                  jnp.where(sub < 4, row(v1, 3), row(v1, 4)) +
                  jnp.where(sub < 4, lo8, pltpu.roll(lo8, 4, 0)), pad),
        jnp.where(sub < 6,
                  jnp.where(sub < 2, row(v1, 5), jnp.where(sub < 4, row(v1, 6), row(v1, 7))) +
                  jnp.where((sub & 1) == 0, row(v2, 0), row(v2, 1)), pad),
        v1[SUBLANES:] + row(v2, 0),
    ]
    return jnp.concatenate(blocks, axis=0)


def _peer_route_body(h_ref, g_ref, wq_ref, keys_ref, xn_ref, idx_ref, gate_ref,
                     sc_ref, sel_ref, gt_ref):
    xn = _rms(h_ref[...], g_ref[...])
    for r in range(SUBLANES):
        xn_ref[:, r] = xn[:, r * LANES:(r + 1) * LANES].reshape(-1, SUBLANES, LANES)
    pq = _bdot(xn, wq_ref[...])
    ngrp = pq.shape[0] // LANES
    for j in range(2 * PEER_HEADS):
        qs = pq[:, j * PEER_HALF:(j + 1) * PEER_HALF].astype(BF16)
        sc = lax.dot_general(keys_ref[j], qs, (((1,), (1,)), ((), ())),
                             preferred_element_type=F32)
        for g in range(ngrp):
            sc_ref[j, g] = sc[:, g * LANES:(g + 1) * LANES]

    def head(h, carry):
        for g in range(ngrp):
            s1, i1 = _top16_rows(sc_ref[2 * h, g])
            s2, i2 = _top16_rows(sc_ref[2 * h + 1, g])
            cand = _pair_candidates(s1, s2, -jnp.inf)
            cand_idx = _pair_candidates(i1 * float(PEER_KEYS), i2, 0.0)
            best, sel = _top16_rows(cand, cand_idx)
            e = jnp.exp(best - jnp.max(best, axis=0, keepdims=True))
            sel_ref[g, h] = sel
            gt_ref[g, h] = e / jnp.sum(e, axis=0, keepdims=True)
        return carry

    lax.fori_loop(0, PEER_HEADS, head, 0)

    nsel = PEER_HEADS * PEER_TOPK
    for g in range(ngrp):
        rows = slice(g * LANES, (g + 1) * LANES)
        idx_ref[rows, :] = (sel_ref[g].reshape(nsel, LANES).T * float(PACK_ROWS)).astype(jnp.int32)
        gate_ref[rows, :] = gt_ref[g].reshape(nsel, LANES).T


def _peer_route(h, gain, w_pq, peer_keys, tm):
    t, d = h.shape
    nk = PEER_HEADS * 2
    ngrp = tm // LANES
    tok = lambda width: pl.BlockSpec((tm, width), lambda i: (i, 0))
    full = lambda shape: pl.BlockSpec(shape, lambda i: (0,) * len(shape))
    return pl.pallas_call(
        _peer_route_body,
        grid=(t // tm,),
        in_specs=[tok(d), full((1, d)), full((d, d)), full((nk, PEER_KEYS, PEER_HALF))],
        out_specs=[pl.BlockSpec((tm // SUBLANES, SUBLANES, SUBLANES, LANES), lambda i: (i, 0, 0, 0)),
                   tok(LANES), tok(LANES)],
        out_shape=[jax.ShapeDtypeStruct((t // SUBLANES, SUBLANES, SUBLANES, LANES), F32),
                   jax.ShapeDtypeStruct((t, LANES), jnp.int32),
                   jax.ShapeDtypeStruct((t, LANES), F32)],
        scratch_shapes=[pltpu.VMEM((nk, ngrp, PEER_KEYS, LANES), F32),
                        pltpu.VMEM((ngrp, PEER_HEADS, PEER_TOPK, LANES), F32),
                        pltpu.VMEM((ngrp, PEER_HEADS, PEER_TOPK, LANES), F32)],
        compiler_params=_params("parallel"),
        name="peer_route",
    )(h, gain.reshape(1, d), w_pq.astype(BF16),
      peer_keys.reshape(nk, PEER_KEYS, PEER_HALF).astype(BF16))


PACK_ROWS = 4


def _pack_table(tbl):
    e, d = tbl.shape
    half = d // 2
    b = tbl.astype(BF16)
    lo = lax.bitcast_convert_type(b[:, :half], jnp.uint16).astype(jnp.uint32)
    hi = lax.bitcast_convert_type(b[:, half:], jnp.uint16).astype(jnp.uint32)
    return ((hi << 16) | lo).reshape(e * PACK_ROWS, LANES)


def _unpack(slab):
    lo = pltpu.bitcast(slab << 16, F32)
    hi = pltpu.bitcast(slab & jnp.uint32(0xFFFF0000), F32)
    return lo, hi


STAGE_STEPS = 4
PEER_SEL = PEER_HEADS * PEER_TOPK
IDX_REFS = 8


def _split_idx(idx):
    t, nsel = idx.shape
    ncol = nsel // IDX_REFS
    parts = idx.reshape(t, IDX_REFS, ncol).transpose(1, 0, 2).reshape(IDX_REFS, t * ncol)
    return tuple(parts[r] for r in range(IDX_REFS))


def _idx_specs(tb):
    n = tb * PEER_SEL // IDX_REFS
    return [pl.BlockSpec((n,), lambda i: (i,), memory_space=pltpu.SMEM) for _ in range(IDX_REFS)]


def _stage_rows(idx_refs, t, tbl_ref, g_ref, step=None):
    ncol = PEER_SEL // IDX_REFS
    per = ncol // STAGE_STEPS
    qs = range(ncol) if step is None else range(step * per, (step + 1) * per)
    for q in qs:
        off = t * ncol + q
        for r in range(IDX_REFS):
            m = r * ncol + q
            row = pl.multiple_of(idx_refs[r][off], PACK_ROWS)
            g_ref[m * PACK_ROWS:(m + 1) * PACK_ROWS, :] = tbl_ref[pl.ds(row, PACK_ROWS), :]


def _bf16_pieces(x, n):
    pieces = []
    for _ in range(n - 1):
        p = x.astype(BF16).astype(F32)
        pieces.append(p)
        x = x - p
    pieces.append(x.astype(BF16).astype(F32))
    return pieces


def _staged_group(idx_ref, tbl_ref, bufs, tb, base, step_fn):
    for k in range(SUBLANES):
        t = base + k
        nxt = t + 1 if k + 1 < SUBLANES else jnp.minimum(t + 1, tb - 1)
        for step in range(STAGE_STEPS):
            _stage_rows(idx_ref, nxt, tbl_ref, bufs[(k + 1) % 2], step)
            step_fn(t, k, step, bufs[k % 2])


def _peer_dot_body(*refs):
    idx_ref = refs[:IDX_REFS]
    x_ref, tbl_ref, act_ref, g0_ref, g1_ref = refs[IDX_REFS:]
    tb, nsel = act_ref.shape
    nslab = SUBLANES * SUBLANES
    lane = lax.broadcasted_iota(jnp.int32, (LANES, LANES), 1)
    col_slab = lane & (nslab - 1)
    _stage_rows(idx_ref, 0, tbl_ref, g0_ref)

    def group(i, carry):
        base = pl.multiple_of(i * SUBLANES, SUBLANES)
        pieces = _bf16_pieces(x_ref[i].reshape(nslab, LANES), LANES // nslab)
        xt = jnp.concatenate(pieces, axis=0).T.astype(BF16)
        zero = jnp.zeros_like(xt)
        acc = [jnp.zeros((nsel, LANES), F32)]

        def step_fn(t, k, j, g_ref):
            lo, hi = _unpack(g_ref[pl.ds(j, nsel, stride=PACK_ROWS), :])
            lhs = jnp.concatenate([lo.astype(BF16), hi.astype(BF16)], axis=1)
            rhs = jnp.concatenate(
                [jnp.where(col_slab == SUBLANES * r + k, xt, zero) for r in (j, PACK_ROWS + j)],
                axis=0)
            acc[0] = acc[0] + jnp.dot(lhs, rhs, preferred_element_type=F32)

        _staged_group(idx_ref, tbl_ref, (g0_ref, g1_ref), tb, base, step_fn)
        out_t = acc[0].T
        act_ref[pl.ds(base, SUBLANES), :] = jnp.sum(
            out_t.reshape(LANES // SUBLANES, SUBLANES, nsel), axis=0)
        return carry

    lax.fori_loop(0, tb // SUBLANES, group, 0)


def _peer_dot(idx_split, xr, tbl, tb):
    t = xr.shape[0] * SUBLANES
    nsel = PEER_SEL
    gbuf = pltpu.VMEM((nsel * PACK_ROWS, LANES), jnp.uint32)
    return pl.pallas_call(
        _peer_dot_body,
        grid=(t // tb,),
        in_specs=_idx_specs(tb) + [
            pl.BlockSpec((tb // SUBLANES, SUBLANES, SUBLANES, LANES), lambda i: (i, 0, 0, 0)),
            pl.BlockSpec(tbl.shape, lambda i: (0, 0), pipeline_mode=pl.Buffered(1))],
        out_specs=pl.BlockSpec((tb, nsel), lambda i: (i, 0)),
        out_shape=jax.ShapeDtypeStruct((t, nsel), F32),
        scratch_shapes=[gbuf, gbuf],
        compiler_params=_params("arbitrary"),
        name="peer_dot",
    )(*idx_split, xr, tbl)


def _peer_weights_body(act_ref, gate_ref, rep_ref, w_ref):
    a = act_ref[...]
    w = gate_ref[...] * (0.5 * a * (1.0 + lax.erf(a * (2.0 ** -0.5))))
    rep = rep_ref[...]
    w_ref[...] = sum(jnp.dot(p.astype(BF16), rep, preferred_element_type=F32)
                     for p in _bf16_pieces(w, 3))


def _peer_weights(act, gates, tm):
    t, n = act.shape
    nrow = 2 * PACK_ROWS * n
    rep = (jnp.arange(n)[:, None] == (jnp.arange(nrow) // (2 * PACK_ROWS))[None, :]).astype(BF16)
    tok = lambda width: pl.BlockSpec((tm, width), lambda i: (i, 0))
    return pl.pallas_call(
        _peer_weights_body,
        grid=(t // tm,),
        in_specs=[tok(n), tok(n), pl.BlockSpec(rep.shape, lambda i: (0, 0))],
        out_specs=tok(nrow),
        out_shape=jax.ShapeDtypeStruct((t, nrow), F32),
        compiler_params=_params("parallel"),
        name="peer_weights",
    )(act, gates, rep)


def _peer_sum_body(has_gain, *refs):
    idx_ref = refs[:IDX_REFS]
    w_ref, h_ref, mask_ref = refs[IDX_REFS:IDX_REFS + 3]
    gain_ref = refs[IDX_REFS + 3] if has_gain else None
    tbl_ref, y_ref, g0_ref, g1_ref = refs[IDX_REFS + 3 + has_gain:]
    tb = y_ref.shape[0]
    mask = mask_ref[...]
    nrow = g0_ref.shape[0] // STAGE_STEPS
    ncol = mask.shape[1] // STAGE_STEPS
    _stage_rows(idx_ref, 0, tbl_ref, g0_ref)

    def finish(i, tiles):
        base = pl.multiple_of(i * SUBLANES, SUBLANES)
        slabs = []
        for r in range(SUBLANES):
            y_r = jnp.concatenate([tile[r:r + 1] for tile in tiles], axis=0)
            slabs.append(h_ref[pl.ds(base, SUBLANES), r * LANES:(r + 1) * LANES] + y_r)
        if gain_ref is not None:
            ms = sum(jnp.sum(s * s, axis=1, keepdims=True) for s in slabs)
            inv = lax.rsqrt(ms * (1.0 / (SUBLANES * LANES)) + EPS)
            slabs = [s * inv * gain_ref[:, r * LANES:(r + 1) * LANES]
                     for r, s in enumerate(slabs)]
        for r, s in enumerate(slabs):
            y_ref[pl.ds(base, SUBLANES), r * LANES:(r + 1) * LANES] = s

    def group(i, prev_rows):
        finish(jnp.maximum(i - 1, 0), prev_rows)
        base = pl.multiple_of(i * SUBLANES, SUBLANES)
        w_rep = w_ref[pl.ds(base, SUBLANES), :]
        acc = [None]

        def step_fn(t, k, step, g_ref):
            cols = slice(step * ncol, (step + 1) * ncol)
            wb = jnp.broadcast_to(w_rep[k:k + 1, cols], (SUBLANES, ncol)) * mask[:, cols]
            a_hi = wb.astype(BF16)
            a_lo = (wb - a_hi.astype(F32)).astype(BF16)
            lhs = jnp.concatenate([a_hi, a_lo], axis=0)
            staged = pltpu.bitcast(g_ref[step * nrow:(step + 1) * nrow, :], BF16)
            part = jnp.dot(lhs, staged, preferred_element_type=F32)
            acc[0] = part if step == 0 else acc[0] + part
            if step == STAGE_STEPS - 1:
                rows.append(acc[0][:SUBLANES] + acc[0][SUBLANES:])

        rows = []
        _staged_group(idx_ref, tbl_ref, (g0_ref, g1_ref), tb, base, step_fn)
        return tuple(rows)

    ngroup = tb // SUBLANES
    zeros = tuple(jnp.zeros((SUBLANES, LANES), F32) for _ in range(SUBLANES))
    finish(ngroup - 1, lax.fori_loop(0, ngroup, group, zeros))


def _peer_sum(idx_split, w, h, gain, tbl, tb):
    t, nrow = w.shape
    d = h.shape[1]
    nsel = PEER_SEL
    tile = pl.BlockSpec((tb, d), lambda i: (i, 0))
    extra = [] if gain is None else [gain.reshape(1, d)]
    c = jnp.arange(nrow)
    out_row = (c % 2) * PACK_ROWS + (c % (2 * PACK_ROWS)) // 2
    mask = (jnp.arange(SUBLANES)[:, None] == out_row[None, :]).astype(F32)
    gbuf = pltpu.VMEM((nsel * PACK_ROWS, LANES), jnp.uint32)
    const = lambda shape: pl.BlockSpec(shape, lambda i: (0, 0))
    return pl.pallas_call(
        functools.partial(_peer_sum_body, len(extra)),
        grid=(t // tb,),
        in_specs=_idx_specs(tb) + [
            pl.BlockSpec((tb, nrow), lambda i: (i, 0)),
            tile,
            const(mask.shape)] + [const((1, d))] * len(extra) + [
            pl.BlockSpec(tbl.shape, lambda i: (0, 0), pipeline_mode=pl.Buffered(1))],
        out_specs=tile,
        out_shape=jax.ShapeDtypeStruct((t, d), F32),
        scratch_shapes=[gbuf, gbuf],
        compiler_params=_params("arbitrary"),
        name="peer_sum",
    )(*idx_split, w, h, mask, *extra, tbl)


def _pick_tile(n, pref):
    tile = min(n, pref)
    while n % tile:
        tile //= 2
    return tile


def _layer(h, mem, batch, seq, norm_mix, w_in, conv_w, dn_conv_w, dn_a_log, dn_dt_bias, dn_norm,
           w_out, norm_x, norm_mem, w_xq, w_xkv, w_xo, norm_ffn, w_pq, peer_keys,
           expert_u, expert_v, final_gain):
    t, d = h.shape
    mem_len = mem.shape[0] // batch
    in_cols = w_in.shape[1]
    pad = (-in_cols) % LANES
    w_in_p = jnp.pad(w_in, ((0, 0), (0, pad))).astype(BF16)

    proj = _norm_matmul(h, norm_mix, w_in_p, _pick_tile(t, 256), "in_proj")
    y_conv, q, k, v, gb = _mixer_pre(proj, conv_w, dn_conv_w, dn_a_log, dn_dt_bias,
                                     batch, seq, _pick_tile(seq, 256))
    o_dn = _delta_rule(q, k, v, gb, batch, seq)
    h = _mixer_out(h, y_conv, o_dn, proj, dn_norm, w_out, _pick_tile(t, 512))

    kv = _norm_matmul(mem, norm_mem, w_xkv.astype(BF16), _pick_tile(mem.shape[0], 256), "kv_proj")
    h = _cross_attn(h, norm_x, w_xq, kv, w_xo, batch, seq, mem_len, _pick_tile(seq, 512))

    xr, idx, gates = _peer_route(h, norm_ffn, w_pq, peer_keys, _pick_tile(t, 512))
    tb = _pick_tile(t, 64)
    idx_split = _split_idx(idx)
    act = _peer_dot(idx_split, xr, _pack_table(expert_u), tb)
    w = _peer_weights(act, gates, _pick_tile(t, 1024))
    return _peer_sum(idx_split, w, h, final_gain, _pack_table(expert_v), tb)


def kernel(x, mem, norm_mix, w_in, conv_w, dn_conv_w, dn_a_log, dn_dt_bias, dn_norm, w_out,
           norm_x, norm_mem, w_xq, w_xkv, w_xo, norm_ffn, w_pq, peer_keys, expert_u, expert_v,
           norm_final):
    batch, seq, d = x.shape
    depth = norm_mix.shape[0]
    h = x.reshape(batch * seq, d)
    mem2 = mem.reshape(batch * mem.shape[1], d)
    for layer in range(depth):
        h = _layer(h, mem2, batch, seq, norm_mix[layer], w_in[layer], conv_w[layer],
                   dn_conv_w[layer], dn_a_log[layer], dn_dt_bias[layer], dn_norm[layer],
                   w_out[layer], norm_x[layer], norm_mem[layer], w_xq[layer], w_xkv[layer],
                   w_xo[layer], norm_ffn[layer], w_pq[layer], peer_keys[layer],
                   expert_u[layer], expert_v[layer],
                   norm_final if layer == depth - 1 else None)
    return h.reshape(batch, seq, d)
```

```python
import functools

import jax
import jax.numpy as jnp
from jax import lax
from jax.experimental import pallas as pl
from jax.experimental.pallas import tpu as pltpu

EPS = 1e-6
F32 = jnp.float32
BF16 = jnp.bfloat16
HIGHEST = lax.Precision.HIGHEST

LANES = 128
SUBLANES = 8
VMEM_LIMIT = 56 * 1024 * 1024

CONV_GROUP_K = 3
DN_CONV_K = 4
DN_HEADS = 4
DN_HEAD_DIM = 128
DN_CHUNK = 64
X_HEADS = 4
PEER_HEADS = 8
PEER_KEYS = 128
PEER_TOPK = 16
PEER_HALF = 64


def _params(*sem):
    return pltpu.CompilerParams(dimension_semantics=sem, vmem_limit_bytes=VMEM_LIMIT)


def _rms(x, gain):
    ms = jnp.mean(x * x, axis=-1, keepdims=True)
    return x * lax.rsqrt(ms + EPS) * gain


def _bdot(a, b):
    return jnp.dot(a.astype(BF16), b.astype(BF16), preferred_element_type=F32)


def _hdot(a, b):
    return jnp.dot(a, b, preferred_element_type=F32, precision=HIGHEST)


def _hdot_nt(a, b):
    return lax.dot_general(a, b, (((1,), (1,)), ((), ())),
                           preferred_element_type=F32, precision=HIGHEST)


def _hdot_tn(a, b):
    return lax.dot_general(a, b, (((0,), (0,)), ((), ())),
                           preferred_element_type=F32, precision=HIGHEST)


def _norm_matmul_body(x_ref, g_ref, w_ref, o_ref):
    xn = _rms(x_ref[...], g_ref[...])
    o_ref[...] = jnp.dot(xn.astype(BF16), w_ref[...], preferred_element_type=F32)


def _norm_matmul(x, gain, w, tm, name):
    t, d = x.shape
    n = w.shape[1]
    return pl.pallas_call(
        _norm_matmul_body,
        grid=(t // tm,),
        in_specs=[pl.BlockSpec((tm, d), lambda i: (i, 0)),
                  pl.BlockSpec((1, d), lambda i: (0, 0)),
                  pl.BlockSpec((d, n), lambda i: (0, 0))],
        out_specs=pl.BlockSpec((tm, n), lambda i: (i, 0)),
        out_shape=jax.ShapeDtypeStruct((t, n), F32),
        compiler_params=_params("parallel"),
        name=name,
    )(x, gain.reshape(1, d), w)


def _shift_rows(x, halo, k):
    rolled = pltpu.roll(x, k, 0)
    hal = pltpu.roll(halo, k, 0)
    row = lax.broadcasted_iota(jnp.int32, (SUBLANES, x.shape[1]), 0)
    head = jnp.where(row < k, hal, rolled[:SUBLANES])
    return jnp.concatenate([head, rolled[SUBLANES:]], axis=0)


def _causal_conv(x, halo, w_ref):
    kk = w_ref.shape[0]
    y = x * w_ref[kk - 1:kk, :]
    for j in range(1, kk):
        y = y + _shift_rows(x, halo, j) * w_ref[kk - 1 - j:kk - j, :]
    return y


def _l2norm_heads(x):
    outs = []
    for h in range(DN_HEADS):
        xh = x[:, h * DN_HEAD_DIM:(h + 1) * DN_HEAD_DIM]
        outs.append(xh * lax.rsqrt(jnp.sum(xh * xh, axis=-1, keepdims=True) + EPS))
    return jnp.concatenate(outs, axis=-1)


def _mixer_pre_body(b_ref, c_ref, h_ref, q_ref, k_ref, v_ref, t_ref,
                    ch_ref, hh_ref, qh_ref, kh_ref, vh_ref,
                    cw_ref, qw_ref, kw_ref, vw_ref, alog_ref, dtb_ref, tril_ref,
                    yc_ref, qo_ref, ko_ref, vo_ref, gb_ref):
    first = pl.program_id(1) == 0
    keep = jnp.where(first, 0.0, 1.0).astype(F32)

    u = c_ref[...] * h_ref[...]
    uh = ch_ref[...] * hh_ref[...] * keep
    yc_ref[...] = b_ref[...] * _causal_conv(u, uh, cw_ref)

    def dn_branch(x_ref, xh_ref, w_ref):
        y = _causal_conv(x_ref[...], xh_ref[...] * keep, w_ref)
        return y * jax.nn.sigmoid(y)

    qo_ref[...] = _l2norm_heads(dn_branch(q_ref, qh_ref, qw_ref))
    ko_ref[...] = _l2norm_heads(dn_branch(k_ref, kh_ref, kw_ref))
    vo_ref[...] = dn_branch(v_ref, vh_ref, vw_ref)

    tail = t_ref[...]
    beta = jax.nn.sigmoid(tail)
    g = -jnp.exp(alog_ref[...]) * jax.nn.softplus(tail + dtb_ref[...])
    gcum = _hdot(tril_ref[...], g)
    lane = lax.broadcasted_iota(jnp.int32, tail.shape, 1)
    gb_ref[...] = jnp.where(lane < DN_HEADS, beta, gcum)


def _mixer_pre(proj, conv_w, dn_conv_w, dn_a_log, dn_dt_bias, batch, seq, ts):
    t = proj.shape[0]
    cw = 512
    nblk = seq // ts
    hb = ts // SUBLANES

    def cur(col):
        return pl.BlockSpec((ts, cw if col < 7 else LANES),
                            lambda b, i, col=col: (b * nblk + i, col))

    def halo(col):
        return pl.BlockSpec(
            (SUBLANES, cw),
            lambda b, i, col=col: (jnp.maximum((b * nblk + i) * hb - 1, 0), col))

    def full(shape):
        return pl.BlockSpec(shape, lambda b, i: (0,) * len(shape))

    qw, kw, vw = (dn_conv_w[:, j * cw:(j + 1) * cw] for j in range(3))
    lane_pad = jnp.zeros((LANES - 2 * DN_HEADS,), F32)
    alog = jnp.concatenate([jnp.zeros((DN_HEADS,), F32), dn_a_log, lane_pad]).reshape(1, LANES)
    dtb = jnp.concatenate([jnp.zeros((DN_HEADS,), F32), dn_dt_bias, lane_pad]).reshape(1, LANES)
    r = jnp.arange(ts)
    tril = ((r[:, None] >= r[None, :]) &
            (r[:, None] // DN_CHUNK == r[None, :] // DN_CHUNK)).astype(F32)

    tail_spec = pl.BlockSpec((ts, LANES), lambda b, i: (b * nblk + i, 7 * cw // LANES))
    out_tok = lambda w: pl.BlockSpec((ts, w), lambda b, i: (b * nblk + i, 0))
    return pl.pallas_call(
        _mixer_pre_body,
        grid=(batch, nblk),
        in_specs=[cur(0), cur(1), cur(2), cur(3), cur(4), cur(5), tail_spec,
                  halo(1), halo(2), halo(3), halo(4), halo(5),
                  full((CONV_GROUP_K, cw)), full((DN_CONV_K, cw)), full((DN_CONV_K, cw)),
                  full((DN_CONV_K, cw)), full((1, LANES)), full((1, LANES)), full((ts, ts))],
        out_specs=[out_tok(cw), out_tok(cw), out_tok(cw), out_tok(cw), out_tok(LANES)],
        out_shape=[jax.ShapeDtypeStruct((t, cw), F32)] * 4 + [jax.ShapeDtypeStruct((t, LANES), F32)],
        compiler_params=_params("parallel", "parallel"),
        name="mixer_pre",
    )(proj, proj, proj, proj, proj, proj, proj, proj, proj, proj, proj, proj,
      conv_w, qw, kw, vw, alog, dtb, tril)


_NN = (((1,), (0,)), ((), ()))
_NT = (((1,), (1,)), ((), ()))
_TN = (((0,), (0,)), ((), ()))
DN_LOCAL_PASSES = 1
DN_SCAN_PASSES = 1


def _mm(a, b, dims, passes):
    dot = lambda x, y: lax.dot_general(x, y, dims, preferred_element_type=F32)
    if passes == 6:
        return lax.dot_general(a, b, dims, preferred_element_type=F32, precision=HIGHEST)
    a_hi = a.astype(BF16)
    b_hi = b.astype(BF16)
    if passes == 1:
        return dot(a_hi, b_hi)
    a_lo = (a - a_hi.astype(F32)).astype(BF16)
    b_lo = (b - b_hi.astype(F32)).astype(BF16)
    return dot(a_hi, b_hi) + (dot(a_hi, b_lo) + dot(a_lo, b_hi))


DN_LOCAL_CHUNKS = 4


def _dn_local_body(q_ref, k_ref, v_ref, gb_ref, u_ref, w_ref, qd_ref, kd_ref, in_ref, gl_ref):
    c = DN_CHUNK
    n = DN_HEADS * c
    p = DN_LOCAL_PASSES
    cis = range(q_ref.shape[0] // c)
    each = lambda f, *xs: [f(*a) for a in zip(*xs)]
    lanes = lambda x: jnp.broadcast_to(x, (x.shape[0], LANES))

    def stack(ref, ci):
        return jnp.concatenate([ref[ci * c:(ci + 1) * c, h * DN_HEAD_DIM:(h + 1) * DN_HEAD_DIM]
                                for h in range(DN_HEADS)], axis=0)

    def col(gb, j):
        return jnp.concatenate([lanes(gb[:, j + h:j + h + 1]) for h in range(DN_HEADS)], axis=0)

    gb = [gb_ref[ci * c:(ci + 1) * c, :] for ci in cis]
    q = [stack(q_ref, ci) * (DN_HEAD_DIM ** -0.5) for ci in cis]
    k = [stack(k_ref, ci) for ci in cis]
    v = [stack(v_ref, ci) for ci in cis]
    beta = [col(x, 0) for x in gb]
    g = [col(x, DN_HEADS) for x in gb]
    g_last = [jnp.concatenate(
        [jnp.broadcast_to(x[c - 1:c, DN_HEADS + h:DN_HEADS + h + 1], (c, LANES))
         for h in range(DN_HEADS)], axis=0) for x in gb]

    row = lax.broadcasted_iota(jnp.int32, (n, n), 0)
    cl = lax.broadcasted_iota(jnp.int32, (n, n), 1)
    shift = c.bit_length() - 1
    same_head = (row >> shift) == (cl >> shift)
    causal = same_head & (row >= cl)
    strict = same_head & (row > cl)
    eye = (row == cl).astype(F32)
    decay = [jnp.where(causal, jnp.exp(jnp.where(causal, x[:, 0:1] - x.T[0:1, :], 0.0)), 0.0)
             for x in g]
    k_beta = each(lambda a, b: a * b, k, beta)
    v_beta = each(lambda a, b: a * b, v, beta)
    a = each(lambda kb, kk, d: jnp.where(strict, _mm(kb, kk, _NT, p) * d, 0.0), k_beta, k, decay)

    t_mat = [eye - x for x in a]
    pw = [_mm(x, x, _NN, p) for x in a]
    for _ in range(4):
        t_mat = each(lambda t, w_: t + _mm(t, w_, _NN, p), t_mat, pw)
        pw = [_mm(x, x, _NN, p) for x in pw]
    t_mat = each(lambda t, w_: t + _mm(t, w_, _NN, p), t_mat, pw)

    e_g = [jnp.exp(x) for x in g]
    uw = each(lambda t, vb, kb, e: _mm(t, jnp.concatenate([vb, kb * e], axis=1), _NN, p),
              t_mat, v_beta, k_beta, e_g)
    intra = each(lambda qq, kk, d: _mm(qq, kk, _NT, p) * d, q, k, decay)
    for ci in cis:
        u_ref[ci] = uw[ci][:, :DN_HEAD_DIM]
        w_ref[ci] = uw[ci][:, DN_HEAD_DIM:]
        in_ref[ci] = intra[ci]
        qd_ref[ci] = q[ci] * e_g[ci]
        kd_ref[ci] = k[ci] * jnp.exp(g_last[ci] - g[ci])
        gl_ref[ci] = jnp.concatenate(
            [jnp.exp(g_last[ci][h * c:h * c + 1]) for h in range(DN_HEADS)] +
            [jnp.zeros((SUBLANES - DN_HEADS, LANES), F32)], axis=0)


def _dn_scan_body(u_ref, w_ref, qd_ref, kd_ref, in_ref, gl_ref, o_ref, state_ref):
    c = DN_CHUNK
    p = DN_SCAN_PASSES

    @pl.when(pl.program_id(0) == 0)
    def _():
        state_ref[...] = jnp.zeros_like(state_ref)

    for b in range(u_ref.shape[0]):
        u = u_ref[b, 0]
        w = w_ref[b, 0]
        qd = qd_ref[b, 0]
        kd = kd_ref[b, 0]
        gl = gl_ref[b, 0]
        v_new, q_state = [], []
        for h in range(DN_HEADS):
            rows = slice(h * c, (h + 1) * c)
            ws = _mm(jnp.concatenate([w[rows], qd[rows]], axis=0), state_ref[b, h], _NN, p)
            v_new.append(u[rows] - ws[:c])
            q_state.append(ws[c:])
        o = jnp.concatenate(q_state, axis=0) + _mm(in_ref[b, 0], jnp.concatenate(v_new, axis=0),
                                                   _NN, p)
        for h in range(DN_HEADS):
            rows = slice(h * c, (h + 1) * c)
            state_ref[b, h] = (state_ref[b, h] * gl[h:h + 1] +
                               _mm(kd[rows], v_new[h], _TN, p))
        o_ref[b] = jnp.concatenate([o[h * c:(h + 1) * c] for h in range(DN_HEADS)], axis=1)


def _delta_rule(q, k, v, gb, batch, seq):
    t, w = q.shape
    nc = seq // DN_CHUNK
    n = DN_HEADS * DN_CHUNK
    cps = _pick_tile(batch * nc, DN_LOCAL_CHUNKS)
    tok = lambda width: pl.BlockSpec((cps * DN_CHUNK, width), lambda i: (i, 0))
    per_chunk = lambda rows, width: pl.BlockSpec((cps, rows, width), lambda i: (i, 0, 0))
    f = lambda rows, width: jax.ShapeDtypeStruct((batch * nc, rows, width), F32)
    u, wm, qd, kd, intra, gl = pl.pallas_call(
        _dn_local_body,
        grid=(batch * nc // cps,),
        in_specs=[tok(w), tok(w), tok(w), tok(LANES)],
        out_specs=[per_chunk(n, DN_HEAD_DIM)] * 4 + [per_chunk(n, n), per_chunk(SUBLANES, LANES)],
        out_shape=[f(n, DN_HEAD_DIM)] * 4 + [f(n, n), f(SUBLANES, LANES)],
        compiler_params=_params("parallel"),
        name="dn_local",
    )(q, k, v, gb)

    seq_blk = lambda rows, width: pl.BlockSpec((batch, 1, rows, width), lambda i: (0, i, 0, 0))
    by_batch = lambda x: x.reshape(batch, nc, *x.shape[1:])
    o = pl.pallas_call(
        _dn_scan_body,
        grid=(nc,),
        in_specs=[seq_blk(n, DN_HEAD_DIM)] * 4 + [seq_blk(n, n), seq_blk(SUBLANES, LANES)],
        out_specs=pl.BlockSpec((batch, DN_CHUNK, w), lambda i: (0, i, 0)),
        out_shape=jax.ShapeDtypeStruct((batch, seq, w), F32),
        scratch_shapes=[pltpu.VMEM((batch, DN_HEADS, DN_HEAD_DIM, DN_HEAD_DIM), F32)],
        compiler_params=_params("arbitrary"),
        name="dn_scan",
    )(by_batch(u), by_batch(wm), by_batch(qd), by_batch(kd), by_batch(intra), by_batch(gl))
    return o.reshape(t, w)


def _mixer_out_body(x_ref, yc_ref, o_ref, z_ref, dng_ref, wa_ref, wb_ref, h_ref):
    o = o_ref[...]
    z = z_ref[...]
    gain = dng_ref[...]
    parts = []
    for h in range(DN_HEADS):
        sl = slice(h * DN_HEAD_DIM, (h + 1) * DN_HEAD_DIM)
        zh = z[:, sl]
        parts.append(_rms(o[:, sl], gain) * (zh * jax.nn.sigmoid(zh)))
    y_dn = jnp.concatenate(parts, axis=-1)
    h_ref[...] = (x_ref[...] + _bdot(yc_ref[...], wa_ref[...]) + _bdot(y_dn, wb_ref[...]))


def _mixer_out(x, y_conv, o_dn, proj, dn_norm, w_out, tm):
    t, d = x.shape
    cw = y_conv.shape[1]
    tok = lambda width, col=0: pl.BlockSpec((tm, width), lambda i, col=col: (i, col))
    full = lambda shape: pl.BlockSpec(shape, lambda i: (0, 0))
    return pl.pallas_call(
        _mixer_out_body,
        grid=(t // tm,),
        in_specs=[tok(d), tok(cw), tok(cw), tok(cw, 6), full((1, DN_HEAD_DIM)),
                  full((cw, d)), full((cw, d))],
        out_specs=tok(d),
        out_shape=jax.ShapeDtypeStruct((t, d), F32),
        compiler_params=_params("parallel"),
        name="mixer_out",
    )(x, y_conv, o_dn, proj, dn_norm.reshape(1, DN_HEAD_DIM),
      w_out[:cw].astype(BF16), w_out[cw:].astype(BF16))


def _cross_attn_body(h_ref, g_ref, wq_ref, kv_ref, wo_ref, o_ref):
    h_in = h_ref[...]
    d = h_in.shape[1]
    dh = d // X_HEADS
    q = _bdot(_rms(h_in, g_ref[...]), wq_ref[...])
    outs = []
    for hd in range(X_HEADS):
        qh = q[:, hd * dh:(hd + 1) * dh].astype(BF16)
        kh = kv_ref[:, hd * dh:(hd + 1) * dh]
        vh = kv_ref[:, d + hd * dh:d + (hd + 1) * dh]
        s = lax.dot_general(qh, kh, (((1,), (1,)), ((), ())),
                            preferred_element_type=F32) * (dh ** -0.5)
        s = s - jnp.max(s, axis=-1, keepdims=True)
        e = jnp.exp(s)
        p = e / jnp.sum(e, axis=-1, keepdims=True)
        outs.append(jnp.dot(p.astype(BF16), vh, preferred_element_type=F32))
    o = jnp.concatenate(outs, axis=-1)
    o_ref[...] = h_in + _bdot(o, wo_ref[...])


def _cross_attn(h, gain, w_q, kv, w_o, batch, seq, mem_len, tm):
    t, d = h.shape
    nblk = seq // tm
    tok = pl.BlockSpec((tm, d), lambda b, i: (b * nblk + i, 0))
    full = lambda shape: pl.BlockSpec(shape, lambda b, i: (0, 0))
    return pl.pallas_call(
        _cross_attn_body,
        grid=(batch, nblk),
        in_specs=[tok, full((1, d)), full((d, d)),
                  pl.BlockSpec((mem_len, 2 * d), lambda b, i: (b, 0)), full((d, d))],
        out_specs=tok,
        out_shape=jax.ShapeDtypeStruct((t, d), F32),
        compiler_params=_params("parallel", "parallel"),
        name="cross_attn",
    )(h, gain.reshape(1, d), w_q.astype(BF16), kv.astype(BF16), w_o.astype(BF16))


def _top16_rows(s, payload=None):
    n = s.shape[0]
    row = lax.broadcasted_iota(jnp.int32, s.shape, 0).astype(F32)
    vals, pays = [], []
    for _ in range(PEER_TOPK):
        m = jnp.max(s, axis=0, keepdims=True)
        pos = jnp.min(jnp.where(s == m, row, float(n)), axis=0, keepdims=True)
        hit = row == pos
        vals.append(m)
        if payload is None:
            pays.append(pos)
        else:
            pays.append(jnp.sum(jnp.where(hit, payload, 0.0), axis=0, keepdims=True))
        s = jnp.where(hit, -jnp.inf, s)
    return jnp.concatenate(vals, axis=0), jnp.concatenate(pays, axis=0)


def _pair_candidates(v1, v2, pad):
    sub = lax.broadcasted_iota(jnp.int32, (SUBLANES, v1.shape[1]), 0)
    row = lambda v, a: jnp.broadcast_to(v[a:a + 1], sub.shape)
    lo8 = v2[:SUBLANES]
    blocks = [
        row(v1, 0) + lo8,
        row(v1, 0) + v2[SUBLANES:],
        row(v1, 1) + lo8,
        jnp.where(sub < 5, row(v1, 2) + lo8, pad),
        jnp.where(sub < 7,
                  jnp.where(sub < 4, row(v1, 3), row(v1, 4)) +
                  jnp.where(sub < 4, lo8, pltpu.roll(lo8, 4, 0)), pad),
        jnp.where(sub < 6,
                  jnp.where(sub < 2, row(v1, 5), jnp.where(sub < 4, row(v1, 6), row(v1, 7))) +
                  jnp.where((sub & 1) == 0, row(v2, 0), row(v2, 1)), pad),
        v1[SUBLANES:] + row(v2, 0),
    ]
    return jnp.concatenate(blocks, axis=0)


def _peer_route_body(h_ref, g_ref, wq_ref, keys_ref, xn_ref, idx_ref, gate_ref,
                     sc_ref, sel_ref, gt_ref):
    xn = _rms(h_ref[...], g_ref[...])
    for r in range(SUBLANES):
        xn_ref[:, r] = xn[:, r * LANES:(r + 1) * LANES].reshape(-1, SUBLANES, LANES)
    pq = _bdot(xn, wq_ref[...])
    ngrp = pq.shape[0] // LANES
    for j in range(2 * PEER_HEADS):
        qs = pq[:, j * PEER_HALF:(j + 1) * PEER_HALF].astype(BF16)
        sc = lax.dot_general(keys_ref[j], qs, (((1,), (1,)), ((), ())),
                             preferred_element_type=F32)
        for g in range(ngrp):
            sc_ref[j, g] = sc[:, g * LANES:(g + 1) * LANES]

    def head(h, carry):
        for g in range(ngrp):
            s1, i1 = _top16_rows(sc_ref[2 * h, g])
            s2, i2 = _top16_rows(sc_ref[2 * h + 1, g])
            cand = _pair_candidates(s1, s2, -jnp.inf)
            cand_idx = _pair_candidates(i1 * float(PEER_KEYS), i2, 0.0)
            best, sel = _top16_rows(cand, cand_idx)
            e = jnp.exp(best - jnp.max(best, axis=0, keepdims=True))
            sel_ref[g, h] = sel
            gt_ref[g, h] = e / jnp.sum(e, axis=0, keepdims=True)
        return carry

    lax.fori_loop(0, PEER_HEADS, head, 0)

    nsel = PEER_HEADS * PEER_TOPK
    for g in range(ngrp):
        rows = slice(g * LANES, (g + 1) * LANES)
        idx_ref[rows, :] = (sel_ref[g].reshape(nsel, LANES).T * float(PACK_ROWS)).astype(jnp.int32)
        gate_ref[rows, :] = gt_ref[g].reshape(nsel, LANES).T


def _peer_route(h, gain, w_pq, peer_keys, tm):
    t, d = h.shape
    nk = PEER_HEADS * 2
    ngrp = tm // LANES
    tok = lambda width: pl.BlockSpec((tm, width), lambda i: (i, 0))
    full = lambda shape: pl.BlockSpec(shape, lambda i: (0,) * len(shape))
    return pl.pallas_call(
        _peer_route_body,
        grid=(t // tm,),
        in_specs=[tok(d), full((1, d)), full((d, d)), full((nk, PEER_KEYS, PEER_HALF))],
        out_specs=[pl.BlockSpec((tm // SUBLANES, SUBLANES, SUBLANES, LANES), lambda i: (i, 0, 0, 0)),
                   tok(LANES), tok(LANES)],
        out_shape=[jax.ShapeDtypeStruct((t // SUBLANES, SUBLANES, SUBLANES, LANES), F32),
                   jax.ShapeDtypeStruct((t, LANES), jnp.int32),
                   jax.ShapeDtypeStruct((t, LANES), F32)],
        scratch_shapes=[pltpu.VMEM((nk, ngrp, PEER_KEYS, LANES), F32),
                        pltpu.VMEM((ngrp, PEER_HEADS, PEER_TOPK, LANES), F32),
                        pltpu.VMEM((ngrp, PEER_HEADS, PEER_TOPK, LANES), F32)],
        compiler_params=_params("parallel"),
        name="peer_route",
    )(h, gain.reshape(1, d), w_pq.astype(BF16),
      peer_keys.reshape(nk, PEER_KEYS, PEER_HALF).astype(BF16))


PACK_ROWS = 4


def _pack_table(tbl):
    e, d = tbl.shape
    half = d // 2
    b = tbl.astype(BF16)
    lo = lax.bitcast_convert_type(b[:, :half], jnp.uint16).astype(jnp.uint32)
    hi = lax.bitcast_convert_type(b[:, half:], jnp.uint16).astype(jnp.uint32)
    return ((hi << 16) | lo).reshape(e * PACK_ROWS, LANES)


def _unpack(slab):
    lo = pltpu.bitcast(slab << 16, F32)
    hi = pltpu.bitcast(slab & jnp.uint32(0xFFFF0000), F32)
    return lo, hi


STAGE_STEPS = 4
PEER_SEL = PEER_HEADS * PEER_TOPK
IDX_REFS = 8


def _split_idx(idx):
    t, nsel = idx.shape
    ncol = nsel // IDX_REFS
    parts = idx.reshape(t, IDX_REFS, ncol).transpose(1, 0, 2).reshape(IDX_REFS, t * ncol)
    return tuple(parts[r] for r in range(IDX_REFS))


def _idx_specs(tb):
    n = tb * PEER_SEL // IDX_REFS
    return [pl.BlockSpec((n,), lambda i: (i,), memory_space=pltpu.SMEM) for _ in range(IDX_REFS)]


def _stage_rows(idx_refs, t, tbl_ref, g_ref, step=None):
    ncol = PEER_SEL // IDX_REFS
    per = ncol // STAGE_STEPS
    qs = range(ncol) if step is None else range(step * per, (step + 1) * per)
    for q in qs:
        off = t * ncol + q
        for r in range(IDX_REFS):
            m = r * ncol + q
            row = pl.multiple_of(idx_refs[r][off], PACK_ROWS)
            g_ref[m * PACK_ROWS:(m + 1) * PACK_ROWS, :] = tbl_ref[pl.ds(row, PACK_ROWS), :]


def _bf16_pieces(x, n):
    pieces = []
    for _ in range(n - 1):
        p = x.astype(BF16).astype(F32)
        pieces.append(p)
        x = x - p
    pieces.append(x.astype(BF16).astype(F32))
    return pieces


def _staged_group(idx_ref, tbl_ref, bufs, tb, base, step_fn):
    for k in range(SUBLANES):
        t = base + k
        nxt = t + 1 if k + 1 < SUBLANES else jnp.minimum(t + 1, tb - 1)
        for step in range(STAGE_STEPS):
            _stage_rows(idx_ref, nxt, tbl_ref, bufs[(k + 1) % 2], step)
            step_fn(t, k, step, bufs[k % 2])


def _peer_dot_body(*refs):
    idx_ref = refs[:IDX_REFS]
    x_ref, tbl_ref, act_ref, g0_ref, g1_ref = refs[IDX_REFS:]
    tb, nsel = act_ref.shape
    nslab = SUBLANES * SUBLANES
    lane = lax.broadcasted_iota(jnp.int32, (LANES, LANES), 1)
    col_slab = lane & (nslab - 1)
    _stage_rows(idx_ref, 0, tbl_ref, g0_ref)

    def group(i, carry):
        base = pl.multiple_of(i * SUBLANES, SUBLANES)
        pieces = _bf16_pieces(x_ref[i].reshape(nslab, LANES), LANES // nslab)
        xt = jnp.concatenate(pieces, axis=0).T.astype(BF16)
        zero = jnp.zeros_like(xt)
        acc = [jnp.zeros((nsel, LANES), F32)]

        def step_fn(t, k, j, g_ref):
            lo, hi = _unpack(g_ref[pl.ds(j, nsel, stride=PACK_ROWS), :])
            lhs = jnp.concatenate([lo.astype(BF16), hi.astype(BF16)], axis=1)
            rhs = jnp.concatenate(
                [jnp.where(col_slab == SUBLANES * r + k, xt, zero) for r in (j, PACK_ROWS + j)],
                axis=0)
            acc[0] = acc[0] + jnp.dot(lhs, rhs, preferred_element_type=F32)

        _staged_group(idx_ref, tbl_ref, (g0_ref, g1_ref), tb, base, step_fn)
        out_t = acc[0].T
        act_ref[pl.ds(base, SUBLANES), :] = jnp.sum(
            out_t.reshape(LANES // SUBLANES, SUBLANES, nsel), axis=0)
        return carry

    lax.fori_loop(0, tb // SUBLANES, group, 0)


def _peer_dot(idx_split, xr, tbl, tb):
    t = xr.shape[0] * SUBLANES
    nsel = PEER_SEL
    gbuf = pltpu.VMEM((nsel * PACK_ROWS, LANES), jnp.uint32)
    return pl.pallas_call(
        _peer_dot_body,
        grid=(t // tb,),
        in_specs=_idx_specs(tb) + [
            pl.BlockSpec((tb // SUBLANES, SUBLANES, SUBLANES, LANES), lambda i: (i, 0, 0, 0)),
            pl.BlockSpec(tbl.shape, lambda i: (0, 0), pipeline_mode=pl.Buffered(1))],
        out_specs=pl.BlockSpec((tb, nsel), lambda i: (i, 0)),
        out_shape=jax.ShapeDtypeStruct((t, nsel), F32),
        scratch_shapes=[gbuf, gbuf],
        compiler_params=_params("arbitrary"),
        name="peer_dot",
    )(*idx_split, xr, tbl)


def _peer_weights_body(act_ref, gate_ref, rep_ref, w_ref):
    a = act_ref[...]
    w = gate_ref[...] * (0.5 * a * (1.0 + lax.erf(a * (2.0 ** -0.5))))
    rep = rep_ref[...]
    w_ref[...] = sum(jnp.dot(p.astype(BF16), rep, preferred_element_type=F32)
                     for p in _bf16_pieces(w, 3))


def _peer_weights(act, gates, tm):
    t, n = act.shape
    nrow = 2 * PACK_ROWS * n
    rep = (jnp.arange(n)[:, None] == (jnp.arange(nrow) // (2 * PACK_ROWS))[None, :]).astype(BF16)
    tok = lambda width: pl.BlockSpec((tm, width), lambda i: (i, 0))
    return pl.pallas_call(
        _peer_weights_body,
        grid=(t // tm,),
        in_specs=[tok(n), tok(n), pl.BlockSpec(rep.shape, lambda i: (0, 0))],
        out_specs=tok(nrow),
        out_shape=jax.ShapeDtypeStruct((t, nrow), F32),
        compiler_params=_params("parallel"),
        name="peer_weights",
    )(act, gates, rep)


def _peer_sum_body(has_gain, *refs):
    idx_ref = refs[:IDX_REFS]
    w_ref, h_ref, mask_ref = refs[IDX_REFS:IDX_REFS + 3]
    gain_ref = refs[IDX_REFS + 3] if has_gain else None
    tbl_ref, y_ref, g0_ref, g1_ref = refs[IDX_REFS + 3 + has_gain:]
    tb = y_ref.shape[0]
    mask = mask_ref[...]
    nrow = g0_ref.shape[0] // STAGE_STEPS
    ncol = mask.shape[1] // STAGE_STEPS
    _stage_rows(idx_ref, 0, tbl_ref, g0_ref)

    def finish(i, tiles):
        base = pl.multiple_of(i * SUBLANES, SUBLANES)
        slabs = []
        for r in range(SUBLANES):
            y_r = jnp.concatenate([tile[r:r + 1] for tile in tiles], axis=0)
            slabs.append(h_ref[pl.ds(base, SUBLANES), r * LANES:(r + 1) * LANES] + y_r)
        if gain_ref is not None:
            ms = sum(jnp.sum(s * s, axis=1, keepdims=True) for s in slabs)
            inv = lax.rsqrt(ms * (1.0 / (SUBLANES * LANES)) + EPS)
            slabs = [s * inv * gain_ref[:, r * LANES:(r + 1) * LANES]
                     for r, s in enumerate(slabs)]
        for r, s in enumerate(slabs):
            y_ref[pl.ds(base, SUBLANES), r * LANES:(r + 1) * LANES] = s

    def group(i, prev_rows):
        finish(jnp.maximum(i - 1, 0), prev_rows)
        base = pl.multiple_of(i * SUBLANES, SUBLANES)
        w_rep = w_ref[pl.ds(base, SUBLANES), :]
        acc = [None]

        def step_fn(t, k, step, g_ref):
            cols = slice(step * ncol, (step + 1) * ncol)
            wb = jnp.broadcast_to(w_rep[k:k + 1, cols], (SUBLANES, ncol)) * mask[:, cols]
            a_hi = wb.astype(BF16)
            a_lo = (wb - a_hi.astype(F32)).astype(BF16)
            lhs = jnp.concatenate([a_hi, a_lo], axis=0)
            staged = pltpu.bitcast(g_ref[step * nrow:(step + 1) * nrow, :], BF16)
            part = jnp.dot(lhs, staged, preferred_element_type=F32)
            acc[0] = part if step == 0 else acc[0] + part
            if step == STAGE_STEPS - 1:
                rows.append(acc[0][:SUBLANES] + acc[0][SUBLANES:])

        rows = []
        _staged_group(idx_ref, tbl_ref, (g0_ref, g1_ref), tb, base, step_fn)
        return tuple(rows)

    ngroup = tb // SUBLANES
    zeros = tuple(jnp.zeros((SUBLANES, LANES), F32) for _ in range(SUBLANES))
    finish(ngroup - 1, lax.fori_loop(0, ngroup, group, zeros))


def _peer_sum(idx_split, w, h, gain, tbl, tb):
    t, nrow = w.shape
    d = h.shape[1]
    nsel = PEER_SEL
    tile = pl.BlockSpec((tb, d), lambda i: (i, 0))
    extra = [] if gain is None else [gain.reshape(1, d)]
    c = jnp.arange(nrow)
    out_row = (c % 2) * PACK_ROWS + (c % (2 * PACK_ROWS)) // 2
    mask = (jnp.arange(SUBLANES)[:, None] == out_row[None, :]).astype(F32)
    gbuf = pltpu.VMEM((nsel * PACK_ROWS, LANES), jnp.uint32)
    const = lambda shape: pl.BlockSpec(shape, lambda i: (0, 0))
    return pl.pallas_call(
        functools.partial(_peer_sum_body, len(extra)),
        grid=(t // tb,),
        in_specs=_idx_specs(tb) + [
            pl.BlockSpec((tb, nrow), lambda i: (i, 0)),
            tile,
            const(mask.shape)] + [const((1, d))] * len(extra) + [
            pl.BlockSpec(tbl.shape, lambda i: (0, 0), pipeline_mode=pl.Buffered(1))],
        out_specs=tile,
        out_shape=jax.ShapeDtypeStruct((t, d), F32),
        scratch_shapes=[gbuf, gbuf],
        compiler_params=_params("arbitrary"),
        name="peer_sum",
    )(*idx_split, w, h, mask, *extra, tbl)


def _pick_tile(n, pref):
    tile = min(n, pref)
    while n % tile:
        tile //= 2
    return tile


def _layer(h, mem, batch, seq, norm_mix, w_in, conv_w, dn_conv_w, dn_a_log, dn_dt_bias, dn_norm,
           w_out, norm_x, norm_mem, w_xq, w_xkv, w_xo, norm_ffn, w_pq, peer_keys,
           expert_u, expert_v, final_gain):
    t, d = h.shape
    mem_len = mem.shape[0] // batch
    in_cols = w_in.shape[1]
    pad = (-in_cols) % LANES
    w_in_p = jnp.pad(w_in, ((0, 0), (0, pad))).astype(BF16)

    proj = _norm_matmul(h, norm_mix, w_in_p, _pick_tile(t, 256), "in_proj")
    y_conv, q, k, v, gb = _mixer_pre(proj, conv_w, dn_conv_w, dn_a_log, dn_dt_bias,
                                     batch, seq, _pick_tile(seq, 256))
    o_dn = _delta_rule(q, k, v, gb, batch, seq)
    h = _mixer_out(h, y_conv, o_dn, proj, dn_norm, w_out, _pick_tile(t, 512))

    kv = _norm_matmul(mem, norm_mem, w_xkv.astype(BF16), _pick_tile(mem.shape[0], 256), "kv_proj")
    h = _cross_attn(h, norm_x, w_xq, kv, w_xo, batch, seq, mem_len, _pick_tile(seq, 512))

    xr, idx, gates = _peer_route(h, norm_ffn, w_pq, peer_keys, _pick_tile(t, 512))
    tb = _pick_tile(t, 128)
    idx_split = _split_idx(idx)
    act = _peer_dot(idx_split, xr, _pack_table(expert_u), tb)
    w = _peer_weights(act, gates, _pick_tile(t, 1024))
    return _peer_sum(idx_split, w, h, final_gain, _pack_table(expert_v), tb)


def kernel(x, mem, norm_mix, w_in, conv_w, dn_conv_w, dn_a_log, dn_dt_bias, dn_norm, w_out,
           norm_x, norm_mem, w_xq, w_xkv, w_xo, norm_ffn, w_pq, peer_keys, expert_u, expert_v,
           norm_final):
    batch, seq, d = x.shape
    depth = norm_mix.shape[0]
    h = x.reshape(batch * seq, d)
    mem2 = mem.reshape(batch * mem.shape[1], d)
    for layer in range(depth):
        h = _layer(h, mem2, batch, seq, norm_mix[layer], w_in[layer], conv_w[layer],
                   dn_conv_w[layer], dn_a_log[layer], dn_dt_bias[layer], dn_norm[layer],
                   w_out[layer], norm_x[layer], norm_mem[layer], w_xq[layer], w_xkv[layer],
                   w_xo[layer], norm_ffn[layer], w_pq[layer], peer_keys[layer],
                   expert_u[layer], expert_v[layer],
                   norm_final if layer == depth - 1 else None)
    return h.reshape(batch, seq, d)
```

```python
import functools

import jax
import jax.numpy as jnp
from jax import lax
from jax.experimental import pallas as pl
from jax.experimental.pallas import tpu as pltpu

EPS = 1e-6
F32 = jnp.float32
BF16 = jnp.bfloat16
HIGHEST = lax.Precision.HIGHEST

LANES = 128
SUBLANES = 8
VMEM_LIMIT = 56 * 1024 * 1024

CONV_GROUP_K = 3
DN_CONV_K = 4
DN_HEADS = 4
DN_HEAD_DIM = 128
DN_CHUNK = 64
X_HEADS = 4
PEER_HEADS = 8
PEER_KEYS = 128
PEER_TOPK = 16
PEER_HALF = 64


def _params(*sem):
    return pltpu.CompilerParams(dimension_semantics=sem, vmem_limit_bytes=VMEM_LIMIT)


def _rms(x, gain):
    ms = jnp.mean(x * x, axis=-1, keepdims=True)
    return x * lax.rsqrt(ms + EPS) * gain


def _bdot(a, b):
    return jnp.dot(a.astype(BF16), b.astype(BF16), preferred_element_type=F32)


def _hdot(a, b):
    return jnp.dot(a, b, preferred_element_type=F32, precision=HIGHEST)


def _hdot_nt(a, b):
    return lax.dot_general(a, b, (((1,), (1,)), ((), ())),
                           preferred_element_type=F32, precision=HIGHEST)


def _hdot_tn(a, b):
    return lax.dot_general(a, b, (((0,), (0,)), ((), ())),
                           preferred_element_type=F32, precision=HIGHEST)


def _norm_matmul_body(x_ref, g_ref, w_ref, o_ref):
    xn = _rms(x_ref[...], g_ref[...])
    o_ref[...] = jnp.dot(xn.astype(BF16), w_ref[...], preferred_element_type=F32)


def _norm_matmul(x, gain, w, tm, name):
    t, d = x.shape
    n = w.shape[1]
    return pl.pallas_call(
        _norm_matmul_body,
        grid=(t // tm,),
        in_specs=[pl.BlockSpec((tm, d), lambda i: (i, 0)),
                  pl.BlockSpec((1, d), lambda i: (0, 0)),
                  pl.BlockSpec((d, n), lambda i: (0, 0))],
        out_specs=pl.BlockSpec((tm, n), lambda i: (i, 0)),
        out_shape=jax.ShapeDtypeStruct((t, n), F32),
        compiler_params=_params("parallel"),
        name=name,
    )(x, gain.reshape(1, d), w)


def _shift_rows(x, halo, k):
    rolled = pltpu.roll(x, k, 0)
    hal = pltpu.roll(halo, k, 0)
    row = lax.broadcasted_iota(jnp.int32, (SUBLANES, x.shape[1]), 0)
    head = jnp.where(row < k, hal, rolled[:SUBLANES])
    return jnp.concatenate([head, rolled[SUBLANES:]], axis=0)


def _causal_conv(x, halo, w_ref):
    kk = w_ref.shape[0]
    y = x * w_ref[kk - 1:kk, :]
    for j in range(1, kk):
        y = y + _shift_rows(x, halo, j) * w_ref[kk - 1 - j:kk - j, :]
    return y


def _l2norm_heads(x):
    outs = []
    for h in range(DN_HEADS):
        xh = x[:, h * DN_HEAD_DIM:(h + 1) * DN_HEAD_DIM]
        outs.append(xh * lax.rsqrt(jnp.sum(xh * xh, axis=-1, keepdims=True) + EPS))
    return jnp.concatenate(outs, axis=-1)


def _mixer_pre_body(b_ref, c_ref, h_ref, q_ref, k_ref, v_ref, t_ref,
                    ch_ref, hh_ref, qh_ref, kh_ref, vh_ref,
                    cw_ref, qw_ref, kw_ref, vw_ref, alog_ref, dtb_ref, tril_ref,
                    yc_ref, qo_ref, ko_ref, vo_ref, gb_ref):
    first = pl.program_id(1) == 0
    keep = jnp.where(first, 0.0, 1.0).astype(F32)

    u = c_ref[...] * h_ref[...]
    uh = ch_ref[...] * hh_ref[...] * keep
    yc_ref[...] = b_ref[...] * _causal_conv(u, uh, cw_ref)

    def dn_branch(x_ref, xh_ref, w_ref):
        y = _causal_conv(x_ref[...], xh_ref[...] * keep, w_ref)
        return y * jax.nn.sigmoid(y)

    qo_ref[...] = _l2norm_heads(dn_branch(q_ref, qh_ref, qw_ref))
    ko_ref[...] = _l2norm_heads(dn_branch(k_ref, kh_ref, kw_ref))
    vo_ref[...] = dn_branch(v_ref, vh_ref, vw_ref)

    tail = t_ref[...]
    beta = jax.nn.sigmoid(tail)
    g = -jnp.exp(alog_ref[...]) * jax.nn.softplus(tail + dtb_ref[...])
    gcum = _hdot(tril_ref[...], g)
    lane = lax.broadcasted_iota(jnp.int32, tail.shape, 1)
    gb_ref[...] = jnp.where(lane < DN_HEADS, beta, gcum)


def _mixer_pre(proj, conv_w, dn_conv_w, dn_a_log, dn_dt_bias, batch, seq, ts):
    t = proj.shape[0]
    cw = 512
    nblk = seq // ts
    hb = ts // SUBLANES

    def cur(col):
        return pl.BlockSpec((ts, cw if col < 7 else LANES),
                            lambda b, i, col=col: (b * nblk + i, col))

    def halo(col):
        return pl.BlockSpec(
            (SUBLANES, cw),
            lambda b, i, col=col: (jnp.maximum((b * nblk + i) * hb - 1, 0), col))

    def full(shape):
        return pl.BlockSpec(shape, lambda b, i: (0,) * len(shape))

    qw, kw, vw = (dn_conv_w[:, j * cw:(j + 1) * cw] for j in range(3))
    lane_pad = jnp.zeros((LANES - 2 * DN_HEADS,), F32)
    alog = jnp.concatenate([jnp.zeros((DN_HEADS,), F32), dn_a_log, lane_pad]).reshape(1, LANES)
    dtb = jnp.concatenate([jnp.zeros((DN_HEADS,), F32), dn_dt_bias, lane_pad]).reshape(1, LANES)
    r = jnp.arange(ts)
    tril = ((r[:, None] >= r[None, :]) &
            (r[:, None] // DN_CHUNK == r[None, :] // DN_CHUNK)).astype(F32)

    tail_spec = pl.BlockSpec((ts, LANES), lambda b, i: (b * nblk + i, 7 * cw // LANES))
    out_tok = lambda w: pl.BlockSpec((ts, w), lambda b, i: (b * nblk + i, 0))
    return pl.pallas_call(
        _mixer_pre_body,
        grid=(batch, nblk),
        in_specs=[cur(0), cur(1), cur(2), cur(3), cur(4), cur(5), tail_spec,
                  halo(1), halo(2), halo(3), halo(4), halo(5),
                  full((CONV_GROUP_K, cw)), full((DN_CONV_K, cw)), full((DN_CONV_K, cw)),
                  full((DN_CONV_K, cw)), full((1, LANES)), full((1, LANES)), full((ts, ts))],
        out_specs=[out_tok(cw), out_tok(cw), out_tok(cw), out_tok(cw), out_tok(LANES)],
        out_shape=[jax.ShapeDtypeStruct((t, cw), F32)] * 4 + [jax.ShapeDtypeStruct((t, LANES), F32)],
        compiler_params=_params("parallel", "parallel"),
        name="mixer_pre",
    )(proj, proj, proj, proj, proj, proj, proj, proj, proj, proj, proj, proj,
      conv_w, qw, kw, vw, alog, dtb, tril)


_NN = (((1,), (0,)), ((), ()))
_NT = (((1,), (1,)), ((), ()))
_TN = (((0,), (0,)), ((), ()))
DN_LOCAL_PASSES = 1
DN_SCAN_PASSES = 1


def _mm(a, b, dims, passes):
    dot = lambda x, y: lax.dot_general(x, y, dims, preferred_element_type=F32)
    if passes == 6:
        return lax.dot_general(a, b, dims, preferred_element_type=F32, precision=HIGHEST)
    a_hi = a.astype(BF16)
    b_hi = b.astype(BF16)
    if passes == 1:
        return dot(a_hi, b_hi)
    a_lo = (a - a_hi.astype(F32)).astype(BF16)
    b_lo = (b - b_hi.astype(F32)).astype(BF16)
    return dot(a_hi, b_hi) + (dot(a_hi, b_lo) + dot(a_lo, b_hi))


DN_LOCAL_CHUNKS = 4


def _dn_local_body(q_ref, k_ref, v_ref, gb_ref, u_ref, w_ref, qd_ref, kd_ref, in_ref, gl_ref):
    c = DN_CHUNK
    n = DN_HEADS * c
    p = DN_LOCAL_PASSES
    cis = range(q_ref.shape[0] // c)
    each = lambda f, *xs: [f(*a) for a in zip(*xs)]
    lanes = lambda x: jnp.broadcast_to(x, (x.shape[0], LANES))

    def stack(ref, ci):
        return jnp.concatenate([ref[ci * c:(ci + 1) * c, h * DN_HEAD_DIM:(h + 1) * DN_HEAD_DIM]
                                for h in range(DN_HEADS)], axis=0)

    def col(gb, j):
        return jnp.concatenate([lanes(gb[:, j + h:j + h + 1]) for h in range(DN_HEADS)], axis=0)

    gb = [gb_ref[ci * c:(ci + 1) * c, :] for ci in cis]
    q = [stack(q_ref, ci) * (DN_HEAD_DIM ** -0.5) for ci in cis]
    k = [stack(k_ref, ci) for ci in cis]
    v = [stack(v_ref, ci) for ci in cis]
    beta = [col(x, 0) for x in gb]
    g = [col(x, DN_HEADS) for x in gb]
    g_last = [jnp.concatenate(
        [jnp.broadcast_to(x[c - 1:c, DN_HEADS + h:DN_HEADS + h + 1], (c, LANES))
         for h in range(DN_HEADS)], axis=0) for x in gb]

    row = lax.broadcasted_iota(jnp.int32, (n, n), 0)
    cl = lax.broadcasted_iota(jnp.int32, (n, n), 1)
    shift = c.bit_length() - 1
    same_head = (row >> shift) == (cl >> shift)
    causal = same_head & (row >= cl)
    strict = same_head & (row > cl)
    eye = (row == cl).astype(F32)
    decay = [jnp.where(causal, jnp.exp(jnp.where(causal, x[:, 0:1] - x.T[0:1, :], 0.0)), 0.0)
             for x in g]
    k_beta = each(lambda a, b: a * b, k, beta)
    v_beta = each(lambda a, b: a * b, v, beta)
    a = each(lambda kb, kk, d: jnp.where(strict, _mm(kb, kk, _NT, p) * d, 0.0), k_beta, k, decay)

    t_mat = [eye - x for x in a]
    pw = [_mm(x, x, _NN, p) for x in a]
    for _ in range(4):
        t_mat = each(lambda t, w_: t + _mm(t, w_, _NN, p), t_mat, pw)
        pw = [_mm(x, x, _NN, p) for x in pw]
    t_mat = each(lambda t, w_: t + _mm(t, w_, _NN, p), t_mat, pw)

    e_g = [jnp.exp(x) for x in g]
    uw = each(lambda t, vb, kb, e: _mm(t, jnp.concatenate([vb, kb * e], axis=1), _NN, p),
              t_mat, v_beta, k_beta, e_g)
    intra = each(lambda qq, kk, d: _mm(qq, kk, _NT, p) * d, q, k, decay)
    for ci in cis:
        u_ref[ci] = uw[ci][:, :DN_HEAD_DIM]
        w_ref[ci] = uw[ci][:, DN_HEAD_DIM:]
        in_ref[ci] = intra[ci]
        qd_ref[ci] = q[ci] * e_g[ci]
        kd_ref[ci] = k[ci] * jnp.exp(g_last[ci] - g[ci])
        gl_ref[ci] = jnp.concatenate(
            [jnp.exp(g_last[ci][h * c:h * c + 1]) for h in range(DN_HEADS)] +
            [jnp.zeros((SUBLANES - DN_HEADS, LANES), F32)], axis=0)


def _dn_scan_body(u_ref, w_ref, qd_ref, kd_ref, in_ref, gl_ref, o_ref, state_ref):
    c = DN_CHUNK
    p = DN_SCAN_PASSES

    @pl.when(pl.program_id(0) == 0)
    def _():
        state_ref[...] = jnp.zeros_like(state_ref)

    for b in range(u_ref.shape[0]):
        u = u_ref[b, 0]
        w = w_ref[b, 0]
        qd = qd_ref[b, 0]
        kd = kd_ref[b, 0]
        gl = gl_ref[b, 0]
        v_new, q_state = [], []
        for h in range(DN_HEADS):
            rows = slice(h * c, (h + 1) * c)
            ws = _mm(jnp.concatenate([w[rows], qd[rows]], axis=0), state_ref[b, h], _NN, p)
            v_new.append(u[rows] - ws[:c])
            q_state.append(ws[c:])
        o = jnp.concatenate(q_state, axis=0) + _mm(in_ref[b, 0], jnp.concatenate(v_new, axis=0),
                                                   _NN, p)
        for h in range(DN_HEADS):
            rows = slice(h * c, (h + 1) * c)
            state_ref[b, h] = (state_ref[b, h] * gl[h:h + 1] +
                               _mm(kd[rows], v_new[h], _TN, p))
        o_ref[b] = jnp.concatenate([o[h * c:(h + 1) * c] for h in range(DN_HEADS)], axis=1)


def _delta_rule(q, k, v, gb, batch, seq):
    t, w = q.shape
    nc = seq // DN_CHUNK
    n = DN_HEADS * DN_CHUNK
    cps = _pick_tile(batch * nc, DN_LOCAL_CHUNKS)
    tok = lambda width: pl.BlockSpec((cps * DN_CHUNK, width), lambda i: (i, 0))
    per_chunk = lambda rows, width: pl.BlockSpec((cps, rows, width), lambda i: (i, 0, 0))
    f = lambda rows, width: jax.ShapeDtypeStruct((batch * nc, rows, width), F32)
    u, wm, qd, kd, intra, gl = pl.pallas_call(
        _dn_local_body,
        grid=(batch * nc // cps,),
        in_specs=[tok(w), tok(w), tok(w), tok(LANES)],
        out_specs=[per_chunk(n, DN_HEAD_DIM)] * 4 + [per_chunk(n, n), per_chunk(SUBLANES, LANES)],
        out_shape=[f(n, DN_HEAD_DIM)] * 4 + [f(n, n), f(SUBLANES, LANES)],
        compiler_params=_params("parallel"),
        name="dn_local",
    )(q, k, v, gb)

    seq_blk = lambda rows, width: pl.BlockSpec((batch, 1, rows, width), lambda i: (0, i, 0, 0))
    by_batch = lambda x: x.reshape(batch, nc, *x.shape[1:])
    o = pl.pallas_call(
        _dn_scan_body,
        grid=(nc,),
        in_specs=[seq_blk(n, DN_HEAD_DIM)] * 4 + [seq_blk(n, n), seq_blk(SUBLANES, LANES)],
        out_specs=pl.BlockSpec((batch, DN_CHUNK, w), lambda i: (0, i, 0)),
        out_shape=jax.ShapeDtypeStruct((batch, seq, w), F32),
        scratch_shapes=[pltpu.VMEM((batch, DN_HEADS, DN_HEAD_DIM, DN_HEAD_DIM), F32)],
        compiler_params=_params("arbitrary"),
        name="dn_scan",
    )(by_batch(u), by_batch(wm), by_batch(qd), by_batch(kd), by_batch(intra), by_batch(gl))
    return o.reshape(t, w)


def _mixer_out_body(x_ref, yc_ref, o_ref, z_ref, dng_ref, wa_ref, wb_ref, h_ref):
    o = o_ref[...]
    z = z_ref[...]
    gain = dng_ref[...]
    parts = []
    for h in range(DN_HEADS):
        sl = slice(h * DN_HEAD_DIM, (h + 1) * DN_HEAD_DIM)
        zh = z[:, sl]
        parts.append(_rms(o[:, sl], gain) * (zh * jax.nn.sigmoid(zh)))
    y_dn = jnp.concatenate(parts, axis=-1)
    h_ref[...] = (x_ref[...] + _bdot(yc_ref[...], wa_ref[...]) + _bdot(y_dn, wb_ref[...]))


def _mixer_out(x, y_conv, o_dn, proj, dn_norm, w_out, tm):
    t, d = x.shape
    cw = y_conv.shape[1]
    tok = lambda width, col=0: pl.BlockSpec((tm, width), lambda i, col=col: (i, col))
    full = lambda shape: pl.BlockSpec(shape, lambda i: (0, 0))
    return pl.pallas_call(
        _mixer_out_body,
        grid=(t // tm,),
        in_specs=[tok(d), tok(cw), tok(cw), tok(cw, 6), full((1, DN_HEAD_DIM)),
                  full((cw, d)), full((cw, d))],
        out_specs=tok(d),
        out_shape=jax.ShapeDtypeStruct((t, d), F32),
        compiler_params=_params("parallel"),
        name="mixer_out",
    )(x, y_conv, o_dn, proj, dn_norm.reshape(1, DN_HEAD_DIM),
      w_out[:cw].astype(BF16), w_out[cw:].astype(BF16))


def _cross_attn_body(h_ref, g_ref, wq_ref, kv_ref, wo_ref, o_ref):
    h_in = h_ref[...]
    d = h_in.shape[1]
    dh = d // X_HEADS
    q = _bdot(_rms(h_in, g_ref[...]), wq_ref[...])
    outs = []
    for hd in range(X_HEADS):
        qh = q[:, hd * dh:(hd + 1) * dh].astype(BF16)
        kh = kv_ref[:, hd * dh:(hd + 1) * dh]
        vh = kv_ref[:, d + hd * dh:d + (hd + 1) * dh]
        s = lax.dot_general(qh, kh, (((1,), (1,)), ((), ())),
                            preferred_element_type=F32) * (dh ** -0.5)
        s = s - jnp.max(s, axis=-1, keepdims=True)
        e = jnp.exp(s)
        p = e / jnp.sum(e, axis=-1, keepdims=True)
        outs.append(jnp.dot(p.astype(BF16), vh, preferred_element_type=F32))
    o = jnp.concatenate(outs, axis=-1)
    o_ref[...] = h_in + _bdot(o, wo_ref[...])


def _cross_attn(h, gain, w_q, kv, w_o, batch, seq, mem_len, tm):
    t, d = h.shape
    nblk = seq // tm
    tok = pl.BlockSpec((tm, d), lambda b, i: (b * nblk + i, 0))
    full = lambda shape: pl.BlockSpec(shape, lambda b, i: (0, 0))
    return pl.pallas_call(
        _cross_attn_body,
        grid=(batch, nblk),
        in_specs=[tok, full((1, d)), full((d, d)),
                  pl.BlockSpec((mem_len, 2 * d), lambda b, i: (b, 0)), full((d, d))],
        out_specs=tok,
        out_shape=jax.ShapeDtypeStruct((t, d), F32),
        compiler_params=_params("parallel", "parallel"),
        name="cross_attn",
    )(h, gain.reshape(1, d), w_q.astype(BF16), kv.astype(BF16), w_o.astype(BF16))


def _top16_rows(s, payload=None):
    n = s.shape[0]
    row = lax.broadcasted_iota(jnp.int32, s.shape, 0).astype(F32)
    vals, pays = [], []
    for _ in range(PEER_TOPK):
        m = jnp.max(s, axis=0, keepdims=True)
        pos = jnp.min(jnp.where(s == m, row, float(n)), axis=0, keepdims=True)
        hit = row == pos
        vals.append(m)
        if payload is None:
            pays.append(pos)
        else:
            pays.append(jnp.sum(jnp.where(hit, payload, 0.0), axis=0, keepdims=True))
        s = jnp.where(hit, -jnp.inf, s)
    return jnp.concatenate(vals, axis=0), jnp.concatenate(pays, axis=0)


def _pair_candidates(v1, v2, pad):
    sub = lax.broadcasted_iota(jnp.int32, (SUBLANES, v1.shape[1]), 0)
    row = lambda v, a: jnp.broadcast_to(v[a:a + 1], sub.shape)
    lo8 = v2[:SUBLANES]
    blocks = [
        row(v1, 0) + lo8,
        row(v1, 0) + v2[SUBLANES:],
        row(v1, 1) + lo8,
        jnp.where(sub < 5, row(v1, 2) + lo8, pad),
        jnp.where(sub < 7,
                  jnp.where(sub < 4, row(v1, 3), row(v1, 4)) +
                  jnp.where(sub < 4, lo8, pltpu.roll(lo8, 4, 0)), pad),
        jnp.where(sub < 6,
                  jnp.where(sub < 2, row(v1, 5), jnp.where(sub < 4, row(v1, 6), row(v1, 7))) +
                  jnp.where((sub & 1) == 0, row(v2, 0), row(v2, 1)), pad),
        v1[SUBLANES:] + row(v2, 0),
    ]
    return jnp.concatenate(blocks, axis=0)


def _peer_route_body(h_ref, g_ref, wq_ref, keys_ref, xn_ref, gate_ref, *rest):
    idx_refs = rest[:PEER_HEADS]
    sc_ref, sel_ref, gt_ref, it_ref = rest[PEER_HEADS:]
    xn = _rms(h_ref[...], g_ref[...])
    for r in range(SUBLANES):
        xn_ref[:, r] = xn[:, r * LANES:(r + 1) * LANES].reshape(-1, SUBLANES, LANES)
    pq = _bdot(xn, wq_ref[...])
    ngrp = pq.shape[0] // LANES
    for j in range(2 * PEER_HEADS):
        qs = pq[:, j * PEER_HALF:(j + 1) * PEER_HALF].astype(BF16)
        sc = lax.dot_general(keys_ref[j], qs, (((1,), (1,)), ((), ())),
                             preferred_element_type=F32)
        for g in range(ngrp):
            sc_ref[j, g] = sc[:, g * LANES:(g + 1) * LANES]

    def head(h, carry):
        for g in range(ngrp):
            s1, i1 = _top16_rows(sc_ref[2 * h, g])
            s2, i2 = _top16_rows(sc_ref[2 * h + 1, g])
            cand = _pair_candidates(s1, s2, -jnp.inf)
            cand_idx = _pair_candidates(i1 * float(PEER_KEYS), i2, 0.0)
            best, sel = _top16_rows(cand, cand_idx)
            e = jnp.exp(best - jnp.max(best, axis=0, keepdims=True))
            sel_ref[g, h] = sel
            gt_ref[g, h] = e / jnp.sum(e, axis=0, keepdims=True)
        return carry

    lax.fori_loop(0, PEER_HEADS, head, 0)

    nsel = PEER_HEADS * PEER_TOPK
    per_row = LANES // PEER_TOPK
    lane_blk = lax.broadcasted_iota(jnp.int32, (LANES // per_row, LANES), 1) // PEER_TOPK
    for g in range(ngrp):
        rows = slice(g * LANES, (g + 1) * LANES)
        gate_ref[rows, :] = gt_ref[g].reshape(nsel, LANES).T
        it_ref[...] = (sel_ref[g].reshape(nsel, LANES).T * float(PACK_ROWS)).astype(jnp.int32)
        by_k = [it_ref[pl.ds(k, LANES // per_row, stride=per_row), :] for k in range(per_row)]
        out_rows = slice(g * LANES // per_row, (g + 1) * LANES // per_row)
        for hd in range(PEER_HEADS):
            out = jnp.zeros_like(by_k[0])
            for k in range(per_row):
                shift = ((k - hd) * PEER_TOPK) % LANES
                moved = pltpu.roll(by_k[k], shift, 1) if shift else by_k[k]
                out = jnp.where(lane_blk == k, moved, out)
            idx_refs[hd][out_rows, :] = out


def _peer_route(h, gain, w_pq, peer_keys, tm):
    t, d = h.shape
    nk = PEER_HEADS * 2
    ngrp = tm // LANES
    tok = lambda width: pl.BlockSpec((tm, width), lambda i: (i, 0))
    full = lambda shape: pl.BlockSpec(shape, lambda i: (0,) * len(shape))
    idx_rows = tm * PEER_TOPK // LANES
    outs = pl.pallas_call(
        _peer_route_body,
        grid=(t // tm,),
        in_specs=[tok(d), full((1, d)), full((d, d)), full((nk, PEER_KEYS, PEER_HALF))],
        out_specs=[pl.BlockSpec((tm // SUBLANES, SUBLANES, SUBLANES, LANES), lambda i: (i, 0, 0, 0)),
                   tok(LANES)] + [pl.BlockSpec((idx_rows, LANES), lambda i: (i, 0))] * PEER_HEADS,
        out_shape=[jax.ShapeDtypeStruct((t // SUBLANES, SUBLANES, SUBLANES, LANES), F32),
                   jax.ShapeDtypeStruct((t, LANES), F32)] +
                  [jax.ShapeDtypeStruct((t * PEER_TOPK // LANES, LANES), jnp.int32)] * PEER_HEADS,
        scratch_shapes=[pltpu.VMEM((nk, ngrp, PEER_KEYS, LANES), F32),
                        pltpu.VMEM((ngrp, PEER_HEADS, PEER_TOPK, LANES), F32),
                        pltpu.VMEM((ngrp, PEER_HEADS, PEER_TOPK, LANES), F32),
                        pltpu.VMEM((LANES, LANES), jnp.int32)],
        compiler_params=_params("parallel"),
        name="peer_route",
    )(h, gain.reshape(1, d), w_pq.astype(BF16),
      peer_keys.reshape(nk, PEER_KEYS, PEER_HALF).astype(BF16))
    xr, gates = outs[0], outs[1]
    return xr, tuple(o.reshape(-1) for o in outs[2:]), gates


PACK_ROWS = 4


def _pack_table_body(t_ref, o_ref):
    rows, d = t_ref.shape
    half = d // 2
    bits = lambda x: pltpu.bitcast(x.astype(BF16).astype(F32), jnp.uint32)
    word = (bits(t_ref[:, half:]) & jnp.uint32(0xFFFF0000)) | (bits(t_ref[:, :half]) >> 16)
    for j in range(PACK_ROWS):
        o_ref[pl.ds(j, rows, stride=PACK_ROWS), :] = word[:, j * LANES:(j + 1) * LANES]


def _pack_table(tbl):
    e, d = tbl.shape
    rows = _pick_tile(e, 512)
    return pl.pallas_call(
        _pack_table_body,
        grid=(e // rows,),
        in_specs=[pl.BlockSpec((rows, d), lambda i: (i, 0))],
        out_specs=pl.BlockSpec((rows * PACK_ROWS, LANES), lambda i: (i, 0)),
        out_shape=jax.ShapeDtypeStruct((e * PACK_ROWS, LANES), jnp.uint32),
        compiler_params=_params("parallel"),
        name="pack_table",
    )(tbl)


def _unpack(slab):
    lo = pltpu.bitcast(slab << 16, F32)
    hi = pltpu.bitcast(slab & jnp.uint32(0xFFFF0000), F32)
    return lo, hi


STAGE_STEPS = 4
PEER_SEL = PEER_HEADS * PEER_TOPK
IDX_REFS = PEER_HEADS


def _idx_specs(tb):
    n = tb * PEER_SEL // IDX_REFS
    return [pl.BlockSpec((n,), lambda i: (i,), memory_space=pltpu.SMEM) for _ in range(IDX_REFS)]


def _stage_rows(idx_refs, t, tbl_ref, g_ref, step=None):
    ncol = PEER_SEL // IDX_REFS
    per = ncol // STAGE_STEPS
    qs = range(ncol) if step is None else range(step * per, (step + 1) * per)
    for q in qs:
        off = t * ncol + q
        for r in range(IDX_REFS):
            m = r * ncol + q
            row = pl.multiple_of(idx_refs[r][off], PACK_ROWS)
            g_ref[m * PACK_ROWS:(m + 1) * PACK_ROWS, :] = tbl_ref[pl.ds(row, PACK_ROWS), :]


def _bf16_pieces(x, n):
    pieces = []
    for _ in range(n - 1):
        p = x.astype(BF16).astype(F32)
        pieces.append(p)
        x = x - p
    pieces.append(x.astype(BF16).astype(F32))
    return pieces


def _staged_group(idx_ref, tbl_ref, bufs, tb, base, step_fn):
    for k in range(SUBLANES):
        t = base + k
        nxt = t + 1 if k + 1 < SUBLANES else jnp.minimum(t + 1, tb - 1)
        for step in range(STAGE_STEPS):
            _stage_rows(idx_ref, nxt, tbl_ref, bufs[(k + 1) % 2], step)
            step_fn(t, k, step, bufs[k % 2])


def _peer_dot_body(*refs):
    idx_ref = refs[:IDX_REFS]
    x_ref, tbl_ref, act_ref, g0_ref, g1_ref = refs[IDX_REFS:]
    tb, nsel = act_ref.shape
    nslab = SUBLANES * SUBLANES
    lane = lax.broadcasted_iota(jnp.int32, (LANES, LANES), 1)
    col_slab = lane & (nslab - 1)
    _stage_rows(idx_ref, 0, tbl_ref, g0_ref)

    def group(i, carry):
        base = pl.multiple_of(i * SUBLANES, SUBLANES)
        pieces = _bf16_pieces(x_ref[i].reshape(nslab, LANES), LANES // nslab)
        xt = jnp.concatenate(pieces, axis=0).T.astype(BF16)
        zero = jnp.zeros_like(xt)
        acc = [jnp.zeros((nsel, LANES), F32)]

        def step_fn(t, k, j, g_ref):
            lo, hi = _unpack(g_ref[pl.ds(j, nsel, stride=PACK_ROWS), :])
            lhs = jnp.concatenate([lo.astype(BF16), hi.astype(BF16)], axis=1)
            rhs = jnp.concatenate(
                [jnp.where(col_slab == SUBLANES * r + k, xt, zero) for r in (j, PACK_ROWS + j)],
                axis=0)
            acc[0] = acc[0] + jnp.dot(lhs, rhs, preferred_element_type=F32)

        _staged_group(idx_ref, tbl_ref, (g0_ref, g1_ref), tb, base, step_fn)
        out_t = acc[0].T
        act_ref[pl.ds(base, SUBLANES), :] = jnp.sum(
            out_t.reshape(LANES // SUBLANES, SUBLANES, nsel), axis=0)
        return carry

    lax.fori_loop(0, tb // SUBLANES, group, 0)


def _peer_dot(idx_split, xr, tbl, tb):
    t = xr.shape[0] * SUBLANES
    nsel = PEER_SEL
    gbuf = pltpu.VMEM((nsel * PACK_ROWS, LANES), jnp.uint32)
    return pl.pallas_call(
        _peer_dot_body,
        grid=(t // tb,),
        in_specs=_idx_specs(tb) + [
            pl.BlockSpec((tb // SUBLANES, SUBLANES, SUBLANES, LANES), lambda i: (i, 0, 0, 0)),
            pl.BlockSpec(tbl.shape, lambda i: (0, 0), pipeline_mode=pl.Buffered(1))],
        out_specs=pl.BlockSpec((tb, nsel), lambda i: (i, 0)),
        out_shape=jax.ShapeDtypeStruct((t, nsel), F32),
        scratch_shapes=[gbuf, gbuf],
        compiler_params=_params("arbitrary"),
        name="peer_dot",
    )(*idx_split, xr, tbl)


def _peer_weights_body(act_ref, gate_ref, rep_ref, w_ref):
    a = act_ref[...]
    w = gate_ref[...] * (0.5 * a * (1.0 + lax.erf(a * (2.0 ** -0.5))))
    rep = rep_ref[...]
    w_ref[...] = sum(jnp.dot(p.astype(BF16), rep, preferred_element_type=F32)
                     for p in _bf16_pieces(w, 3))


def _peer_weights(act, gates, tm):
    t, n = act.shape
    nrow = 2 * PACK_ROWS * n
    rep = (jnp.arange(n)[:, None] == (jnp.arange(nrow) // (2 * PACK_ROWS))[None, :]).astype(BF16)
    tok = lambda width: pl.BlockSpec((tm, width), lambda i: (i, 0))
    return pl.pallas_call(
        _peer_weights_body,
        grid=(t // tm,),
        in_specs=[tok(n), tok(n), pl.BlockSpec(rep.shape, lambda i: (0, 0))],
        out_specs=tok(nrow),
        out_shape=jax.ShapeDtypeStruct((t, nrow), F32),
        compiler_params=_params("parallel"),
        name="peer_weights",
    )(act, gates, rep)


def _peer_sum_body(has_gain, *refs):
    idx_ref = refs[:IDX_REFS]
    w_ref, h_ref, mask_ref = refs[IDX_REFS:IDX_REFS + 3]
    gain_ref = refs[IDX_REFS + 3] if has_gain else None
    tbl_ref, y_ref, g0_ref, g1_ref = refs[IDX_REFS + 3 + has_gain:]
    tb = y_ref.shape[0]
    mask = mask_ref[...]
    nrow = g0_ref.shape[0] // STAGE_STEPS
    ncol = mask.shape[1] // STAGE_STEPS
    _stage_rows(idx_ref, 0, tbl_ref, g0_ref)

    def finish(i, tiles):
        base = pl.multiple_of(i * SUBLANES, SUBLANES)
        slabs = []
        for r in range(SUBLANES):
            y_r = jnp.concatenate([tile[r:r + 1] for tile in tiles], axis=0)
            slabs.append(h_ref[pl.ds(base, SUBLANES), r * LANES:(r + 1) * LANES] + y_r)
        if gain_ref is not None:
            ms = sum(jnp.sum(s * s, axis=1, keepdims=True) for s in slabs)
            inv = lax.rsqrt(ms * (1.0 / (SUBLANES * LANES)) + EPS)
            slabs = [s * inv * gain_ref[:, r * LANES:(r + 1) * LANES]
                     for r, s in enumerate(slabs)]
        for r, s in enumerate(slabs):
            y_ref[pl.ds(base, SUBLANES), r * LANES:(r + 1) * LANES] = s

    def group(i, prev_rows):
        finish(jnp.maximum(i - 1, 0), prev_rows)
        base = pl.multiple_of(i * SUBLANES, SUBLANES)
        w_rep = w_ref[pl.ds(base, SUBLANES), :]
        acc = [None]

        def step_fn(t, k, step, g_ref):
            cols = slice(step * ncol, (step + 1) * ncol)
            wb = jnp.broadcast_to(w_rep[k:k + 1, cols], (SUBLANES, ncol)) * mask[:, cols]
            a_hi = wb.astype(BF16)
            a_lo = (wb - a_hi.astype(F32)).astype(BF16)
            lhs = jnp.concatenate([a_hi, a_lo], axis=0)
            staged = pltpu.bitcast(g_ref[step * nrow:(step + 1) * nrow, :], BF16)
            part = jnp.dot(lhs, staged, preferred_element_type=F32)
            acc[0] = part if step == 0 else acc[0] + part
            if step == STAGE_STEPS - 1:
                rows.append(acc[0][:SUBLANES] + acc[0][SUBLANES:])

        rows = []
        _staged_group(idx_ref, tbl_ref, (g0_ref, g1_ref), tb, base, step_fn)
        return tuple(rows)

    ngroup = tb // SUBLANES
    zeros = tuple(jnp.zeros((SUBLANES, LANES), F32) for _ in range(SUBLANES))
    finish(ngroup - 1, lax.fori_loop(0, ngroup, group, zeros))


def _peer_sum(idx_split, w, h, gain, tbl, tb):
    t, nrow = w.shape
    d = h.shape[1]
    nsel = PEER_SEL
    tile = pl.BlockSpec((tb, d), lambda i: (i, 0))
    extra = [] if gain is None else [gain.reshape(1, d)]
    c = jnp.arange(nrow)
    out_row = (c % 2) * PACK_ROWS + (c % (2 * PACK_ROWS)) // 2
    mask = (jnp.arange(SUBLANES)[:, None] == out_row[None, :]).astype(F32)
    gbuf = pltpu.VMEM((nsel * PACK_ROWS, LANES), jnp.uint32)
    const = lambda shape: pl.BlockSpec(shape, lambda i: (0, 0))
    return pl.pallas_call(
        functools.partial(_peer_sum_body, len(extra)),
        grid=(t // tb,),
        in_specs=_idx_specs(tb) + [
            pl.BlockSpec((tb, nrow), lambda i: (i, 0)),
            tile,
            const(mask.shape)] + [const((1, d))] * len(extra) + [
            pl.BlockSpec(tbl.shape, lambda i: (0, 0), pipeline_mode=pl.Buffered(1))],
        out_specs=tile,
        out_shape=jax.ShapeDtypeStruct((t, d), F32),
        scratch_shapes=[gbuf, gbuf],
        compiler_params=_params("arbitrary"),
        name="peer_sum",
    )(*idx_split, w, h, mask, *extra, tbl)


def _pick_tile(n, pref):
    tile = min(n, pref)
    while n % tile:
        tile //= 2
    return tile


def _layer(h, mem, batch, seq, norm_mix, w_in, conv_w, dn_conv_w, dn_a_log, dn_dt_bias, dn_norm,
           w_out, norm_x, norm_mem, w_xq, w_xkv, w_xo, norm_ffn, w_pq, peer_keys,
           expert_u, expert_v, final_gain):
    t, d = h.shape
    mem_len = mem.shape[0] // batch
    in_cols = w_in.shape[1]
    pad = (-in_cols) % LANES
    w_in_p = jnp.pad(w_in, ((0, 0), (0, pad))).astype(BF16)

    proj = _norm_matmul(h, norm_mix, w_in_p, _pick_tile(t, 256), "in_proj")
    y_conv, q, k, v, gb = _mixer_pre(proj, conv_w, dn_conv_w, dn_a_log, dn_dt_bias,
                                     batch, seq, _pick_tile(seq, 256))
    o_dn = _delta_rule(q, k, v, gb, batch, seq)
    h = _mixer_out(h, y_conv, o_dn, proj, dn_norm, w_out, _pick_tile(t, 512))

    kv = _norm_matmul(mem, norm_mem, w_xkv.astype(BF16), _pick_tile(mem.shape[0], 256), "kv_proj")
    h = _cross_attn(h, norm_x, w_xq, kv, w_xo, batch, seq, mem_len, _pick_tile(seq, 512))

    xr, idx_split, gates = _peer_route(h, norm_ffn, w_pq, peer_keys, _pick_tile(t, 512))
    tb = _pick_tile(t, 128)
    act = _peer_dot(idx_split, xr, _pack_table(expert_u), tb)
    w = _peer_weights(act, gates, _pick_tile(t, 1024))
    return _peer_sum(idx_split, w, h, final_gain, _pack_table(expert_v), tb)


def kernel(x, mem, norm_mix, w_in, conv_w, dn_conv_w, dn_a_log, dn_dt_bias, dn_norm, w_out,
           norm_x, norm_mem, w_xq, w_xkv, w_xo, norm_ffn, w_pq, peer_keys, expert_u, expert_v,
           norm_final):
    batch, seq, d = x.shape
    depth = norm_mix.shape[0]
    h = x.reshape(batch * seq, d)
    mem2 = mem.reshape(batch * mem.shape[1], d)
    for layer in range(depth):
        h = _layer(h, mem2, batch, seq, norm_mix[layer], w_in[layer], conv_w[layer],
                   dn_conv_w[layer], dn_a_log[layer], dn_dt_bias[layer], dn_norm[layer],
                   w_out[layer], norm_x[layer], norm_mem[layer], w_xq[layer], w_xkv[layer],
                   w_xo[layer], norm_ffn[layer], w_pq[layer], peer_keys[layer],
                   expert_u[layer], expert_v[layer],
                   norm_final if layer == depth - 1 else None)
    return h.reshape(batch, seq, d)
```

```python
import functools

import jax
import jax.numpy as jnp
from jax import lax
from jax.experimental import pallas as pl
from jax.experimental.pallas import tpu as pltpu

EPS = 1e-6
F32 = jnp.float32
BF16 = jnp.bfloat16
HIGHEST = lax.Precision.HIGHEST

LANES = 128
SUBLANES = 8
VMEM_LIMIT = 56 * 1024 * 1024

CONV_GROUP_K = 3
DN_CONV_K = 4
DN_HEADS = 4
DN_HEAD_DIM = 128
DN_CHUNK = 64
X_HEADS = 4
PEER_HEADS = 8
PEER_KEYS = 128
PEER_TOPK = 16
PEER_HALF = 64


def _params(*sem):
    return pltpu.CompilerParams(dimension_semantics=sem, vmem_limit_bytes=VMEM_LIMIT)


def _rms(x, gain):
    ms = jnp.mean(x * x, axis=-1, keepdims=True)
    return x * lax.rsqrt(ms + EPS) * gain


def _bdot(a, b):
    return jnp.dot(a.astype(BF16), b.astype(BF16), preferred_element_type=F32)


def _hdot(a, b):
    return jnp.dot(a, b, preferred_element_type=F32, precision=HIGHEST)


def _norm_matmul_body(x_ref, g_ref, w_ref, o_ref):
    xn = _rms(x_ref[...], g_ref[...])
    o_ref[...] = jnp.dot(xn.astype(BF16), w_ref[...], preferred_element_type=F32)


def _norm_matmul(x, gain, w, tm, name):
    t, d = x.shape
    n = w.shape[1]
    return pl.pallas_call(
        _norm_matmul_body,
        grid=(t // tm,),
        in_specs=[pl.BlockSpec((tm, d), lambda i: (i, 0)),
                  pl.BlockSpec((1, d), lambda i: (0, 0)),
                  pl.BlockSpec((d, n), lambda i: (0, 0))],
        out_specs=pl.BlockSpec((tm, n), lambda i: (i, 0)),
        out_shape=jax.ShapeDtypeStruct((t, n), F32),
        compiler_params=_params("parallel"),
        name=name,
    )(x, gain.reshape(1, d), w)


def _shift_rows(x, halo, k):
    rolled = pltpu.roll(x, k, 0)
    hal = pltpu.roll(halo, k, 0)
    row = lax.broadcasted_iota(jnp.int32, (SUBLANES, x.shape[1]), 0)
    head = jnp.where(row < k, hal, rolled[:SUBLANES])
    return jnp.concatenate([head, rolled[SUBLANES:]], axis=0)


def _causal_conv(x, halo, w_ref):
    kk = w_ref.shape[0]
    y = x * w_ref[kk - 1:kk, :]
    for j in range(1, kk):
        y = y + _shift_rows(x, halo, j) * w_ref[kk - 1 - j:kk - j, :]
    return y


def _l2norm_heads(x):
    outs = []
    for h in range(DN_HEADS):
        xh = x[:, h * DN_HEAD_DIM:(h + 1) * DN_HEAD_DIM]
        outs.append(xh * lax.rsqrt(jnp.sum(xh * xh, axis=-1, keepdims=True) + EPS))
    return jnp.concatenate(outs, axis=-1)


def _mixer_pre_body(b_ref, c_ref, h_ref, q_ref, k_ref, v_ref, t_ref,
                    ch_ref, hh_ref, qh_ref, kh_ref, vh_ref,
                    cw_ref, qw_ref, kw_ref, vw_ref, alog_ref, dtb_ref, tril_ref,
                    yc_ref, qo_ref, ko_ref, vo_ref, gb_ref):
    first = pl.program_id(1) == 0
    keep = jnp.where(first, 0.0, 1.0).astype(F32)

    u = c_ref[...] * h_ref[...]
    uh = ch_ref[...] * hh_ref[...] * keep
    yc_ref[...] = b_ref[...] * _causal_conv(u, uh, cw_ref)

    def dn_branch(x_ref, xh_ref, w_ref):
        y = _causal_conv(x_ref[...], xh_ref[...] * keep, w_ref)
        return y * jax.nn.sigmoid(y)

    qo_ref[...] = _l2norm_heads(dn_branch(q_ref, qh_ref, qw_ref))
    ko_ref[...] = _l2norm_heads(dn_branch(k_ref, kh_ref, kw_ref))
    vo_ref[...] = dn_branch(v_ref, vh_ref, vw_ref)

    tail = t_ref[...]
    beta = jax.nn.sigmoid(tail)
    g = -jnp.exp(alog_ref[...]) * jax.nn.softplus(tail + dtb_ref[...])
    gcum = _hdot(tril_ref[...], g)
    lane = lax.broadcasted_iota(jnp.int32, tail.shape, 1)
    gb_ref[...] = jnp.where(lane < DN_HEADS, beta, gcum)


def _mixer_pre(proj, conv_w, dn_conv_w, dn_a_log, dn_dt_bias, batch, seq, ts):
    t = proj.shape[0]
    cw = 512
    nblk = seq // ts
    hb = ts // SUBLANES

    def cur(col):
        return pl.BlockSpec((ts, cw if col < 7 else LANES),
                            lambda b, i, col=col: (b * nblk + i, col))

    def halo(col):
        return pl.BlockSpec(
            (SUBLANES, cw),
            lambda b, i, col=col: (jnp.maximum((b * nblk + i) * hb - 1, 0), col))

    def full(shape):
        return pl.BlockSpec(shape, lambda b, i: (0,) * len(shape))

    qw, kw, vw = (dn_conv_w[:, j * cw:(j + 1) * cw] for j in range(3))
    lane_pad = jnp.zeros((LANES - 2 * DN_HEADS,), F32)
    alog = jnp.concatenate([jnp.zeros((DN_HEADS,), F32), dn_a_log, lane_pad]).reshape(1, LANES)
    dtb = jnp.concatenate([jnp.zeros((DN_HEADS,), F32), dn_dt_bias, lane_pad]).reshape(1, LANES)
    r = jnp.arange(ts)
    tril = ((r[:, None] >= r[None, :]) &
            (r[:, None] // DN_CHUNK == r[None, :] // DN_CHUNK)).astype(F32)

    tail_spec = pl.BlockSpec((ts, LANES), lambda b, i: (b * nblk + i, 7 * cw // LANES))
    out_tok = lambda w: pl.BlockSpec((ts, w), lambda b, i: (b * nblk + i, 0))
    return pl.pallas_call(
        _mixer_pre_body,
        grid=(batch, nblk),
        in_specs=[cur(0), cur(1), cur(2), cur(3), cur(4), cur(5), tail_spec,
                  halo(1), halo(2), halo(3), halo(4), halo(5),
                  full((CONV_GROUP_K, cw)), full((DN_CONV_K, cw)), full((DN_CONV_K, cw)),
                  full((DN_CONV_K, cw)), full((1, LANES)), full((1, LANES)), full((ts, ts))],
        out_specs=[out_tok(cw), out_tok(cw), out_tok(cw), out_tok(cw), out_tok(LANES)],
        out_shape=[jax.ShapeDtypeStruct((t, cw), F32)] * 4 + [jax.ShapeDtypeStruct((t, LANES), F32)],
        compiler_params=_params("parallel", "parallel"),
        name="mixer_pre",
    )(proj, proj, proj, proj, proj, proj, proj, proj, proj, proj, proj, proj,
      conv_w, qw, kw, vw, alog, dtb, tril)


_NN = (((1,), (0,)), ((), ()))
_NT = (((1,), (1,)), ((), ()))
_TN = (((0,), (0,)), ((), ()))
DN_LOCAL_PASSES = 1
DN_SCAN_PASSES = 1


def _mm(a, b, dims, passes):
    dot = lambda x, y: lax.dot_general(x, y, dims, preferred_element_type=F32)
    if passes == 6:
        return lax.dot_general(a, b, dims, preferred_element_type=F32, precision=HIGHEST)
    a_hi = a.astype(BF16)
    b_hi = b.astype(BF16)
    if passes == 1:
        return dot(a_hi, b_hi)
    a_lo = (a - a_hi.astype(F32)).astype(BF16)
    b_lo = (b - b_hi.astype(F32)).astype(BF16)
    return dot(a_hi, b_hi) + (dot(a_hi, b_lo) + dot(a_lo, b_hi))


DN_LOCAL_CHUNKS = 4


def _dn_local_body(q_ref, k_ref, v_ref, gb_ref, u_ref, w_ref, qd_ref, kd_ref, in_ref, gl_ref):
    c = DN_CHUNK
    n = DN_HEADS * c
    p = DN_LOCAL_PASSES
    cis = range(q_ref.shape[0] // c)
    each = lambda f, *xs: [f(*a) for a in zip(*xs)]
    lanes = lambda x: jnp.broadcast_to(x, (x.shape[0], LANES))

    def stack(ref, ci):
        return jnp.concatenate([ref[ci * c:(ci + 1) * c, h * DN_HEAD_DIM:(h + 1) * DN_HEAD_DIM]
                                for h in range(DN_HEADS)], axis=0)

    def col(gb, j):
        return jnp.concatenate([lanes(gb[:, j + h:j + h + 1]) for h in range(DN_HEADS)], axis=0)

    gb = [gb_ref[ci * c:(ci + 1) * c, :] for ci in cis]
    q = [stack(q_ref, ci) * (DN_HEAD_DIM ** -0.5) for ci in cis]
    k = [stack(k_ref, ci) for ci in cis]
    v = [stack(v_ref, ci) for ci in cis]
    beta = [col(x, 0) for x in gb]
    g = [col(x, DN_HEADS) for x in gb]
    g_last = [jnp.concatenate(
        [jnp.broadcast_to(x[c - 1:c, DN_HEADS + h:DN_HEADS + h + 1], (c, LANES))
         for h in range(DN_HEADS)], axis=0) for x in gb]

    row = lax.broadcasted_iota(jnp.int32, (n, n), 0)
    cl = lax.broadcasted_iota(jnp.int32, (n, n), 1)
    shift = c.bit_length() - 1
    same_head = (row >> shift) == (cl >> shift)
    causal = same_head & (row >= cl)
    strict = same_head & (row > cl)
    eye = (row == cl).astype(F32)
    decay = [jnp.where(causal, jnp.exp(jnp.where(causal, x[:, 0:1] - x.T[0:1, :], 0.0)), 0.0)
             for x in g]
    k_beta = each(lambda a, b: a * b, k, beta)
    v_beta = each(lambda a, b: a * b, v, beta)
    a = each(lambda kb, kk, d: jnp.where(strict, _mm(kb, kk, _NT, p) * d, 0.0), k_beta, k, decay)

    t_mat = [eye - x for x in a]
    pw = [_mm(x, x, _NN, p) for x in a]
    for _ in range(4):
        t_mat = each(lambda t, w_: t + _mm(t, w_, _NN, p), t_mat, pw)
        pw = [_mm(x, x, _NN, p) for x in pw]
    t_mat = each(lambda t, w_: t + _mm(t, w_, _NN, p), t_mat, pw)

    e_g = [jnp.exp(x) for x in g]
    uw = each(lambda t, vb, kb, e: _mm(t, jnp.concatenate([vb, kb * e], axis=1), _NN, p),
              t_mat, v_beta, k_beta, e_g)
    intra = each(lambda qq, kk, d: _mm(qq, kk, _NT, p) * d, q, k, decay)
    for ci in cis:
        u_ref[ci] = uw[ci][:, :DN_HEAD_DIM]
        w_ref[ci] = uw[ci][:, DN_HEAD_DIM:]
        in_ref[ci] = intra[ci]
        qd_ref[ci] = q[ci] * e_g[ci]
        kd_ref[ci] = k[ci] * jnp.exp(g_last[ci] - g[ci])
        gl_ref[ci] = jnp.concatenate(
            [jnp.exp(g_last[ci][h * c:h * c + 1]) for h in range(DN_HEADS)] +
            [jnp.zeros((SUBLANES - DN_HEADS, LANES), F32)], axis=0)


def _dn_scan_body(u_ref, w_ref, qd_ref, kd_ref, in_ref, gl_ref, o_ref, state_ref):
    c = DN_CHUNK
    p = DN_SCAN_PASSES

    @pl.when(pl.program_id(0) == 0)
    def _():
        state_ref[...] = jnp.zeros_like(state_ref)

    for b in range(u_ref.shape[0]):
        u = u_ref[b, 0]
        w = w_ref[b, 0]
        qd = qd_ref[b, 0]
        kd = kd_ref[b, 0]
        gl = gl_ref[b, 0]
        v_new, q_state = [], []
        for h in range(DN_HEADS):
            rows = slice(h * c, (h + 1) * c)
            ws = _mm(jnp.concatenate([w[rows], qd[rows]], axis=0), state_ref[b, h], _NN, p)
            v_new.append(u[rows] - ws[:c])
            q_state.append(ws[c:])
        o = jnp.concatenate(q_state, axis=0) + _mm(in_ref[b, 0], jnp.concatenate(v_new, axis=0),
                                                   _NN, p)
        for h in range(DN_HEADS):
            rows = slice(h * c, (h + 1) * c)
            state_ref[b, h] = (state_ref[b, h] * gl[h:h + 1] +
                               _mm(kd[rows], v_new[h], _TN, p))
        o_ref[b] = jnp.concatenate([o[h * c:(h + 1) * c] for h in range(DN_HEADS)], axis=1)


def _delta_rule(q, k, v, gb, batch, seq):
    t, w = q.shape
    nc = seq // DN_CHUNK
    n = DN_HEADS * DN_CHUNK
    cps = _pick_tile(batch * nc, DN_LOCAL_CHUNKS)
    tok = lambda width: pl.BlockSpec((cps * DN_CHUNK, width), lambda i: (i, 0))
    per_chunk = lambda rows, width: pl.BlockSpec((cps, rows, width), lambda i: (i, 0, 0))
    f = lambda rows, width: jax.ShapeDtypeStruct((batch * nc, rows, width), F32)
    u, wm, qd, kd, intra, gl = pl.pallas_call(
        _dn_local_body,
        grid=(batch * nc // cps,),
        in_specs=[tok(w), tok(w), tok(w), tok(LANES)],
        out_specs=[per_chunk(n, DN_HEAD_DIM)] * 4 + [per_chunk(n, n), per_chunk(SUBLANES, LANES)],
        out_shape=[f(n, DN_HEAD_DIM)] * 4 + [f(n, n), f(SUBLANES, LANES)],
        compiler_params=_params("parallel"),
        name="dn_local",
    )(q, k, v, gb)

    seq_blk = lambda rows, width: pl.BlockSpec((batch, 1, rows, width), lambda i: (0, i, 0, 0))
    by_batch = lambda x: x.reshape(batch, nc, *x.shape[1:])
    o = pl.pallas_call(
        _dn_scan_body,
        grid=(nc,),
        in_specs=[seq_blk(n, DN_HEAD_DIM)] * 4 + [seq_blk(n, n), seq_blk(SUBLANES, LANES)],
        out_specs=pl.BlockSpec((batch, DN_CHUNK, w), lambda i: (0, i, 0)),
        out_shape=jax.ShapeDtypeStruct((batch, seq, w), F32),
        scratch_shapes=[pltpu.VMEM((batch, DN_HEADS, DN_HEAD_DIM, DN_HEAD_DIM), F32)],
        compiler_params=_params("arbitrary"),
        name="dn_scan",
    )(by_batch(u), by_batch(wm), by_batch(qd), by_batch(kd), by_batch(intra), by_batch(gl))
    return o.reshape(t, w)


def _mixer_out_body(x_ref, yc_ref, o_ref, z_ref, dng_ref, wa_ref, wb_ref, h_ref):
    o = o_ref[...]
    z = z_ref[...]
    gain = dng_ref[...]
    parts = []
    for h in range(DN_HEADS):
        sl = slice(h * DN_HEAD_DIM, (h + 1) * DN_HEAD_DIM)
        zh = z[:, sl]
        parts.append(_rms(o[:, sl], gain) * (zh * jax.nn.sigmoid(zh)))
    y_dn = jnp.concatenate(parts, axis=-1)
    h_ref[...] = (x_ref[...] + _bdot(yc_ref[...], wa_ref[...]) + _bdot(y_dn, wb_ref[...]))


def _mixer_out(x, y_conv, o_dn, proj, dn_norm, w_out, tm):
    t, d = x.shape
    cw = y_conv.shape[1]
    tok = lambda width, col=0: pl.BlockSpec((tm, width), lambda i, col=col: (i, col))
    full = lambda shape: pl.BlockSpec(shape, lambda i: (0, 0))
    return pl.pallas_call(
        _mixer_out_body,
        grid=(t // tm,),
        in_specs=[tok(d), tok(cw), tok(cw), tok(cw, 6), full((1, DN_HEAD_DIM)),
                  full((cw, d)), full((cw, d))],
        out_specs=tok(d),
        out_shape=jax.ShapeDtypeStruct((t, d), F32),
        compiler_params=_params("parallel"),
        name="mixer_out",
    )(x, y_conv, o_dn, proj, dn_norm.reshape(1, DN_HEAD_DIM),
      w_out[:cw].astype(BF16), w_out[cw:].astype(BF16))


def _cross_attn_body(h_ref, g_ref, wq_ref, kv_ref, wo_ref, o_ref):
    h_in = h_ref[...]
    d = h_in.shape[1]
    dh = d // X_HEADS
    q = _bdot(_rms(h_in, g_ref[...]), wq_ref[...])
    outs = []
    for hd in range(X_HEADS):
        qh = q[:, hd * dh:(hd + 1) * dh].astype(BF16)
        kh = kv_ref[:, hd * dh:(hd + 1) * dh]
        vh = kv_ref[:, d + hd * dh:d + (hd + 1) * dh]
        s = lax.dot_general(qh, kh, (((1,), (1,)), ((), ())),
                            preferred_element_type=F32) * (dh ** -0.5)
        s = s - jnp.max(s, axis=-1, keepdims=True)
        e = jnp.exp(s)
        p = e / jnp.sum(e, axis=-1, keepdims=True)
        outs.append(jnp.dot(p.astype(BF16), vh, preferred_element_type=F32))
    o = jnp.concatenate(outs, axis=-1)
    o_ref[...] = h_in + _bdot(o, wo_ref[...])


def _cross_attn(h, gain, w_q, kv, w_o, batch, seq, mem_len, tm):
    t, d = h.shape
    nblk = seq // tm
    tok = pl.BlockSpec((tm, d), lambda b, i: (b * nblk + i, 0))
    full = lambda shape: pl.BlockSpec(shape, lambda b, i: (0, 0))
    return pl.pallas_call(
        _cross_attn_body,
        grid=(batch, nblk),
        in_specs=[tok, full((1, d)), full((d, d)),
                  pl.BlockSpec((mem_len, 2 * d), lambda b, i: (b, 0)), full((d, d))],
        out_specs=tok,
        out_shape=jax.ShapeDtypeStruct((t, d), F32),
        compiler_params=_params("parallel", "parallel"),
        name="cross_attn",
    )(h, gain.reshape(1, d), w_q.astype(BF16), kv.astype(BF16), w_o.astype(BF16))


def _top16_rows(s, payload=None):
    n = s.shape[0]
    row = lax.broadcasted_iota(jnp.int32, s.shape, 0).astype(F32)
    vals, pays = [], []
    for _ in range(PEER_TOPK):
        m = jnp.max(s, axis=0, keepdims=True)
        pos = jnp.min(jnp.where(s == m, row, float(n)), axis=0, keepdims=True)
        hit = row == pos
        vals.append(m)
        if payload is None:
            pays.append(pos)
        else:
            pays.append(jnp.sum(jnp.where(hit, payload, 0.0), axis=0, keepdims=True))
        s = jnp.where(hit, -jnp.inf, s)
    return jnp.concatenate(vals, axis=0), jnp.concatenate(pays, axis=0)


def _pair_candidates(v1, v2, pad):
    sub = lax.broadcasted_iota(jnp.int32, (SUBLANES, v1.shape[1]), 0)
    row = lambda v, a: jnp.broadcast_to(v[a:a + 1], sub.shape)
    lo8 = v2[:SUBLANES]
    blocks = [
        row(v1, 0) + lo8,
        row(v1, 0) + v2[SUBLANES:],
        row(v1, 1) + lo8,
        jnp.where(sub < 5, row(v1, 2) + lo8, pad),
        jnp.where(sub < 7,
                  jnp.where(sub < 4, row(v1, 3), row(v1, 4)) +
                  jnp.where(sub < 4, lo8, pltpu.roll(lo8, 4, 0)), pad),
        jnp.where(sub < 6,
                  jnp.where(sub < 2, row(v1, 5), jnp.where(sub < 4, row(v1, 6), row(v1, 7))) +
                  jnp.where((sub & 1) == 0, row(v2, 0), row(v2, 1)), pad),
        v1[SUBLANES:] + row(v2, 0),
    ]
    return jnp.concatenate(blocks, axis=0)


def _peer_route_body(h_ref, g_ref, wq_ref, keys_ref, xn_ref, gate_ref, *rest):
    idx_refs = rest[:PEER_HEADS]
    sc_ref, sel_ref, gt_ref, it_ref = rest[PEER_HEADS:]
    xn = _rms(h_ref[...], g_ref[...])
    for r in range(SUBLANES):
        xn_ref[:, r] = xn[:, r * LANES:(r + 1) * LANES].reshape(-1, SUBLANES, LANES)
    pq = _bdot(xn, wq_ref[...])
    ngrp = pq.shape[0] // LANES
    for j in range(2 * PEER_HEADS):
        qs = pq[:, j * PEER_HALF:(j + 1) * PEER_HALF].astype(BF16)
        sc = lax.dot_general(keys_ref[j], qs, (((1,), (1,)), ((), ())),
                             preferred_element_type=F32)
        for g in range(ngrp):
            sc_ref[j, g] = sc[:, g * LANES:(g + 1) * LANES]

    def head(h, carry):
        for g in range(ngrp):
            s1, i1 = _top16_rows(sc_ref[2 * h, g])
            s2, i2 = _top16_rows(sc_ref[2 * h + 1, g])
            cand = _pair_candidates(s1, s2, -jnp.inf)
            cand_idx = _pair_candidates(i1 * float(PEER_KEYS), i2, 0.0)
            best, sel = _top16_rows(cand, cand_idx)
            e = jnp.exp(best - jnp.max(best, axis=0, keepdims=True))
            sel_ref[g, h] = sel
            gt_ref[g, h] = e / jnp.sum(e, axis=0, keepdims=True)
        return carry

    lax.fori_loop(0, PEER_HEADS, head, 0)

    nsel = PEER_HEADS * PEER_TOPK
    per_row = LANES // PEER_TOPK
    lane_blk = lax.broadcasted_iota(jnp.int32, (LANES // per_row, LANES), 1) // PEER_TOPK
    for g in range(ngrp):
        rows = slice(g * LANES, (g + 1) * LANES)
        gate_ref[rows, :] = gt_ref[g].reshape(nsel, LANES).T
        it_ref[...] = (sel_ref[g].reshape(nsel, LANES).T * float(PACK_ROWS)).astype(jnp.int32)
        by_k = [it_ref[pl.ds(k, LANES // per_row, stride=per_row), :] for k in range(per_row)]
        out_rows = slice(g * LANES // per_row, (g + 1) * LANES // per_row)
        for hd in range(PEER_HEADS):
            out = jnp.zeros_like(by_k[0])
            for k in range(per_row):
                shift = ((k - hd) * PEER_TOPK) % LANES
                moved = pltpu.roll(by_k[k], shift, 1) if shift else by_k[k]
                out = jnp.where(lane_blk == k, moved, out)
            idx_refs[hd][out_rows, :] = out


def _peer_route(h, gain, w_pq, peer_keys, tm):
    t, d = h.shape
    nk = PEER_HEADS * 2
    ngrp = tm // LANES
    tok = lambda width: pl.BlockSpec((tm, width), lambda i: (i, 0))
    full = lambda shape: pl.BlockSpec(shape, lambda i: (0,) * len(shape))
    idx_rows = tm * PEER_TOPK // LANES
    outs = pl.pallas_call(
        _peer_route_body,
        grid=(t // tm,),
        in_specs=[tok(d), full((1, d)), full((d, d)), full((nk, PEER_KEYS, PEER_HALF))],
        out_specs=[pl.BlockSpec((tm // SUBLANES, SUBLANES, SUBLANES, LANES), lambda i: (i, 0, 0, 0)),
                   tok(LANES)] + [pl.BlockSpec((idx_rows, LANES), lambda i: (i, 0))] * PEER_HEADS,
        out_shape=[jax.ShapeDtypeStruct((t // SUBLANES, SUBLANES, SUBLANES, LANES), F32),
                   jax.ShapeDtypeStruct((t, LANES), F32)] +
                  [jax.ShapeDtypeStruct((t * PEER_TOPK // LANES, LANES), jnp.int32)] * PEER_HEADS,
        scratch_shapes=[pltpu.VMEM((nk, ngrp, PEER_KEYS, LANES), F32),
                        pltpu.VMEM((ngrp, PEER_HEADS, PEER_TOPK, LANES), F32),
                        pltpu.VMEM((ngrp, PEER_HEADS, PEER_TOPK, LANES), F32),
                        pltpu.VMEM((LANES, LANES), jnp.int32)],
        compiler_params=_params("parallel"),
        name="peer_route",
    )(h, gain.reshape(1, d), w_pq.astype(BF16),
      peer_keys.reshape(nk, PEER_KEYS, PEER_HALF).astype(BF16))
    xr, gates = outs[0], outs[1]
    return xr, tuple(o.reshape(-1) for o in outs[2:]), gates


PACK_ROWS = 4


def _pack_table_body(t_ref, o_ref):
    rows, d = t_ref.shape
    half = d // 2
    bits = lambda x: pltpu.bitcast(x.astype(BF16).astype(F32), jnp.uint32)
    word = (bits(t_ref[:, half:]) & jnp.uint32(0xFFFF0000)) | (bits(t_ref[:, :half]) >> 16)
    for j in range(PACK_ROWS):
        o_ref[pl.ds(j, rows, stride=PACK_ROWS), :] = word[:, j * LANES:(j + 1) * LANES]


def _pack_table(tbl):
    e, d = tbl.shape
    rows = _pick_tile(e, 512)
    return pl.pallas_call(
        _pack_table_body,
        grid=(e // rows,),
        in_specs=[pl.BlockSpec((rows, d), lambda i: (i, 0))],
        out_specs=pl.BlockSpec((rows * PACK_ROWS, LANES), lambda i: (i, 0)),
        out_shape=jax.ShapeDtypeStruct((e * PACK_ROWS, LANES), jnp.uint32),
        compiler_params=_params("parallel"),
        name="pack_table",
    )(tbl)


def _unpack(slab):
    lo = pltpu.bitcast(slab << 16, F32)
    hi = pltpu.bitcast(slab & jnp.uint32(0xFFFF0000), F32)
    return lo, hi


STAGE_STEPS = 4
PEER_SEL = PEER_HEADS * PEER_TOPK
IDX_REFS = PEER_HEADS


def _idx_specs(tb):
    n = tb * PEER_SEL // IDX_REFS
    return [pl.BlockSpec((n,), lambda i: (i,), memory_space=pltpu.SMEM) for _ in range(IDX_REFS)]


def _stage_rows(idx_refs, t, tbl_ref, g_ref, step=None):
    ncol = PEER_SEL // IDX_REFS
    per = ncol // STAGE_STEPS
    qs = range(ncol) if step is None else range(step * per, (step + 1) * per)
    for q in qs:
        off = t * ncol + q
        for r in range(IDX_REFS):
            m = r * ncol + q
            row = pl.multiple_of(idx_refs[r][off], PACK_ROWS)
            g_ref[m * PACK_ROWS:(m + 1) * PACK_ROWS, :] = tbl_ref[pl.ds(row, PACK_ROWS), :]


def _bf16_pieces(x, n):
    pieces = []
    for _ in range(n - 1):
        p = x.astype(BF16).astype(F32)
        pieces.append(p)
        x = x - p
    pieces.append(x.astype(BF16).astype(F32))
    return pieces


def _staged_group(idx_ref, tbl_ref, bufs, tb, base, step_fn):
    for k in range(SUBLANES):
        t = base + k
        nxt = t + 1 if k + 1 < SUBLANES else jnp.minimum(t + 1, tb - 1)
        for step in range(STAGE_STEPS):
            _stage_rows(idx_ref, nxt, tbl_ref, bufs[(k + 1) % 2], step)
            step_fn(t, k, step, bufs[k % 2])


def _peer_dot_body(*refs):
    idx_ref = refs[:IDX_REFS]
    x_ref, tbl_ref, act_ref, g0_ref, g1_ref = refs[IDX_REFS:]
    tb, nsel = act_ref.shape
    nslab = SUBLANES * SUBLANES
    lane = lax.broadcasted_iota(jnp.int32, (LANES, LANES), 1)
    col_slab = lane & (nslab - 1)
    _stage_rows(idx_ref, 0, tbl_ref, g0_ref)

    def group(i, carry):
        base = pl.multiple_of(i * SUBLANES, SUBLANES)
        pieces = _bf16_pieces(x_ref[i].reshape(nslab, LANES), LANES // nslab)
        xt = jnp.concatenate(pieces, axis=0).T.astype(BF16)
        zero = jnp.zeros_like(xt)
        acc = [jnp.zeros((nsel, LANES), F32)]

        def step_fn(t, k, j, g_ref):
            lo, hi = _unpack(g_ref[pl.ds(j, nsel, stride=PACK_ROWS), :])
            lhs = jnp.concatenate([lo.astype(BF16), hi.astype(BF16)], axis=1)
            rhs = jnp.concatenate(
                [jnp.where(col_slab == SUBLANES * r + k, xt, zero) for r in (j, PACK_ROWS + j)],
                axis=0)
            acc[0] = acc[0] + jnp.dot(lhs, rhs, preferred_element_type=F32)

        _staged_group(idx_ref, tbl_ref, (g0_ref, g1_ref), tb, base, step_fn)
        out_t = acc[0].T
        act_ref[pl.ds(base, SUBLANES), :] = jnp.sum(
            out_t.reshape(LANES // SUBLANES, SUBLANES, nsel), axis=0)
        return carry

    lax.fori_loop(0, tb // SUBLANES, group, 0)


def _peer_dot(idx_split, xr, tbl, tb):
    t = xr.shape[0] * SUBLANES
    nsel = PEER_SEL
    gbuf = pltpu.VMEM((nsel * PACK_ROWS, LANES), jnp.uint32)
    return pl.pallas_call(
        _peer_dot_body,
        grid=(t // tb,),
        in_specs=_idx_specs(tb) + [
            pl.BlockSpec((tb // SUBLANES, SUBLANES, SUBLANES, LANES), lambda i: (i, 0, 0, 0)),
            pl.BlockSpec(tbl.shape, lambda i: (0, 0), pipeline_mode=pl.Buffered(1))],
        out_specs=pl.BlockSpec((tb, nsel), lambda i: (i, 0)),
        out_shape=jax.ShapeDtypeStruct((t, nsel), F32),
        scratch_shapes=[gbuf, gbuf],
        compiler_params=_params("arbitrary"),
        name="peer_dot",
    )(*idx_split, xr, tbl)


def _peer_weights_body(act_ref, gate_ref, rep_ref, w_ref):
    a = act_ref[...]
    w = gate_ref[...] * (0.5 * a * (1.0 + lax.erf(a * (2.0 ** -0.5))))
    rep = rep_ref[...]
    w_ref[...] = sum(jnp.dot(p.astype(BF16), rep, preferred_element_type=F32)
                     for p in _bf16_pieces(w, 3))


def _peer_weights(act, gates, tm):
    t, n = act.shape
    nrow = 2 * PACK_ROWS * n
    rep = (jnp.arange(n)[:, None] == (jnp.arange(nrow) // (2 * PACK_ROWS))[None, :]).astype(BF16)
    tok = lambda width: pl.BlockSpec((tm, width), lambda i: (i, 0))
    return pl.pallas_call(
        _peer_weights_body,
        grid=(t // tm,),
        in_specs=[tok(n), tok(n), pl.BlockSpec(rep.shape, lambda i: (0, 0))],
        out_specs=tok(nrow),
        out_shape=jax.ShapeDtypeStruct((t, nrow), F32),
        compiler_params=_params("parallel"),
        name="peer_weights",
    )(act, gates, rep)


def _peer_sum_body(has_gain, *refs):
    idx_ref = refs[:IDX_REFS]
    w_ref, h_ref, mask_ref = refs[IDX_REFS:IDX_REFS + 3]
    gain_ref = refs[IDX_REFS + 3] if has_gain else None
    tbl_ref, y_ref, g0_ref, g1_ref = refs[IDX_REFS + 3 + has_gain:]
    tb = y_ref.shape[0]
    mask = mask_ref[...]
    nrow = g0_ref.shape[0] // STAGE_STEPS
    ncol = mask.shape[1] // STAGE_STEPS
    _stage_rows(idx_ref, 0, tbl_ref, g0_ref)

    def finish(i, tiles):
        base = pl.multiple_of(i * SUBLANES, SUBLANES)
        slabs = []
        for r in range(SUBLANES):
            y_r = jnp.concatenate([tile[r:r + 1] for tile in tiles], axis=0)
            slabs.append(h_ref[pl.ds(base, SUBLANES), r * LANES:(r + 1) * LANES] + y_r)
        if gain_ref is not None:
            ms = sum(jnp.sum(s * s, axis=1, keepdims=True) for s in slabs)
            inv = lax.rsqrt(ms * (1.0 / (SUBLANES * LANES)) + EPS)
            slabs = [s * inv * gain_ref[:, r * LANES:(r + 1) * LANES]
                     for r, s in enumerate(slabs)]
        for r, s in enumerate(slabs):
            y_ref[pl.ds(base, SUBLANES), r * LANES:(r + 1) * LANES] = s

    def group(i, prev_rows):
        finish(jnp.maximum(i - 1, 0), prev_rows)
        base = pl.multiple_of(i * SUBLANES, SUBLANES)
        w_rep = w_ref[pl.ds(base, SUBLANES), :]
        acc = [None]

        def step_fn(t, k, step, g_ref):
            cols = slice(step * ncol, (step + 1) * ncol)
            wb = jnp.broadcast_to(w_rep[k:k + 1, cols], (SUBLANES, ncol)) * mask[:, cols]
            a_hi = wb.astype(BF16)
            a_lo = (wb - a_hi.astype(F32)).astype(BF16)
            lhs = jnp.concatenate([a_hi, a_lo], axis=0)
            staged = pltpu.bitcast(g_ref[step * nrow:(step + 1) * nrow, :], BF16)
            part = jnp.dot(lhs, staged, preferred_element_type=F32)
            acc[0] = part if step == 0 else acc[0] + part
            if step == STAGE_STEPS - 1:
                rows.append(acc[0][:SUBLANES] + acc[0][SUBLANES:])

        rows = []
        _staged_group(idx_ref, tbl_ref, (g0_ref, g1_ref), tb, base, step_fn)
        return tuple(rows)

    ngroup = tb // SUBLANES
    zeros = tuple(jnp.zeros((SUBLANES, LANES), F32) for _ in range(SUBLANES))
    finish(ngroup - 1, lax.fori_loop(0, ngroup, group, zeros))


def _peer_sum(idx_split, w, h, gain, tbl, tb):
    t, nrow = w.shape
    d = h.shape[1]
    nsel = PEER_SEL
    tile = pl.BlockSpec((tb, d), lambda i: (i, 0))
    extra = [] if gain is None else [gain.reshape(1, d)]
    c = jnp.arange(nrow)
    out_row = (c % 2) * PACK_ROWS + (c % (2 * PACK_ROWS)) // 2
    mask = (jnp.arange(SUBLANES)[:, None] == out_row[None, :]).astype(F32)
    gbuf = pltpu.VMEM((nsel * PACK_ROWS, LANES), jnp.uint32)
    const = lambda shape: pl.BlockSpec(shape, lambda i: (0, 0))
    return pl.pallas_call(
        functools.partial(_peer_sum_body, len(extra)),
        grid=(t // tb,),
        in_specs=_idx_specs(tb) + [
            pl.BlockSpec((tb, nrow), lambda i: (i, 0)),
            tile,
            const(mask.shape)] + [const((1, d))] * len(extra) + [
            pl.BlockSpec(tbl.shape, lambda i: (0, 0), pipeline_mode=pl.Buffered(1))],
        out_specs=tile,
        out_shape=jax.ShapeDtypeStruct((t, d), F32),
        scratch_shapes=[gbuf, gbuf],
        compiler_params=_params("arbitrary"),
        name="peer_sum",
    )(*idx_split, w, h, mask, *extra, tbl)


def _pick_tile(n, pref):
    tile = min(n, pref)
    while n % tile:
        tile //= 2
    return tile


def _layer(h, mem, batch, seq, norm_mix, w_in, conv_w, dn_conv_w, dn_a_log, dn_dt_bias, dn_norm,
           w_out, norm_x, norm_mem, w_xq, w_xkv, w_xo, norm_ffn, w_pq, peer_keys,
           expert_u, expert_v, final_gain):
    t, d = h.shape
    mem_len = mem.shape[0] // batch
    in_cols = w_in.shape[1]
    pad = (-in_cols) % LANES
    w_in_p = jnp.pad(w_in, ((0, 0), (0, pad))).astype(BF16)

    proj = _norm_matmul(h, norm_mix, w_in_p, _pick_tile(t, 256), "in_proj")
    y_conv, q, k, v, gb = _mixer_pre(proj, conv_w, dn_conv_w, dn_a_log, dn_dt_bias,
                                     batch, seq, _pick_tile(seq, 256))
    o_dn = _delta_rule(q, k, v, gb, batch, seq)
    h = _mixer_out(h, y_conv, o_dn, proj, dn_norm, w_out, _pick_tile(t, 512))

    kv = _norm_matmul(mem, norm_mem, w_xkv.astype(BF16), _pick_tile(mem.shape[0], 256), "kv_proj")
    h = _cross_attn(h, norm_x, w_xq, kv, w_xo, batch, seq, mem_len, _pick_tile(seq, 512))

    xr, idx_split, gates = _peer_route(h, norm_ffn, w_pq, peer_keys, _pick_tile(t, 512))
    tb = _pick_tile(t, 128)
    act = _peer_dot(idx_split, xr, _pack_table(expert_u), tb)
    w = _peer_weights(act, gates, _pick_tile(t, 1024))
    return _peer_sum(idx_split, w, h, final_gain, _pack_table(expert_v), tb)


def kernel(x, mem, norm_mix, w_in, conv_w, dn_conv_w, dn_a_log, dn_dt_bias, dn_norm, w_out,
           norm_x, norm_mem, w_xq, w_xkv, w_xo, norm_ffn, w_pq, peer_keys, expert_u, expert_v,
           norm_final):
    batch, seq, d = x.shape
    depth = norm_mix.shape[0]
    h = x.reshape(batch * seq, d)
    mem2 = mem.reshape(batch * mem.shape[1], d)
    for layer in range(depth):
        h = _layer(h, mem2, batch, seq, norm_mix[layer], w_in[layer], conv_w[layer],
                   dn_conv_w[layer], dn_a_log[layer], dn_dt_bias[layer], dn_norm[layer],
                   w_out[layer], norm_x[layer], norm_mem[layer], w_xq[layer], w_xkv[layer],
                   w_xo[layer], norm_ffn[layer], w_pq[layer], peer_keys[layer],
                   expert_u[layer], expert_v[layer],
                   norm_final if layer == depth - 1 else None)
    return h.reshape(batch, seq, d)
```

```python
import functools

import jax
import jax.numpy as jnp
from jax import lax
from jax.experimental import pallas as pl
from jax.experimental.pallas import tpu as pltpu

EPS = 1e-6
F32 = jnp.float32
BF16 = jnp.bfloat16
HIGHEST = lax.Precision.HIGHEST

LANES = 128
SUBLANES = 8
VMEM_LIMIT = 56 * 1024 * 1024

CONV_GROUP_K = 3
DN_CONV_K = 4
DN_HEADS = 4
DN_HEAD_DIM = 128
DN_CHUNK = 64
X_HEADS = 4
PEER_HEADS = 8
PEER_KEYS = 128
PEER_TOPK = 16
PEER_HALF = 64


def _params(*sem):
    return pltpu.CompilerParams(dimension_semantics=sem, vmem_limit_bytes=VMEM_LIMIT)


def _rms(x, gain):
    ms = jnp.mean(x * x, axis=-1, keepdims=True)
    return x * lax.rsqrt(ms + EPS) * gain


def _bdot(a, b):
    return jnp.dot(a.astype(BF16), b.astype(BF16), preferred_element_type=F32)


def _hdot(a, b):
    return jnp.dot(a, b, preferred_element_type=F32, precision=HIGHEST)


def _norm_matmul_body(x_ref, g_ref, w_ref, o_ref):
    xn = _rms(x_ref[...], g_ref[...])
    o_ref[...] = jnp.dot(xn.astype(BF16), w_ref[...], preferred_element_type=F32)


def _norm_matmul(x, gain, w, tm, name):
    t, d = x.shape
    n = w.shape[1]
    return pl.pallas_call(
        _norm_matmul_body,
        grid=(t // tm,),
        in_specs=[pl.BlockSpec((tm, d), lambda i: (i, 0)),
                  pl.BlockSpec((1, d), lambda i: (0, 0)),
                  pl.BlockSpec((d, n), lambda i: (0, 0))],
        out_specs=pl.BlockSpec((tm, n), lambda i: (i, 0)),
        out_shape=jax.ShapeDtypeStruct((t, n), F32),
        compiler_params=_params("parallel"),
        name=name,
    )(x, gain.reshape(1, d), w)


def _shift_rows(x, halo, k):
    rolled = pltpu.roll(x, k, 0)
    hal = pltpu.roll(halo, k, 0)
    row = lax.broadcasted_iota(jnp.int32, (SUBLANES, x.shape[1]), 0)
    head = jnp.where(row < k, hal, rolled[:SUBLANES])
    return jnp.concatenate([head, rolled[SUBLANES:]], axis=0)


def _causal_conv(x, halo, w_ref):
    kk = w_ref.shape[0]
    y = x * w_ref[kk - 1:kk, :]
    for j in range(1, kk):
        y = y + _shift_rows(x, halo, j) * w_ref[kk - 1 - j:kk - j, :]
    return y


def _l2norm_heads(x):
    outs = []
    for h in range(DN_HEADS):
        xh = x[:, h * DN_HEAD_DIM:(h + 1) * DN_HEAD_DIM]
        outs.append(xh * lax.rsqrt(jnp.sum(xh * xh, axis=-1, keepdims=True) + EPS))
    return jnp.concatenate(outs, axis=-1)


def _mixer_pre_body(b_ref, c_ref, h_ref, q_ref, k_ref, v_ref, t_ref,
                    ch_ref, hh_ref, qh_ref, kh_ref, vh_ref,
                    cw_ref, qw_ref, kw_ref, vw_ref, alog_ref, dtb_ref, tril_ref,
                    yc_ref, qo_ref, ko_ref, vo_ref, gb_ref):
    first = pl.program_id(1) == 0
    keep = jnp.where(first, 0.0, 1.0).astype(F32)

    u = c_ref[...] * h_ref[...]
    uh = ch_ref[...] * hh_ref[...] * keep
    yc_ref[...] = b_ref[...] * _causal_conv(u, uh, cw_ref)

    def dn_branch(x_ref, xh_ref, w_ref):
        y = _causal_conv(x_ref[...], xh_ref[...] * keep, w_ref)
        return y * jax.nn.sigmoid(y)

    qo_ref[...] = _l2norm_heads(dn_branch(q_ref, qh_ref, qw_ref))
    ko_ref[...] = _l2norm_heads(dn_branch(k_ref, kh_ref, kw_ref))
    vo_ref[...] = dn_branch(v_ref, vh_ref, vw_ref)

    tail = t_ref[...]
    beta = jax.nn.sigmoid(tail)
    g = -jnp.exp(alog_ref[...]) * jax.nn.softplus(tail + dtb_ref[...])
    gcum = _hdot(tril_ref[...], g)
    lane = lax.broadcasted_iota(jnp.int32, tail.shape, 1)
    gb_ref[...] = jnp.where(lane < DN_HEADS, beta, gcum)


def _mixer_pre(proj, conv_w, dn_conv_w, dn_a_log, dn_dt_bias, batch, seq, ts):
    t = proj.shape[0]
    cw = 512
    nblk = seq // ts
    hb = ts // SUBLANES

    def cur(col):
        return pl.BlockSpec((ts, cw), lambda b, i, col=col: (b * nblk + i, col))

    def halo(col):
        return pl.BlockSpec(
            (SUBLANES, cw),
            lambda b, i, col=col: (jnp.maximum((b * nblk + i) * hb - 1, 0), col))

    def full(shape):
        return pl.BlockSpec(shape, lambda b, i: (0,) * len(shape))

    qw, kw, vw = (dn_conv_w[:, j * cw:(j + 1) * cw] for j in range(3))
    lane_pad = jnp.zeros((LANES - 2 * DN_HEADS,), F32)
    alog = jnp.concatenate([jnp.zeros((DN_HEADS,), F32), dn_a_log, lane_pad]).reshape(1, LANES)
    dtb = jnp.concatenate([jnp.zeros((DN_HEADS,), F32), dn_dt_bias, lane_pad]).reshape(1, LANES)
    r = jnp.arange(ts)
    tril = ((r[:, None] >= r[None, :]) &
            (r[:, None] // DN_CHUNK == r[None, :] // DN_CHUNK)).astype(F32)

    tail_spec = pl.BlockSpec((ts, LANES), lambda b, i: (b * nblk + i, 7 * cw // LANES))
    out_tok = lambda w: pl.BlockSpec((ts, w), lambda b, i: (b * nblk + i, 0))
    return pl.pallas_call(
        _mixer_pre_body,
        grid=(batch, nblk),
        in_specs=[cur(0), cur(1), cur(2), cur(3), cur(4), cur(5), tail_spec,
                  halo(1), halo(2), halo(3), halo(4), halo(5),
                  full((CONV_GROUP_K, cw)), full((DN_CONV_K, cw)), full((DN_CONV_K, cw)),
                  full((DN_CONV_K, cw)), full((1, LANES)), full((1, LANES)), full((ts, ts))],
        out_specs=[out_tok(cw), out_tok(cw), out_tok(cw), out_tok(cw), out_tok(LANES)],
        out_shape=[jax.ShapeDtypeStruct((t, cw), F32)] * 4 + [jax.ShapeDtypeStruct((t, LANES), F32)],
        compiler_params=_params("parallel", "parallel"),
        name="mixer_pre",
    )(proj, proj, proj, proj, proj, proj, proj, proj, proj, proj, proj, proj,
      conv_w, qw, kw, vw, alog, dtb, tril)


_NN = (((1,), (0,)), ((), ()))
_NT = (((1,), (1,)), ((), ()))
_TN = (((0,), (0,)), ((), ()))
DN_LOCAL_PASSES = 1
DN_SCAN_PASSES = 1


def _mm(a, b, dims, passes):
    dot = lambda x, y: lax.dot_general(x, y, dims, preferred_element_type=F32)
    if passes == 6:
        return lax.dot_general(a, b, dims, preferred_element_type=F32, precision=HIGHEST)
    a_hi = a.astype(BF16)
    b_hi = b.astype(BF16)
    if passes == 1:
        return dot(a_hi, b_hi)
    a_lo = (a - a_hi.astype(F32)).astype(BF16)
    b_lo = (b - b_hi.astype(F32)).astype(BF16)
    return dot(a_hi, b_hi) + (dot(a_hi, b_lo) + dot(a_lo, b_hi))


DN_LOCAL_CHUNKS = 4


def _dn_local_body(q_ref, k_ref, v_ref, gb_ref, u_ref, w_ref, qd_ref, kd_ref, in_ref, gl_ref):
    c = DN_CHUNK
    n = DN_HEADS * c
    p = DN_LOCAL_PASSES
    cis = range(q_ref.shape[0] // c)
    each = lambda f, *xs: [f(*a) for a in zip(*xs)]
    lanes = lambda x: jnp.broadcast_to(x, (x.shape[0], LANES))

    def stack(ref, ci):
        return jnp.concatenate([ref[ci * c:(ci + 1) * c, h * DN_HEAD_DIM:(h + 1) * DN_HEAD_DIM]
                                for h in range(DN_HEADS)], axis=0)

    def col(gb, j):
        return jnp.concatenate([lanes(gb[:, j + h:j + h + 1]) for h in range(DN_HEADS)], axis=0)

    gb = [gb_ref[ci * c:(ci + 1) * c, :] for ci in cis]
    q = [stack(q_ref, ci) * (DN_HEAD_DIM ** -0.5) for ci in cis]
    k = [stack(k_ref, ci) for ci in cis]
    v = [stack(v_ref, ci) for ci in cis]
    beta = [col(x, 0) for x in gb]
    g = [col(x, DN_HEADS) for x in gb]
    g_last = [jnp.concatenate(
        [jnp.broadcast_to(x[c - 1:c, DN_HEADS + h:DN_HEADS + h + 1], (c, LANES))
         for h in range(DN_HEADS)], axis=0) for x in gb]

    row = lax.broadcasted_iota(jnp.int32, (n, n), 0)
    cl = lax.broadcasted_iota(jnp.int32, (n, n), 1)
    shift = c.bit_length() - 1
    same_head = (row >> shift) == (cl >> shift)
    causal = same_head & (row >= cl)
    strict = same_head & (row > cl)
    eye = (row == cl).astype(F32)
    decay = [jnp.where(causal, jnp.exp(jnp.where(causal, x[:, 0:1] - x.T[0:1, :], 0.0)), 0.0)
             for x in g]
    k_beta = each(lambda a, b: a * b, k, beta)
    v_beta = each(lambda a, b: a * b, v, beta)
    a = each(lambda kb, kk, d: jnp.where(strict, _mm(kb, kk, _NT, p) * d, 0.0), k_beta, k, decay)

    t_mat = [eye - x for x in a]
    pw = [_mm(x, x, _NN, p) for x in a]
    for _ in range(4):
        t_mat = each(lambda t, w_: t + _mm(t, w_, _NN, p), t_mat, pw)
        pw = [_mm(x, x, _NN, p) for x in pw]
    t_mat = each(lambda t, w_: t + _mm(t, w_, _NN, p), t_mat, pw)

    e_g = [jnp.exp(x) for x in g]
    uw = each(lambda t, vb, kb, e: _mm(t, jnp.concatenate([vb, kb * e], axis=1), _NN, p),
              t_mat, v_beta, k_beta, e_g)
    intra = each(lambda qq, kk, d: _mm(qq, kk, _NT, p) * d, q, k, decay)
    for ci in cis:
        u_ref[ci] = uw[ci][:, :DN_HEAD_DIM]
        w_ref[ci] = uw[ci][:, DN_HEAD_DIM:]
        in_ref[ci] = intra[ci]
        qd_ref[ci] = q[ci] * e_g[ci]
        kd_ref[ci] = k[ci] * jnp.exp(g_last[ci] - g[ci])
        gl_ref[ci] = jnp.concatenate(
            [jnp.exp(g_last[ci][h * c:h * c + 1]) for h in range(DN_HEADS)] +
            [jnp.zeros((SUBLANES - DN_HEADS, LANES), F32)], axis=0)


def _dn_scan_body(u_ref, w_ref, qd_ref, kd_ref, in_ref, gl_ref, o_ref, state_ref):
    c = DN_CHUNK
    p = DN_SCAN_PASSES

    @pl.when(pl.program_id(0) == 0)
    def _():
        state_ref[...] = jnp.zeros_like(state_ref)

    nb = range(u_ref.shape[0])
    heads = range(DN_HEADS)
    rows = [slice(h * c, (h + 1) * c) for h in heads]
    ws = [[_mm(jnp.concatenate([w_ref[b, 0][rows[h]], qd_ref[b, 0][rows[h]]], axis=0),
               state_ref[b, h], _NN, p) for h in heads] for b in nb]
    v_new = [[u_ref[b, 0][rows[h]] - ws[b][h][:c] for h in heads] for b in nb]
    o = [jnp.concatenate([ws[b][h][c:] for h in heads], axis=0) +
         _mm(in_ref[b, 0], jnp.concatenate(v_new[b], axis=0), _NN, p) for b in nb]
    for b in nb:
        for h in heads:
            state_ref[b, h] = (state_ref[b, h] * gl_ref[b, 0][h:h + 1] +
                               _mm(kd_ref[b, 0][rows[h]], v_new[b][h], _TN, p))
    for b in nb:
        o_ref[b] = jnp.concatenate([o[b][rows[h]] for h in heads], axis=1)


def _delta_rule(q, k, v, gb, batch, seq):
    t, w = q.shape
    nc = seq // DN_CHUNK
    n = DN_HEADS * DN_CHUNK
    cps = _pick_tile(batch * nc, DN_LOCAL_CHUNKS)
    tok = lambda width: pl.BlockSpec((cps * DN_CHUNK, width), lambda i: (i, 0))
    per_chunk = lambda rows, width: pl.BlockSpec((cps, rows, width), lambda i: (i, 0, 0))
    f = lambda rows, width: jax.ShapeDtypeStruct((batch * nc, rows, width), F32)
    u, wm, qd, kd, intra, gl = pl.pallas_call(
        _dn_local_body,
        grid=(batch * nc // cps,),
        in_specs=[tok(w), tok(w), tok(w), tok(LANES)],
        out_specs=[per_chunk(n, DN_HEAD_DIM)] * 4 + [per_chunk(n, n), per_chunk(SUBLANES, LANES)],
        out_shape=[f(n, DN_HEAD_DIM)] * 4 + [f(n, n), f(SUBLANES, LANES)],
        compiler_params=_params("parallel"),
        name="dn_local",
    )(q, k, v, gb)

    seq_blk = lambda rows, width: pl.BlockSpec((batch, 1, rows, width), lambda i: (0, i, 0, 0))
    by_batch = lambda x: x.reshape(batch, nc, *x.shape[1:])
    o = pl.pallas_call(
        _dn_scan_body,
        grid=(nc,),
        in_specs=[seq_blk(n, DN_HEAD_DIM)] * 4 + [seq_blk(n, n), seq_blk(SUBLANES, LANES)],
        out_specs=pl.BlockSpec((batch, DN_CHUNK, w), lambda i: (0, i, 0)),
        out_shape=jax.ShapeDtypeStruct((batch, seq, w), F32),
        scratch_shapes=[pltpu.VMEM((batch, DN_HEADS, DN_HEAD_DIM, DN_HEAD_DIM), F32)],
        compiler_params=_params("arbitrary"),
        name="dn_scan",
    )(by_batch(u), by_batch(wm), by_batch(qd), by_batch(kd), by_batch(intra), by_batch(gl))
    return o.reshape(t, w)


def _mixer_out_body(x_ref, yc_ref, o_ref, z_ref, dng_ref, wa_ref, wb_ref, h_ref):
    o = o_ref[...]
    z = z_ref[...]
    gain = dng_ref[...]
    parts = []
    for h in range(DN_HEADS):
        sl = slice(h * DN_HEAD_DIM, (h + 1) * DN_HEAD_DIM)
        zh = z[:, sl]
        parts.append(_rms(o[:, sl], gain) * (zh * jax.nn.sigmoid(zh)))
    y_dn = jnp.concatenate(parts, axis=-1)
    h_ref[...] = (x_ref[...] + _bdot(yc_ref[...], wa_ref[...]) + _bdot(y_dn, wb_ref[...]))


def _mixer_out(x, y_conv, o_dn, proj, dn_norm, w_out, tm):
    t, d = x.shape
    cw = y_conv.shape[1]
    tok = lambda width, col=0: pl.BlockSpec((tm, width), lambda i, col=col: (i, col))
    full = lambda shape: pl.BlockSpec(shape, lambda i: (0, 0))
    return pl.pallas_call(
        _mixer_out_body,
        grid=(t // tm,),
        in_specs=[tok(d), tok(cw), tok(cw), tok(cw, 6), full((1, DN_HEAD_DIM)),
                  full((cw, d)), full((cw, d))],
        out_specs=tok(d),
        out_shape=jax.ShapeDtypeStruct((t, d), F32),
        compiler_params=_params("parallel"),
        name="mixer_out",
    )(x, y_conv, o_dn, proj, dn_norm.reshape(1, DN_HEAD_DIM),
      w_out[:cw].astype(BF16), w_out[cw:].astype(BF16))


def _cross_attn_body(h_ref, g_ref, wq_ref, kv_ref, wo_ref, o_ref):
    h_in = h_ref[...]
    d = h_in.shape[1]
    dh = d // X_HEADS
    q = _bdot(_rms(h_in, g_ref[...]), wq_ref[...])
    outs = []
    for hd in range(X_HEADS):
        qh = q[:, hd * dh:(hd + 1) * dh].astype(BF16)
        kh = kv_ref[:, hd * dh:(hd + 1) * dh]
        vh = kv_ref[:, d + hd * dh:d + (hd + 1) * dh]
        s = lax.dot_general(qh, kh, (((1,), (1,)), ((), ())),
                            preferred_element_type=F32) * (dh ** -0.5)
        s = s - jnp.max(s, axis=-1, keepdims=True)
        e = jnp.exp(s)
        p = e / jnp.sum(e, axis=-1, keepdims=True)
        outs.append(jnp.dot(p.astype(BF16), vh, preferred_element_type=F32))
    o = jnp.concatenate(outs, axis=-1)
    o_ref[...] = h_in + _bdot(o, wo_ref[...])


def _cross_attn(h, gain, w_q, kv, w_o, batch, seq, mem_len, tm):
    t, d = h.shape
    nblk = seq // tm
    tok = pl.BlockSpec((tm, d), lambda b, i: (b * nblk + i, 0))
    full = lambda shape: pl.BlockSpec(shape, lambda b, i: (0, 0))
    return pl.pallas_call(
        _cross_attn_body,
        grid=(batch, nblk),
        in_specs=[tok, full((1, d)), full((d, d)),
                  pl.BlockSpec((mem_len, 2 * d), lambda b, i: (b, 0)), full((d, d))],
        out_specs=tok,
        out_shape=jax.ShapeDtypeStruct((t, d), F32),
        compiler_params=_params("parallel", "parallel"),
        name="cross_attn",
    )(h, gain.reshape(1, d), w_q.astype(BF16), kv.astype(BF16), w_o.astype(BF16))


def _top16_rows(s, payload=None):
    n = s.shape[0]
    row = lax.broadcasted_iota(jnp.int32, s.shape, 0).astype(F32)
    vals, pays = [], []
    for _ in range(PEER_TOPK):
        m = jnp.max(s, axis=0, keepdims=True)
        pos = jnp.min(jnp.where(s == m, row, float(n)), axis=0, keepdims=True)
        hit = row == pos
        vals.append(m)
        if payload is None:
            pays.append(pos)
        else:
            pays.append(jnp.sum(jnp.where(hit, payload, 0.0), axis=0, keepdims=True))
        s = jnp.where(hit, -jnp.inf, s)
    return jnp.concatenate(vals, axis=0), jnp.concatenate(pays, axis=0)


def _pair_candidates(v1, v2, pad):
    sub = lax.broadcasted_iota(jnp.int32, (SUBLANES, v1.shape[1]), 0)
    row = lambda v, a: jnp.broadcast_to(v[a:a + 1], sub.shape)
    lo8 = v2[:SUBLANES]
    blocks = [
        row(v1, 0) + lo8,
        row(v1, 0) + v2[SUBLANES:],
        row(v1, 1) + lo8,
        jnp.where(sub < 5, row(v1, 2) + lo8, pad),
        jnp.where(sub < 7,
                  jnp.where(sub < 4, row(v1, 3), row(v1, 4)) +
                  jnp.where(sub < 4, lo8, pltpu.roll(lo8, 4, 0)), pad),
        jnp.where(sub < 6,
                  jnp.where(sub < 2, row(v1, 5), jnp.where(sub < 4, row(v1, 6), row(v1, 7))) +
                  jnp.where((sub & 1) == 0, row(v2, 0), row(v2, 1)), pad),
        v1[SUBLANES:] + row(v2, 0),
    ]
    return jnp.concatenate(blocks, axis=0)


def _peer_route_body(h_ref, g_ref, wq_ref, keys_ref, xn_ref, gate_ref, *rest):
    idx_refs = rest[:PEER_HEADS]
    sc_ref, sel_ref, gt_ref, it_ref = rest[PEER_HEADS:]
    xn = _rms(h_ref[...], g_ref[...])
    for r in range(SUBLANES):
        xn_ref[:, r] = xn[:, r * LANES:(r + 1) * LANES].reshape(-1, SUBLANES, LANES)
    pq = _bdot(xn, wq_ref[...])
    ngrp = pq.shape[0] // LANES
    for j in range(2 * PEER_HEADS):
        qs = pq[:, j * PEER_HALF:(j + 1) * PEER_HALF].astype(BF16)
        sc = lax.dot_general(keys_ref[j], qs, (((1,), (1,)), ((), ())),
                             preferred_element_type=F32)
        for g in range(ngrp):
            sc_ref[j, g] = sc[:, g * LANES:(g + 1) * LANES]

    def head(h, carry):
        for g in range(ngrp):
            s1, i1 = _top16_rows(sc_ref[2 * h, g])
            s2, i2 = _top16_rows(sc_ref[2 * h + 1, g])
            cand = _pair_candidates(s1, s2, -jnp.inf)
            cand_idx = _pair_candidates(i1 * float(PEER_KEYS), i2, 0.0)
            best, sel = _top16_rows(cand, cand_idx)
            e = jnp.exp(best - jnp.max(best, axis=0, keepdims=True))
            sel_ref[g, h] = sel
            gt_ref[g, h] = e / jnp.sum(e, axis=0, keepdims=True)
        return carry

    lax.fori_loop(0, PEER_HEADS, head, 0)

    nsel = PEER_HEADS * PEER_TOPK
    per_row = LANES // PEER_TOPK
    lane_blk = lax.broadcasted_iota(jnp.int32, (LANES // per_row, LANES), 1) // PEER_TOPK
    for g in range(ngrp):
        rows = slice(g * LANES, (g + 1) * LANES)
        gate_ref[rows, :] = gt_ref[g].reshape(nsel, LANES).T
        it_ref[...] = (sel_ref[g].reshape(nsel, LANES).T * float(PACK_ROWS)).astype(jnp.int32)
        by_k = [it_ref[pl.ds(k, LANES // per_row, stride=per_row), :] for k in range(per_row)]
        out_rows = slice(g * LANES // per_row, (g + 1) * LANES // per_row)
        for hd in range(PEER_HEADS):
            out = jnp.zeros_like(by_k[0])
            for k in range(per_row):
                shift = ((k - hd) * PEER_TOPK) % LANES
                moved = pltpu.roll(by_k[k], shift, 1) if shift else by_k[k]
                out = jnp.where(lane_blk == k, moved, out)
            idx_refs[hd][out_rows, :] = out


def _peer_route(h, gain, w_pq, peer_keys, tm):
    t, d = h.shape
    nk = PEER_HEADS * 2
    ngrp = tm // LANES
    tok = lambda width: pl.BlockSpec((tm, width), lambda i: (i, 0))
    full = lambda shape: pl.BlockSpec(shape, lambda i: (0,) * len(shape))
    idx_rows = tm * PEER_TOPK // LANES
    outs = pl.pallas_call(
        _peer_route_body,
        grid=(t // tm,),
        in_specs=[tok(d), full((1, d)), full((d, d)), full((nk, PEER_KEYS, PEER_HALF))],
        out_specs=[pl.BlockSpec((tm // SUBLANES, SUBLANES, SUBLANES, LANES), lambda i: (i, 0, 0, 0)),
                   tok(LANES)] + [pl.BlockSpec((idx_rows, LANES), lambda i: (i, 0))] * PEER_HEADS,
        out_shape=[jax.ShapeDtypeStruct((t // SUBLANES, SUBLANES, SUBLANES, LANES), F32),
                   jax.ShapeDtypeStruct((t, LANES), F32)] +
                  [jax.ShapeDtypeStruct((t * PEER_TOPK // LANES, LANES), jnp.int32)] * PEER_HEADS,
        scratch_shapes=[pltpu.VMEM((nk, ngrp, PEER_KEYS, LANES), F32),
                        pltpu.VMEM((ngrp, PEER_HEADS, PEER_TOPK, LANES), F32),
                        pltpu.VMEM((ngrp, PEER_HEADS, PEER_TOPK, LANES), F32),
                        pltpu.VMEM((LANES, LANES), jnp.int32)],
        compiler_params=_params("parallel"),
        name="peer_route",
    )(h, gain.reshape(1, d), w_pq.astype(BF16),
      peer_keys.reshape(nk, PEER_KEYS, PEER_HALF).astype(BF16))
    xr, gates = outs[0], outs[1]
    return xr, tuple(o.reshape(-1) for o in outs[2:]), gates


PACK_ROWS = 4


def _pack_table_body(t_ref, o_ref):
    rows, d = t_ref.shape
    half = d // 2
    bits = lambda x: pltpu.bitcast(x.astype(BF16).astype(F32), jnp.uint32)
    word = (bits(t_ref[:, half:]) & jnp.uint32(0xFFFF0000)) | (bits(t_ref[:, :half]) >> 16)
    for j in range(PACK_ROWS):
        o_ref[pl.ds(j, rows, stride=PACK_ROWS), :] = word[:, j * LANES:(j + 1) * LANES]


def _pack_table(tbl):
    e, d = tbl.shape
    rows = _pick_tile(e, 512)
    return pl.pallas_call(
        _pack_table_body,
        grid=(e // rows,),
        in_specs=[pl.BlockSpec((rows, d), lambda i: (i, 0))],
        out_specs=pl.BlockSpec((rows * PACK_ROWS, LANES), lambda i: (i, 0)),
        out_shape=jax.ShapeDtypeStruct((e * PACK_ROWS, LANES), jnp.uint32),
        compiler_params=_params("parallel"),
        name="pack_table",
    )(tbl)


def _unpack(slab):
    lo = pltpu.bitcast(slab << 16, F32)
    hi = pltpu.bitcast(slab & jnp.uint32(0xFFFF0000), F32)
    return lo, hi


STAGE_STEPS = 4
PEER_SEL = PEER_HEADS * PEER_TOPK
IDX_REFS = PEER_HEADS


def _idx_specs(tb):
    n = tb * PEER_SEL // IDX_REFS
    return [pl.BlockSpec((n,), lambda i: (i,), memory_space=pltpu.SMEM) for _ in range(IDX_REFS)]


def _stage_rows(idx_refs, t, tbl_ref, g_ref, step=None):
    ncol = PEER_SEL // IDX_REFS
    per = ncol // STAGE_STEPS
    qs = range(ncol) if step is None else range(step * per, (step + 1) * per)
    for q in qs:
        off = t * ncol + q
        for r in range(IDX_REFS):
            m = r * ncol + q
            row = pl.multiple_of(idx_refs[r][off], PACK_ROWS)
            g_ref[m * PACK_ROWS:(m + 1) * PACK_ROWS, :] = tbl_ref[pl.ds(row, PACK_ROWS), :]


def _bf16_pieces(x, n):
    pieces = []
    for _ in range(n - 1):
        p = x.astype(BF16).astype(F32)
        pieces.append(p)
        x = x - p
    pieces.append(x.astype(BF16).astype(F32))
    return pieces


def _staged_group(idx_ref, tbl_ref, bufs, tb, base, step_fn):
    for k in range(SUBLANES):
        t = base + k
        nxt = t + 1 if k + 1 < SUBLANES else jnp.minimum(t + 1, tb - 1)
        for step in range(STAGE_STEPS):
            _stage_rows(idx_ref, nxt, tbl_ref, bufs[(k + 1) % 2], step)
            step_fn(t, k, step, bufs[k % 2])


def _peer_dot_body(*refs):
    idx_ref = refs[:IDX_REFS]
    x_ref, tbl_ref, act_ref, g0_ref, g1_ref = refs[IDX_REFS:]
    tb, nsel = act_ref.shape
    nslab = SUBLANES * SUBLANES
    lane = lax.broadcasted_iota(jnp.int32, (LANES, LANES), 1)
    col_slab = lane & (nslab - 1)
    _stage_rows(idx_ref, 0, tbl_ref, g0_ref)

    def group(i, carry):
        base = pl.multiple_of(i * SUBLANES, SUBLANES)
        pieces = _bf16_pieces(x_ref[i].reshape(nslab, LANES), LANES // nslab)
        xt = jnp.concatenate(pieces, axis=0).T.astype(BF16)
        zero = jnp.zeros_like(xt)
        acc = [jnp.zeros((nsel, LANES), F32)]

        def step_fn(t, k, j, g_ref):
            lo, hi = _unpack(g_ref[pl.ds(j, nsel, stride=PACK_ROWS), :])
            lhs = jnp.concatenate([lo.astype(BF16), hi.astype(BF16)], axis=1)
            rhs = jnp.concatenate(
                [jnp.where(col_slab == SUBLANES * r + k, xt, zero) for r in (j, PACK_ROWS + j)],
                axis=0)
            acc[0] = acc[0] + jnp.dot(lhs, rhs, preferred_element_type=F32)

        _staged_group(idx_ref, tbl_ref, (g0_ref, g1_ref), tb, base, step_fn)
        out_t = acc[0].T
        act_ref[pl.ds(base, SUBLANES), :] = jnp.sum(
            out_t.reshape(LANES // SUBLANES, SUBLANES, nsel), axis=0)
        return carry

    lax.fori_loop(0, tb // SUBLANES, group, 0)


def _peer_dot(idx_split, xr, tbl, tb):
    t = xr.shape[0] * SUBLANES
    nsel = PEER_SEL
    gbuf = pltpu.VMEM((nsel * PACK_ROWS, LANES), jnp.uint32)
    return pl.pallas_call(
        _peer_dot_body,
        grid=(t // tb,),
        in_specs=_idx_specs(tb) + [
            pl.BlockSpec((tb // SUBLANES, SUBLANES, SUBLANES, LANES), lambda i: (i, 0, 0, 0)),
            pl.BlockSpec(tbl.shape, lambda i: (0, 0), pipeline_mode=pl.Buffered(1))],
        out_specs=pl.BlockSpec((tb, nsel), lambda i: (i, 0)),
        out_shape=jax.ShapeDtypeStruct((t, nsel), F32),
        scratch_shapes=[gbuf, gbuf],
        compiler_params=_params("arbitrary"),
        name="peer_dot",
    )(*idx_split, xr, tbl)


def _peer_weights_body(act_ref, gate_ref, rep_ref, w_ref):
    a = act_ref[...]
    w = gate_ref[...] * (0.5 * a * (1.0 + lax.erf(a * (2.0 ** -0.5))))
    rep = rep_ref[...]
    w_ref[...] = sum(jnp.dot(p.astype(BF16), rep, preferred_element_type=F32)
                     for p in _bf16_pieces(w, 3))


def _peer_weights(act, gates, tm):
    t, n = act.shape
    nrow = 2 * PACK_ROWS * n
    rep = (jnp.arange(n)[:, None] == (jnp.arange(nrow) // (2 * PACK_ROWS))[None, :]).astype(BF16)
    tok = lambda width: pl.BlockSpec((tm, width), lambda i: (i, 0))
    return pl.pallas_call(
        _peer_weights_body,
        grid=(t // tm,),
        in_specs=[tok(n), tok(n), pl.BlockSpec(rep.shape, lambda i: (0, 0))],
        out_specs=tok(nrow),
        out_shape=jax.ShapeDtypeStruct((t, nrow), F32),
        compiler_params=_params("parallel"),
        name="peer_weights",
    )(act, gates, rep)


def _peer_sum_body(has_gain, *refs):
    idx_ref = refs[:IDX_REFS]
    w_ref, h_ref, mask_ref = refs[IDX_REFS:IDX_REFS + 3]
    gain_ref = refs[IDX_REFS + 3] if has_gain else None
    tbl_ref, y_ref, g0_ref, g1_ref = refs[IDX_REFS + 3 + has_gain:]
    tb = y_ref.shape[0]
    mask = mask_ref[...]
    nrow = g0_ref.shape[0] // STAGE_STEPS
    ncol = mask.shape[1] // STAGE_STEPS
    _stage_rows(idx_ref, 0, tbl_ref, g0_ref)

    def finish(i, tiles):
        base = pl.multiple_of(i * SUBLANES, SUBLANES)
        slabs = []
        for r in range(SUBLANES):
            y_r = jnp.concatenate([tile[r:r + 1] for tile in tiles], axis=0)
            slabs.append(h_ref[pl.ds(base, SUBLANES), r * LANES:(r + 1) * LANES] + y_r)
        if gain_ref is not None:
            ms = sum(jnp.sum(s * s, axis=1, keepdims=True) for s in slabs)
            inv = lax.rsqrt(ms * (1.0 / (SUBLANES * LANES)) + EPS)
            slabs = [s * inv * gain_ref[:, r * LANES:(r + 1) * LANES]
                     for r, s in enumerate(slabs)]
        for r, s in enumerate(slabs):
            y_ref[pl.ds(base, SUBLANES), r * LANES:(r + 1) * LANES] = s

    def group(i, prev_rows):
        finish(jnp.maximum(i - 1, 0), prev_rows)
        base = pl.multiple_of(i * SUBLANES, SUBLANES)
        w_rep = w_ref[pl.ds(base, SUBLANES), :]
        acc = [None]

        def step_fn(t, k, step, g_ref):
            cols = slice(step * ncol, (step + 1) * ncol)
            wb = jnp.broadcast_to(w_rep[k:k + 1, cols], (SUBLANES, ncol)) * mask[:, cols]
            a_hi = wb.astype(BF16)
            a_lo = (wb - a_hi.astype(F32)).astype(BF16)
            lhs = jnp.concatenate([a_hi, a_lo], axis=0)
            staged = pltpu.bitcast(g_ref[step * nrow:(step + 1) * nrow, :], BF16)
            part = jnp.dot(lhs, staged, preferred_element_type=F32)
            acc[0] = part if step == 0 else acc[0] + part
            if step == STAGE_STEPS - 1:
                rows.append(acc[0][:SUBLANES] + acc[0][SUBLANES:])

        rows = []
        _staged_group(idx_ref, tbl_ref, (g0_ref, g1_ref), tb, base, step_fn)
        return tuple(rows)

    ngroup = tb // SUBLANES
    zeros = tuple(jnp.zeros((SUBLANES, LANES), F32) for _ in range(SUBLANES))
    finish(ngroup - 1, lax.fori_loop(0, ngroup, group, zeros))


def _peer_sum(idx_split, w, h, gain, tbl, tb):
    t, nrow = w.shape
    d = h.shape[1]
    nsel = PEER_SEL
    tile = pl.BlockSpec((tb, d), lambda i: (i, 0))
    extra = [] if gain is None else [gain.reshape(1, d)]
    c = jnp.arange(nrow)
    out_row = (c % 2) * PACK_ROWS + (c % (2 * PACK_ROWS)) // 2
    mask = (jnp.arange(SUBLANES)[:, None] == out_row[None, :]).astype(F32)
    gbuf = pltpu.VMEM((nsel * PACK_ROWS, LANES), jnp.uint32)
    const = lambda shape: pl.BlockSpec(shape, lambda i: (0, 0))
    return pl.pallas_call(
        functools.partial(_peer_sum_body, len(extra)),
        grid=(t // tb,),
        in_specs=_idx_specs(tb) + [
            pl.BlockSpec((tb, nrow), lambda i: (i, 0)),
            tile,
            const(mask.shape)] + [const((1, d))] * len(extra) + [
            pl.BlockSpec(tbl.shape, lambda i: (0, 0), pipeline_mode=pl.Buffered(1))],
        out_specs=tile,
        out_shape=jax.ShapeDtypeStruct((t, d), F32),
        scratch_shapes=[gbuf, gbuf],
        compiler_params=_params("arbitrary"),
        name="peer_sum",
    )(*idx_split, w, h, mask, *extra, tbl)


def _pick_tile(n, pref):
    tile = min(n, pref)
    while n % tile:
        tile //= 2
    return tile


def _layer(h, mem, batch, seq, norm_mix, w_in, conv_w, dn_conv_w, dn_a_log, dn_dt_bias, dn_norm,
           w_out, norm_x, norm_mem, w_xq, w_xkv, w_xo, norm_ffn, w_pq, peer_keys,
           expert_u, expert_v, final_gain):
    t, d = h.shape
    mem_len = mem.shape[0] // batch
    in_cols = w_in.shape[1]
    pad = (-in_cols) % LANES
    w_in_p = jnp.pad(w_in, ((0, 0), (0, pad))).astype(BF16)

    proj = _norm_matmul(h, norm_mix, w_in_p, _pick_tile(t, 256), "in_proj")
    y_conv, q, k, v, gb = _mixer_pre(proj, conv_w, dn_conv_w, dn_a_log, dn_dt_bias,
                                     batch, seq, _pick_tile(seq, 256))
    o_dn = _delta_rule(q, k, v, gb, batch, seq)
    h = _mixer_out(h, y_conv, o_dn, proj, dn_norm, w_out, _pick_tile(t, 512))

    kv = _norm_matmul(mem, norm_mem, w_xkv.astype(BF16), _pick_tile(mem.shape[0], 256), "kv_proj")
    h = _cross_attn(h, norm_x, w_xq, kv, w_xo, batch, seq, mem_len, _pick_tile(seq, 512))

    xr, idx_split, gates = _peer_route(h, norm_ffn, w_pq, peer_keys, _pick_tile(t, 512))
    tb = _pick_tile(t, 128)
    act = _peer_dot(idx_split, xr, _pack_table(expert_u), tb)
    w = _peer_weights(act, gates, _pick_tile(t, 1024))
    return _peer_sum(idx_split, w, h, final_gain, _pack_table(expert_v), tb)


def kernel(x, mem, norm_mix, w_in, conv_w, dn_conv_w, dn_a_log, dn_dt_bias, dn_norm, w_out,
           norm_x, norm_mem, w_xq, w_xkv, w_xo, norm_ffn, w_pq, peer_keys, expert_u, expert_v,
           norm_final):
    batch, seq, d = x.shape
    depth = norm_mix.shape[0]
    h = x.reshape(batch * seq, d)
    mem2 = mem.reshape(batch * mem.shape[1], d)
    for layer in range(depth):
        h = _layer(h, mem2, batch, seq, norm_mix[layer], w_in[layer], conv_w[layer],
                   dn_conv_w[layer], dn_a_log[layer], dn_dt_bias[layer], dn_norm[layer],
                   w_out[layer], norm_x[layer], norm_mem[layer], w_xq[layer], w_xkv[layer],
                   w_xo[layer], norm_ffn[layer], w_pq[layer], peer_keys[layer],
                   expert_u[layer], expert_v[layer],
                   norm_final if layer == depth - 1 else None)
    return h.reshape(batch, seq, d)
```

```python
import functools

import jax
import jax.numpy as jnp
from jax import lax
from jax.experimental import pallas as pl
from jax.experimental.pallas import tpu as pltpu

EPS = 1e-6
F32 = jnp.float32
BF16 = jnp.bfloat16
HIGHEST = lax.Precision.HIGHEST

LANES = 128
SUBLANES = 8
VMEM_LIMIT = 56 * 1024 * 1024

CONV_GROUP_K = 3
DN_CONV_K = 4
DN_HEADS = 4
DN_HEAD_DIM = 128
DN_CHUNK = 64
X_HEADS = 4
PEER_HEADS = 8
PEER_KEYS = 128
PEER_TOPK = 16
PEER_HALF = 64


def _params(*sem):
    return pltpu.CompilerParams(dimension_semantics=sem, vmem_limit_bytes=VMEM_LIMIT)


def _rms(x, gain):
    ms = jnp.mean(x * x, axis=-1, keepdims=True)
    return x * lax.rsqrt(ms + EPS) * gain


def _bdot(a, b):
    return jnp.dot(a.astype(BF16), b.astype(BF16), preferred_element_type=F32)


def _hdot(a, b):
    return jnp.dot(a, b, preferred_element_type=F32, precision=HIGHEST)


def _norm_matmul_body(x_ref, g_ref, w_ref, o_ref):
    xn = _rms(x_ref[...], g_ref[...])
    o_ref[...] = jnp.dot(xn.astype(BF16), w_ref[...], preferred_element_type=F32)


def _norm_matmul(x, gain, w, tm, name):
    t, d = x.shape
    n = w.shape[1]
    return pl.pallas_call(
        _norm_matmul_body,
        grid=(t // tm,),
        in_specs=[pl.BlockSpec((tm, d), lambda i: (i, 0)),
                  pl.BlockSpec((1, d), lambda i: (0, 0)),
                  pl.BlockSpec((d, n), lambda i: (0, 0))],
        out_specs=pl.BlockSpec((tm, n), lambda i: (i, 0)),
        out_shape=jax.ShapeDtypeStruct((t, n), F32),
        compiler_params=_params("parallel"),
        name=name,
    )(x, gain.reshape(1, d), w)


def _shift_rows(x, halo, k):
    rolled = pltpu.roll(x, k, 0)
    hal = pltpu.roll(halo, k, 0)
    row = lax.broadcasted_iota(jnp.int32, (SUBLANES, x.shape[1]), 0)
    head = jnp.where(row < k, hal, rolled[:SUBLANES])
    return jnp.concatenate([head, rolled[SUBLANES:]], axis=0)


def _causal_conv(x, halo, w_ref):
    kk = w_ref.shape[0]
    y = x * w_ref[kk - 1:kk, :]
    for j in range(1, kk):
        y = y + _shift_rows(x, halo, j) * w_ref[kk - 1 - j:kk - j, :]
    return y


def _l2norm_heads(x):
    outs = []
    for h in range(DN_HEADS):
        xh = x[:, h * DN_HEAD_DIM:(h + 1) * DN_HEAD_DIM]
        outs.append(xh * lax.rsqrt(jnp.sum(xh * xh, axis=-1, keepdims=True) + EPS))
    return jnp.concatenate(outs, axis=-1)


def _mixer_pre_body(b_ref, c_ref, h_ref, q_ref, k_ref, v_ref, t_ref,
                    ch_ref, hh_ref, qh_ref, kh_ref, vh_ref,
                    cw_ref, qw_ref, kw_ref, vw_ref, alog_ref, dtb_ref, tril_ref,
                    yc_ref, qo_ref, ko_ref, vo_ref, gb_ref):
    first = pl.program_id(1) == 0
    keep = jnp.where(first, 0.0, 1.0).astype(F32)

    u = c_ref[...] * h_ref[...]
    uh = ch_ref[...] * hh_ref[...] * keep
    yc_ref[...] = b_ref[...] * _causal_conv(u, uh, cw_ref)

    def dn_branch(x_ref, xh_ref, w_ref):
        y = _causal_conv(x_ref[...], xh_ref[...] * keep, w_ref)
        return y * jax.nn.sigmoid(y)

    qo_ref[...] = _l2norm_heads(dn_branch(q_ref, qh_ref, qw_ref))
    ko_ref[...] = _l2norm_heads(dn_branch(k_ref, kh_ref, kw_ref))
    vo_ref[...] = dn_branch(v_ref, vh_ref, vw_ref)

    tail = t_ref[...]
    beta = jax.nn.sigmoid(tail)
    g = -jnp.exp(alog_ref[...]) * jax.nn.softplus(tail + dtb_ref[...])
    gcum = _hdot(tril_ref[...], g)
    lane = lax.broadcasted_iota(jnp.int32, tail.shape, 1)
    gb_ref[...] = jnp.where(lane < DN_HEADS, beta, gcum)


def _mixer_pre(proj, conv_w, dn_conv_w, dn_a_log, dn_dt_bias, batch, seq, ts):
    t = proj.shape[0]
    cw = 512
    nblk = seq // ts
    hb = ts // SUBLANES

    def cur(col):
        return pl.BlockSpec((ts, cw), lambda b, i, col=col: (b * nblk + i, col))

    def halo(col):
        return pl.BlockSpec(
            (SUBLANES, cw),
            lambda b, i, col=col: (jnp.maximum((b * nblk + i) * hb - 1, 0), col))

    def full(shape):
        return pl.BlockSpec(shape, lambda b, i: (0,) * len(shape))

    qw, kw, vw = (dn_conv_w[:, j * cw:(j + 1) * cw] for j in range(3))
    lane_pad = jnp.zeros((LANES - 2 * DN_HEADS,), F32)
    alog = jnp.concatenate([jnp.zeros((DN_HEADS,), F32), dn_a_log, lane_pad]).reshape(1, LANES)
    dtb = jnp.concatenate([jnp.zeros((DN_HEADS,), F32), dn_dt_bias, lane_pad]).reshape(1, LANES)
    r = jnp.arange(ts)
    tril = ((r[:, None] >= r[None, :]) &
            (r[:, None] // DN_CHUNK == r[None, :] // DN_CHUNK)).astype(F32)

    tail_spec = pl.BlockSpec((ts, LANES), lambda b, i: (b * nblk + i, 7 * cw // LANES))
    out_tok = lambda w: pl.BlockSpec((ts, w), lambda b, i: (b * nblk + i, 0))
    return pl.pallas_call(
        _mixer_pre_body,
        grid=(batch, nblk),
        in_specs=[cur(0), cur(1), cur(2), cur(3), cur(4), cur(5), tail_spec,
                  halo(1), halo(2), halo(3), halo(4), halo(5),
                  full((CONV_GROUP_K, cw)), full((DN_CONV_K, cw)), full((DN_CONV_K, cw)),
                  full((DN_CONV_K, cw)), full((1, LANES)), full((1, LANES)), full((ts, ts))],
        out_specs=[out_tok(cw), out_tok(cw), out_tok(cw), out_tok(cw), out_tok(LANES)],
        out_shape=[jax.ShapeDtypeStruct((t, cw), F32)] * 4 + [jax.ShapeDtypeStruct((t, LANES), F32)],
        compiler_params=_params("parallel", "parallel"),
        name="mixer_pre",
    )(proj, proj, proj, proj, proj, proj, proj, proj, proj, proj, proj, proj,
      conv_w, qw, kw, vw, alog, dtb, tril)


_NN = (((1,), (0,)), ((), ()))
_NT = (((1,), (1,)), ((), ()))
_TN = (((0,), (0,)), ((), ()))
DN_LOCAL_PASSES = 1
DN_SCAN_PASSES = 1


def _mm(a, b, dims, passes):
    dot = lambda x, y: lax.dot_general(x, y, dims, preferred_element_type=F32)
    if passes == 6:
        return lax.dot_general(a, b, dims, preferred_element_type=F32, precision=HIGHEST)
    a_hi = a.astype(BF16)
    b_hi = b.astype(BF16)
    if passes == 1:
        return dot(a_hi, b_hi)
    a_lo = (a - a_hi.astype(F32)).astype(BF16)
    b_lo = (b - b_hi.astype(F32)).astype(BF16)
    return dot(a_hi, b_hi) + (dot(a_hi, b_lo) + dot(a_lo, b_hi))


DN_LOCAL_CHUNKS = 4


def _dn_local_body(q_ref, k_ref, v_ref, gb_ref, u_ref, w_ref, qd_ref, kd_ref, in_ref, gl_ref):
    c = DN_CHUNK
    n = DN_HEADS * c
    p = DN_LOCAL_PASSES
    cis = range(q_ref.shape[0] // c)
    each = lambda f, *xs: [f(*a) for a in zip(*xs)]
    lanes = lambda x: jnp.broadcast_to(x, (x.shape[0], LANES))

    def stack(ref, ci):
        return jnp.concatenate([ref[ci * c:(ci + 1) * c, h * DN_HEAD_DIM:(h + 1) * DN_HEAD_DIM]
                                for h in range(DN_HEADS)], axis=0)

    def col(gb, j):
        return jnp.concatenate([lanes(gb[:, j + h:j + h + 1]) for h in range(DN_HEADS)], axis=0)

    gb = [gb_ref[ci * c:(ci + 1) * c, :] for ci in cis]
    q = [stack(q_ref, ci) * (DN_HEAD_DIM ** -0.5) for ci in cis]
    k = [stack(k_ref, ci) for ci in cis]
    v = [stack(v_ref, ci) for ci in cis]
    beta = [col(x, 0) for x in gb]
    g = [col(x, DN_HEADS) for x in gb]
    g_last = [jnp.concatenate(
        [jnp.broadcast_to(x[c - 1:c, DN_HEADS + h:DN_HEADS + h + 1], (c, LANES))
         for h in range(DN_HEADS)], axis=0) for x in gb]

    row = lax.broadcasted_iota(jnp.int32, (n, n), 0)
    cl = lax.broadcasted_iota(jnp.int32, (n, n), 1)
    shift = c.bit_length() - 1
    same_head = (row >> shift) == (cl >> shift)
    causal = same_head & (row >= cl)
    strict = same_head & (row > cl)
    eye = (row == cl).astype(F32)
    decay = [jnp.where(causal, jnp.exp(jnp.where(causal, x[:, 0:1] - x.T[0:1, :], 0.0)), 0.0)
             for x in g]
    k_beta = each(lambda a, b: a * b, k, beta)
    v_beta = each(lambda a, b: a * b, v, beta)
    a = each(lambda kb, kk, d: jnp.where(strict, _mm(kb, kk, _NT, p) * d, 0.0), k_beta, k, decay)

    t_mat = [eye - x for x in a]
    pw = [_mm(x, x, _NN, p) for x in a]
    for _ in range(4):
        t_mat = each(lambda t, w_: t + _mm(t, w_, _NN, p), t_mat, pw)
        pw = [_mm(x, x, _NN, p) for x in pw]
    t_mat = each(lambda t, w_: t + _mm(t, w_, _NN, p), t_mat, pw)

    e_g = [jnp.exp(x) for x in g]
    uw = each(lambda t, vb, kb, e: _mm(t, jnp.concatenate([vb, kb * e], axis=1), _NN, p),
              t_mat, v_beta, k_beta, e_g)
    intra = each(lambda qq, kk, d: _mm(qq, kk, _NT, p) * d, q, k, decay)
    for ci in cis:
        u_ref[ci] = uw[ci][:, :DN_HEAD_DIM]
        w_ref[ci] = uw[ci][:, DN_HEAD_DIM:]
        in_ref[ci] = intra[ci]
        qd_ref[ci] = q[ci] * e_g[ci]
        kd_ref[ci] = k[ci] * jnp.exp(g_last[ci] - g[ci])
        gl_ref[ci] = jnp.concatenate(
            [jnp.exp(g_last[ci][h * c:h * c + 1]) for h in range(DN_HEADS)] +
            [jnp.zeros((SUBLANES - DN_HEADS, LANES), F32)], axis=0)


def _dn_scan_body(u_ref, w_ref, qd_ref, kd_ref, in_ref, gl_ref, o_ref, state_ref):
    c = DN_CHUNK
    p = DN_SCAN_PASSES

    @pl.when(pl.program_id(0) == 0)
    def _():
        state_ref[...] = jnp.zeros_like(state_ref)

    nb = range(u_ref.shape[0])
    heads = range(DN_HEADS)
    rows = [slice(h * c, (h + 1) * c) for h in heads]
    ws = [[_mm(jnp.concatenate([w_ref[b, 0][rows[h]], qd_ref[b, 0][rows[h]]], axis=0),
               state_ref[b, h], _NN, p) for h in heads] for b in nb]
    v_new = [[u_ref[b, 0][rows[h]] - ws[b][h][:c] for h in heads] for b in nb]
    o = [jnp.concatenate([ws[b][h][c:] for h in heads], axis=0) +
         _mm(in_ref[b, 0], jnp.concatenate(v_new[b], axis=0), _NN, p) for b in nb]
    for b in nb:
        for h in heads:
            state_ref[b, h] = (state_ref[b, h] * gl_ref[b, 0][h:h + 1] +
                               _mm(kd_ref[b, 0][rows[h]], v_new[b][h], _TN, p))
    for b in nb:
        o_ref[b] = jnp.concatenate([o[b][rows[h]] for h in heads], axis=1)


def _delta_rule(q, k, v, gb, batch, seq):
    t, w = q.shape
    nc = seq // DN_CHUNK
    n = DN_HEADS * DN_CHUNK
    cps = _pick_tile(batch * nc, DN_LOCAL_CHUNKS)
    tok = lambda width: pl.BlockSpec((cps * DN_CHUNK, width), lambda i: (i, 0))
    per_chunk = lambda rows, width: pl.BlockSpec((cps, rows, width), lambda i: (i, 0, 0))
    f = lambda rows, width: jax.ShapeDtypeStruct((batch * nc, rows, width), F32)
    u, wm, qd, kd, intra, gl = pl.pallas_call(
        _dn_local_body,
        grid=(batch * nc // cps,),
        in_specs=[tok(w), tok(w), tok(w), tok(LANES)],
        out_specs=[per_chunk(n, DN_HEAD_DIM)] * 4 + [per_chunk(n, n), per_chunk(SUBLANES, LANES)],
        out_shape=[f(n, DN_HEAD_DIM)] * 4 + [f(n, n), f(SUBLANES, LANES)],
        compiler_params=_params("parallel"),
        name="dn_local",
    )(q, k, v, gb)

    seq_blk = lambda rows, width: pl.BlockSpec((batch, 1, rows, width), lambda i: (0, i, 0, 0))
    by_batch = lambda x: x.reshape(batch, nc, *x.shape[1:])
    o = pl.pallas_call(
        _dn_scan_body,
        grid=(nc,),
        in_specs=[seq_blk(n, DN_HEAD_DIM)] * 4 + [seq_blk(n, n), seq_blk(SUBLANES, LANES)],
        out_specs=pl.BlockSpec((batch, DN_CHUNK, w), lambda i: (0, i, 0)),
        out_shape=jax.ShapeDtypeStruct((batch, seq, w), F32),
        scratch_shapes=[pltpu.VMEM((batch, DN_HEADS, DN_HEAD_DIM, DN_HEAD_DIM), F32)],
        compiler_params=_params("arbitrary"),
        name="dn_scan",
    )(by_batch(u), by_batch(wm), by_batch(qd), by_batch(kd), by_batch(intra), by_batch(gl))
    return o.reshape(t, w)


def _mixer_out_body(x_ref, yc_ref, o_ref, z_ref, dng_ref, wa_ref, wb_ref, h_ref):
    o = o_ref[...]
    z = z_ref[...]
    gain = dng_ref[...]
    parts = []
    for h in range(DN_HEADS):
        sl = slice(h * DN_HEAD_DIM, (h + 1) * DN_HEAD_DIM)
        zh = z[:, sl]
        parts.append(_rms(o[:, sl], gain) * (zh * jax.nn.sigmoid(zh)))
    y_dn = jnp.concatenate(parts, axis=-1)
    h_ref[...] = (x_ref[...] + _bdot(yc_ref[...], wa_ref[...]) + _bdot(y_dn, wb_ref[...]))


def _mixer_out(x, y_conv, o_dn, proj, dn_norm, w_out, tm):
    t, d = x.shape
    cw = y_conv.shape[1]
    tok = lambda width, col=0: pl.BlockSpec((tm, width), lambda i, col=col: (i, col))
    full = lambda shape: pl.BlockSpec(shape, lambda i: (0, 0))
    return pl.pallas_call(
        _mixer_out_body,
        grid=(t // tm,),
        in_specs=[tok(d), tok(cw), tok(cw), tok(cw, 6), full((1, DN_HEAD_DIM)),
                  full((cw, d)), full((cw, d))],
        out_specs=tok(d),
        out_shape=jax.ShapeDtypeStruct((t, d), F32),
        compiler_params=_params("parallel"),
        name="mixer_out",
    )(x, y_conv, o_dn, proj, dn_norm.reshape(1, DN_HEAD_DIM),
      w_out[:cw].astype(BF16), w_out[cw:].astype(BF16))


def _cross_attn_body(h_ref, g_ref, wq_ref, kv_ref, wo_ref, o_ref):
    h_in = h_ref[...]
    d = h_in.shape[1]
    dh = d // X_HEADS
    q = _bdot(_rms(h_in, g_ref[...]), wq_ref[...])
    outs = []
    for hd in range(X_HEADS):
        qh = q[:, hd * dh:(hd + 1) * dh].astype(BF16)
        kh = kv_ref[:, hd * dh:(hd + 1) * dh]
        vh = kv_ref[:, d + hd * dh:d + (hd + 1) * dh]
        s = lax.dot_general(qh, kh, (((1,), (1,)), ((), ())),
                            preferred_element_type=F32) * (dh ** -0.5)
        s = s - jnp.max(s, axis=-1, keepdims=True)
        e = jnp.exp(s)
        p = e / jnp.sum(e, axis=-1, keepdims=True)
        outs.append(jnp.dot(p.astype(BF16), vh, preferred_element_type=F32))
    o = jnp.concatenate(outs, axis=-1)
    o_ref[...] = h_in + _bdot(o, wo_ref[...])


def _cross_attn(h, gain, w_q, kv, w_o, batch, seq, mem_len, tm):
    t, d = h.shape
    nblk = seq // tm
    tok = pl.BlockSpec((tm, d), lambda b, i: (b * nblk + i, 0))
    full = lambda shape: pl.BlockSpec(shape, lambda b, i: (0, 0))
    return pl.pallas_call(
        _cross_attn_body,
        grid=(batch, nblk),
        in_specs=[tok, full((1, d)), full((d, d)),
                  pl.BlockSpec((mem_len, 2 * d), lambda b, i: (b, 0)), full((d, d))],
        out_specs=tok,
        out_shape=jax.ShapeDtypeStruct((t, d), F32),
        compiler_params=_params("parallel", "parallel"),
        name="cross_attn",
    )(h, gain.reshape(1, d), w_q.astype(BF16), kv.astype(BF16), w_o.astype(BF16))


def _top16_rows(s, payload=None):
    n = s.shape[0]
    row = lax.broadcasted_iota(jnp.int32, s.shape, 0).astype(F32)
    vals, pays = [], []
    for _ in range(PEER_TOPK):
        m = jnp.max(s, axis=0, keepdims=True)
        pos = jnp.min(jnp.where(s == m, row, float(n)), axis=0, keepdims=True)
        hit = row == pos
        vals.append(m)
        if payload is None:
            pays.append(pos)
        else:
            pays.append(jnp.sum(jnp.where(hit, payload, 0.0), axis=0, keepdims=True))
        s = jnp.where(hit, -jnp.inf, s)
    return jnp.concatenate(vals, axis=0), jnp.concatenate(pays, axis=0)


def _pair_candidates(v1, v2, pad):
    sub = lax.broadcasted_iota(jnp.int32, (SUBLANES, v1.shape[1]), 0)
    row = lambda v, a: jnp.broadcast_to(v[a:a + 1], sub.shape)
    lo8 = v2[:SUBLANES]
    blocks = [
        row(v1, 0) + lo8,
        row(v1, 0) + v2[SUBLANES:],
        row(v1, 1) + lo8,
        jnp.where(sub < 5, row(v1, 2) + lo8, pad),
        jnp.where(sub < 7,
                  jnp.where(sub < 4, row(v1, 3), row(v1, 4)) +
                  jnp.where(sub < 4, lo8, pltpu.roll(lo8, 4, 0)), pad),
        jnp.where(sub < 6,
                  jnp.where(sub < 2, row(v1, 5), jnp.where(sub < 4, row(v1, 6), row(v1, 7))) +
                  jnp.where((sub & 1) == 0, row(v2, 0), row(v2, 1)), pad),
        v1[SUBLANES:] + row(v2, 0),
    ]
    return jnp.concatenate(blocks, axis=0)


def _peer_route_body(h_ref, g_ref, wq_ref, keys_ref, xn_ref, gate_ref, *rest):
    idx_refs = rest[:PEER_HEADS]
    sc_ref, sel_ref, gt_ref, it_ref = rest[PEER_HEADS:]
    xn = _rms(h_ref[...], g_ref[...])
    for r in range(SUBLANES):
        xn_ref[:, r] = xn[:, r * LANES:(r + 1) * LANES].reshape(-1, SUBLANES, LANES)
    pq = _bdot(xn, wq_ref[...])
    ngrp = pq.shape[0] // LANES
    for j in range(2 * PEER_HEADS):
        qs = pq[:, j * PEER_HALF:(j + 1) * PEER_HALF].astype(BF16)
        sc = lax.dot_general(keys_ref[j], qs, (((1,), (1,)), ((), ())),
                             preferred_element_type=F32)
        for g in range(ngrp):
            sc_ref[j, g] = sc[:, g * LANES:(g + 1) * LANES]

    def head(h, carry):
        for g in range(ngrp):
            s1, i1 = _top16_rows(sc_ref[2 * h, g])
            s2, i2 = _top16_rows(sc_ref[2 * h + 1, g])
            cand = _pair_candidates(s1, s2, -jnp.inf)
            cand_idx = _pair_candidates(i1 * float(PEER_KEYS), i2, 0.0)
            best, sel = _top16_rows(cand, cand_idx)
            e = jnp.exp(best - jnp.max(best, axis=0, keepdims=True))
            sel_ref[g, h] = sel
            gt_ref[g, h] = e / jnp.sum(e, axis=0, keepdims=True)
        return carry

    lax.fori_loop(0, PEER_HEADS, head, 0)

    nsel = PEER_HEADS * PEER_TOPK
    per_row = LANES // PEER_TOPK
    lane_blk = lax.broadcasted_iota(jnp.int32, (LANES // per_row, LANES), 1) // PEER_TOPK
    for g in range(ngrp):
        rows = slice(g * LANES, (g + 1) * LANES)
        gate_ref[rows, :] = gt_ref[g].reshape(nsel, LANES).T
        it_ref[...] = (sel_ref[g].reshape(nsel, LANES).T * float(PACK_ROWS)).astype(jnp.int32)
        by_k = [it_ref[pl.ds(k, LANES // per_row, stride=per_row), :] for k in range(per_row)]
        out_rows = slice(g * LANES // per_row, (g + 1) * LANES // per_row)
        for hd in range(PEER_HEADS):
            out = jnp.zeros_like(by_k[0])
            for k in range(per_row):
                shift = ((k - hd) * PEER_TOPK) % LANES
                moved = pltpu.roll(by_k[k], shift, 1) if shift else by_k[k]
                out = jnp.where(lane_blk == k, moved, out)
            idx_refs[hd][out_rows, :] = out


def _peer_route(h, gain, w_pq, peer_keys, tm):
    t, d = h.shape
    nk = PEER_HEADS * 2
    ngrp = tm // LANES
    tok = lambda width: pl.BlockSpec((tm, width), lambda i: (i, 0))
    full = lambda shape: pl.BlockSpec(shape, lambda i: (0,) * len(shape))
    idx_rows = tm * PEER_TOPK // LANES
    outs = pl.pallas_call(
        _peer_route_body,
        grid=(t // tm,),
        in_specs=[tok(d), full((1, d)), full((d, d)), full((nk, PEER_KEYS, PEER_HALF))],
        out_specs=[pl.BlockSpec((tm // SUBLANES, SUBLANES, SUBLANES, LANES), lambda i: (i, 0, 0, 0)),
                   tok(LANES)] + [pl.BlockSpec((idx_rows, LANES), lambda i: (i, 0))] * PEER_HEADS,
        out_shape=[jax.ShapeDtypeStruct((t // SUBLANES, SUBLANES, SUBLANES, LANES), F32),
                   jax.ShapeDtypeStruct((t, LANES), F32)] +
                  [jax.ShapeDtypeStruct((t * PEER_TOPK // LANES, LANES), jnp.int32)] * PEER_HEADS,
        scratch_shapes=[pltpu.VMEM((nk, ngrp, PEER_KEYS, LANES), F32),
                        pltpu.VMEM((ngrp, PEER_HEADS, PEER_TOPK, LANES), F32),
                        pltpu.VMEM((ngrp, PEER_HEADS, PEER_TOPK, LANES), F32),
                        pltpu.VMEM((LANES, LANES), jnp.int32)],
        compiler_params=_params("parallel"),
        name="peer_route",
    )(h, gain.reshape(1, d), w_pq.astype(BF16),
      peer_keys.reshape(nk, PEER_KEYS, PEER_HALF).astype(BF16))
    xr, gates = outs[0], outs[1]
    return xr, tuple(o.reshape(-1) for o in outs[2:]), gates


PACK_ROWS = 4


def _pack_table_body(t_ref, o_ref):
    rows, d = t_ref.shape
    half = d // 2
    bits = lambda x: pltpu.bitcast(x.astype(BF16).astype(F32), jnp.uint32)
    word = (bits(t_ref[:, half:]) & jnp.uint32(0xFFFF0000)) | (bits(t_ref[:, :half]) >> 16)
    for j in range(PACK_ROWS):
        o_ref[pl.ds(j, rows, stride=PACK_ROWS), :] = word[:, j * LANES:(j + 1) * LANES]


def _pack_table(tbl):
    e, d = tbl.shape
    rows = _pick_tile(e, 512)
    return pl.pallas_call(
        _pack_table_body,
        grid=(e // rows,),
        in_specs=[pl.BlockSpec((rows, d), lambda i: (i, 0))],
        out_specs=pl.BlockSpec((rows * PACK_ROWS, LANES), lambda i: (i, 0)),
        out_shape=jax.ShapeDtypeStruct((e * PACK_ROWS, LANES), jnp.uint32),
        compiler_params=_params("parallel"),
        name="pack_table",
    )(tbl)


def _unpack(slab):
    lo = pltpu.bitcast(slab << 16, F32)
    hi = pltpu.bitcast(slab & jnp.uint32(0xFFFF0000), F32)
    return lo, hi


STAGE_STEPS = 4
PEER_SEL = PEER_HEADS * PEER_TOPK
IDX_REFS = PEER_HEADS


def _idx_specs(tb):
    n = tb * PEER_SEL // IDX_REFS
    return [pl.BlockSpec((n,), lambda i: (i,), memory_space=pltpu.SMEM) for _ in range(IDX_REFS)]


def _stage_rows(idx_refs, t, tbl_ref, g_ref, step=None):
    ncol = PEER_SEL // IDX_REFS
    per = ncol // STAGE_STEPS
    qs = range(ncol) if step is None else range(step * per, (step + 1) * per)
    for q in qs:
        off = t * ncol + q
        for r in range(IDX_REFS):
            m = r * ncol + q
            row = pl.multiple_of(idx_refs[r][off], PACK_ROWS)
            g_ref[m * PACK_ROWS:(m + 1) * PACK_ROWS, :] = tbl_ref[pl.ds(row, PACK_ROWS), :]


def _bf16_pieces(x, n):
    pieces = []
    for _ in range(n - 1):
        p = x.astype(BF16).astype(F32)
        pieces.append(p)
        x = x - p
    pieces.append(x.astype(BF16).astype(F32))
    return pieces


def _staged_group(idx_ref, tbl_ref, bufs, tb, base, step_fn):
    for k in range(SUBLANES):
        t = base + k
        nxt = t + 1 if k + 1 < SUBLANES else jnp.minimum(t + 1, tb - 1)
        for step in range(STAGE_STEPS):
            _stage_rows(idx_ref, nxt, tbl_ref, bufs[(k + 1) % 2], step)
            step_fn(t, k, step, bufs[k % 2])


def _peer_dot_body(*refs):
    idx_ref = refs[:IDX_REFS]
    x_ref, tbl_ref, act_ref, g0_ref, g1_ref = refs[IDX_REFS:]
    tb, nsel = act_ref.shape
    nslab = SUBLANES * SUBLANES
    lane = lax.broadcasted_iota(jnp.int32, (LANES, LANES), 1)
    col_slab = lane & (nslab - 1)
    _stage_rows(idx_ref, 0, tbl_ref, g0_ref)

    def group(i, carry):
        base = pl.multiple_of(i * SUBLANES, SUBLANES)
        pieces = _bf16_pieces(x_ref[i].reshape(nslab, LANES), LANES // nslab)
        xt = jnp.concatenate(pieces, axis=0).T.astype(BF16)
        zero = jnp.zeros_like(xt)
        acc = [jnp.zeros((nsel, LANES), F32)]

        def step_fn(t, k, j, g_ref):
            lo, hi = _unpack(g_ref[pl.ds(j, nsel, stride=PACK_ROWS), :])
            lhs = jnp.concatenate([lo.astype(BF16), hi.astype(BF16)], axis=1)
            rhs = jnp.concatenate(
                [jnp.where(col_slab == SUBLANES * r + k, xt, zero) for r in (j, PACK_ROWS + j)],
                axis=0)
            acc[0] = acc[0] + jnp.dot(lhs, rhs, preferred_element_type=F32)

        _staged_group(idx_ref, tbl_ref, (g0_ref, g1_ref), tb, base, step_fn)
        out_t = acc[0].T
        act_ref[pl.ds(base, SUBLANES), :] = jnp.sum(
            out_t.reshape(LANES // SUBLANES, SUBLANES, nsel), axis=0)
        return carry

    lax.fori_loop(0, tb // SUBLANES, group, 0)


def _peer_dot(idx_split, xr, tbl, tb):
    t = xr.shape[0] * SUBLANES
    nsel = PEER_SEL
    gbuf = pltpu.VMEM((nsel * PACK_ROWS, LANES), jnp.uint32)
    return pl.pallas_call(
        _peer_dot_body,
        grid=(t // tb,),
        in_specs=_idx_specs(tb) + [
            pl.BlockSpec((tb // SUBLANES, SUBLANES, SUBLANES, LANES), lambda i: (i, 0, 0, 0)),
            pl.BlockSpec(tbl.shape, lambda i: (0, 0), pipeline_mode=pl.Buffered(1))],
        out_specs=pl.BlockSpec((tb, nsel), lambda i: (i, 0)),
        out_shape=jax.ShapeDtypeStruct((t, nsel), F32),
        scratch_shapes=[gbuf, gbuf],
        compiler_params=_params("arbitrary"),
        name="peer_dot",
    )(*idx_split, xr, tbl)


def _peer_weights_body(act_ref, gate_ref, rep_ref, w_ref):
    a = act_ref[...]
    w = gate_ref[...] * (0.5 * a * (1.0 + lax.erf(a * (2.0 ** -0.5))))
    rep = rep_ref[...]
    w_ref[...] = sum(jnp.dot(p.astype(BF16), rep, preferred_element_type=F32)
                     for p in _bf16_pieces(w, 3))


def _peer_weights(act, gates, tm):
    t, n = act.shape
    nrow = 2 * PACK_ROWS * n
    rep = (jnp.arange(n)[:, None] == (jnp.arange(nrow) // (2 * PACK_ROWS))[None, :]).astype(BF16)
    tok = lambda width: pl.BlockSpec((tm, width), lambda i: (i, 0))
    return pl.pallas_call(
        _peer_weights_body,
        grid=(t // tm,),
        in_specs=[tok(n), tok(n), pl.BlockSpec(rep.shape, lambda i: (0, 0))],
        out_specs=tok(nrow),
        out_shape=jax.ShapeDtypeStruct((t, nrow), F32),
        compiler_params=_params("parallel"),
        name="peer_weights",
    )(act, gates, rep)


def _peer_sum_body(has_gain, *refs):
    idx_ref = refs[:IDX_REFS]
    w_ref, h_ref, mask_ref = refs[IDX_REFS:IDX_REFS + 3]
    gain_ref = refs[IDX_REFS + 3] if has_gain else None
    tbl_ref, y_ref, g0_ref, g1_ref = refs[IDX_REFS + 3 + has_gain:]
    tb = y_ref.shape[0]
    mask = mask_ref[...]
    nrow = g0_ref.shape[0] // STAGE_STEPS
    ncol = mask.shape[1] // STAGE_STEPS
    _stage_rows(idx_ref, 0, tbl_ref, g0_ref)

    def finish(i, tiles):
        base = pl.multiple_of(i * SUBLANES, SUBLANES)
        slabs = []
        for r in range(SUBLANES):
            y_r = jnp.concatenate([tile[r:r + 1] for tile in tiles], axis=0)
            slabs.append(h_ref[pl.ds(base, SUBLANES), r * LANES:(r + 1) * LANES] + y_r)
        if gain_ref is not None:
            ms = sum(jnp.sum(s * s, axis=1, keepdims=True) for s in slabs)
            inv = lax.rsqrt(ms * (1.0 / (SUBLANES * LANES)) + EPS)
            slabs = [s * inv * gain_ref[:, r * LANES:(r + 1) * LANES]
                     for r, s in enumerate(slabs)]
        for r, s in enumerate(slabs):
            y_ref[pl.ds(base, SUBLANES), r * LANES:(r + 1) * LANES] = s

    def group(i, prev_rows):
        finish(jnp.maximum(i - 1, 0), prev_rows)
        base = pl.multiple_of(i * SUBLANES, SUBLANES)
        w_rep = w_ref[pl.ds(base, SUBLANES), :]
        acc = [None]

        def step_fn(t, k, step, g_ref):
            cols = slice(step * ncol, (step + 1) * ncol)
            wb = jnp.broadcast_to(w_rep[k:k + 1, cols], (SUBLANES, ncol)) * mask[:, cols]
            a_hi = wb.astype(BF16)
            a_lo = (wb - a_hi.astype(F32)).astype(BF16)
            lhs = jnp.concatenate([a_hi, a_lo], axis=0)
            staged = pltpu.bitcast(g_ref[step * nrow:(step + 1) * nrow, :], BF16)
            part = jnp.dot(lhs, staged, preferred_element_type=F32)
            acc[0] = part if step == 0 else acc[0] + part
            if step == STAGE_STEPS - 1:
                rows.append(acc[0][:SUBLANES] + acc[0][SUBLANES:])

        rows = []
        _staged_group(idx_ref, tbl_ref, (g0_ref, g1_ref), tb, base, step_fn)
        return tuple(rows)

    ngroup = tb // SUBLANES
    zeros = tuple(jnp.zeros((SUBLANES, LANES), F32) for _ in range(SUBLANES))
    finish(ngroup - 1, lax.fori_loop(0, ngroup, group, zeros))


def _peer_sum(idx_split, w, h, gain, tbl, tb):
    t, nrow = w.shape
    d = h.shape[1]
    nsel = PEER_SEL
    tile = pl.BlockSpec((tb, d), lambda i: (i, 0))
    extra = [] if gain is None else [gain.reshape(1, d)]
    c = jnp.arange(nrow)
    out_row = (c % 2) * PACK_ROWS + (c % (2 * PACK_ROWS)) // 2
    mask = (jnp.arange(SUBLANES)[:, None] == out_row[None, :]).astype(F32)
    gbuf = pltpu.VMEM((nsel * PACK_ROWS, LANES), jnp.uint32)
    const = lambda shape: pl.BlockSpec(shape, lambda i: (0, 0))
    return pl.pallas_call(
        functools.partial(_peer_sum_body, len(extra)),
        grid=(t // tb,),
        in_specs=_idx_specs(tb) + [
            pl.BlockSpec((tb, nrow), lambda i: (i, 0)),
            tile,
            const(mask.shape)] + [const((1, d))] * len(extra) + [
            pl.BlockSpec(tbl.shape, lambda i: (0, 0), pipeline_mode=pl.Buffered(1))],
        out_specs=tile,
        out_shape=jax.ShapeDtypeStruct((t, d), F32),
        scratch_shapes=[gbuf, gbuf],
        compiler_params=_params("arbitrary"),
        name="peer_sum",
    )(*idx_split, w, h, mask, *extra, tbl)


def _pick_tile(n, pref):
    tile = min(n, pref)
    while n % tile:
        tile //= 2
    return tile


def _layer(h, mem, batch, seq, norm_mix, w_in, conv_w, dn_conv_w, dn_a_log, dn_dt_bias, dn_norm,
           w_out, norm_x, norm_mem, w_xq, w_xkv, w_xo, norm_ffn, w_pq, peer_keys,
           expert_u, expert_v, final_gain):
    t, d = h.shape
    mem_len = mem.shape[0] // batch
    in_cols = w_in.shape[1]
    pad = (-in_cols) % LANES
    w_in_p = jnp.pad(w_in, ((0, 0), (0, pad))).astype(BF16)

    proj = _norm_matmul(h, norm_mix, w_in_p, _pick_tile(t, 256), "in_proj")
    y_conv, q, k, v, gb = _mixer_pre(proj, conv_w, dn_conv_w, dn_a_log, dn_dt_bias,
                                     batch, seq, _pick_tile(seq, 256))
    o_dn = _delta_rule(q, k, v, gb, batch, seq)
    h = _mixer_out(h, y_conv, o_dn, proj, dn_norm, w_out, _pick_tile(t, 512))

    kv = _norm_matmul(mem, norm_mem, w_xkv.astype(BF16), _pick_tile(mem.shape[0], 256), "kv_proj")
    h = _cross_attn(h, norm_x, w_xq, kv, w_xo, batch, seq, mem_len, _pick_tile(seq, 512))

    xr, idx_split, gates = _peer_route(h, norm_ffn, w_pq, peer_keys, _pick_tile(t, 512))
    tb = _pick_tile(t, 256)
    act = _peer_dot(idx_split, xr, _pack_table(expert_u), tb)
    w = _peer_weights(act, gates, _pick_tile(t, 1024))
    return _peer_sum(idx_split, w, h, final_gain, _pack_table(expert_v), tb)


def kernel(x, mem, norm_mix, w_in, conv_w, dn_conv_w, dn_a_log, dn_dt_bias, dn_norm, w_out,
           norm_x, norm_mem, w_xq, w_xkv, w_xo, norm_ffn, w_pq, peer_keys, expert_u, expert_v,
           norm_final):
    batch, seq, d = x.shape
    depth = norm_mix.shape[0]
    h = x.reshape(batch * seq, d)
    mem2 = mem.reshape(batch * mem.shape[1], d)
    for layer in range(depth):
        h = _layer(h, mem2, batch, seq, norm_mix[layer], w_in[layer], conv_w[layer],
                   dn_conv_w[layer], dn_a_log[layer], dn_dt_bias[layer], dn_norm[layer],
                   w_out[layer], norm_x[layer], norm_mem[layer], w_xq[layer], w_xkv[layer],
                   w_xo[layer], norm_ffn[layer], w_pq[layer], peer_keys[layer],
                   expert_u[layer], expert_v[layer],
                   norm_final if layer == depth - 1 else None)
    return h.reshape(batch, seq, d)
```

```python
import functools

import jax
import jax.numpy as jnp
from jax import lax
from jax.experimental import pallas as pl
from jax.experimental.pallas import tpu as pltpu

EPS = 1e-6
F32 = jnp.float32
BF16 = jnp.bfloat16
HIGHEST = lax.Precision.HIGHEST

LANES = 128
SUBLANES = 8
VMEM_LIMIT = 56 * 1024 * 1024

CONV_GROUP_K = 3
DN_CONV_K = 4
DN_HEADS = 4
DN_HEAD_DIM = 128
DN_CHUNK = 64
X_HEADS = 4
PEER_HEADS = 8
PEER_KEYS = 128
PEER_TOPK = 16
PEER_HALF = 64


def _params(*sem):
    return pltpu.CompilerParams(dimension_semantics=sem, vmem_limit_bytes=VMEM_LIMIT)


def _rms(x, gain):
    ms = jnp.mean(x * x, axis=-1, keepdims=True)
    return x * lax.rsqrt(ms + EPS) * gain


def _bdot(a, b):
    return jnp.dot(a.astype(BF16), b.astype(BF16), preferred_element_type=F32)


def _hdot(a, b):
    return jnp.dot(a, b, preferred_element_type=F32, precision=HIGHEST)


def _norm_matmul_body(x_ref, g_ref, w_ref, o_ref):
    xn = _rms(x_ref[...], g_ref[...])
    o_ref[...] = jnp.dot(xn.astype(BF16), w_ref[...], preferred_element_type=F32)


def _norm_matmul(x, gain, w, tm, name):
    t, d = x.shape
    n = w.shape[1]
    return pl.pallas_call(
        _norm_matmul_body,
        grid=(t // tm,),
        in_specs=[pl.BlockSpec((tm, d), lambda i: (i, 0)),
                  pl.BlockSpec((1, d), lambda i: (0, 0)),
                  pl.BlockSpec((d, n), lambda i: (0, 0))],
        out_specs=pl.BlockSpec((tm, n), lambda i: (i, 0)),
        out_shape=jax.ShapeDtypeStruct((t, n), F32),
        compiler_params=_params("parallel"),
        name=name,
    )(x, gain.reshape(1, d), w)


def _shift_rows(x, halo, k):
    rolled = pltpu.roll(x, k, 0)
    hal = pltpu.roll(halo, k, 0)
    row = lax.broadcasted_iota(jnp.int32, (SUBLANES, x.shape[1]), 0)
    head = jnp.where(row < k, hal, rolled[:SUBLANES])
    return jnp.concatenate([head, rolled[SUBLANES:]], axis=0)


def _causal_conv(x, halo, w_ref):
    kk = w_ref.shape[0]
    y = x * w_ref[kk - 1:kk, :]
    for j in range(1, kk):
        y = y + _shift_rows(x, halo, j) * w_ref[kk - 1 - j:kk - j, :]
    return y


def _l2norm_heads(x):
    outs = []
    for h in range(DN_HEADS):
        xh = x[:, h * DN_HEAD_DIM:(h + 1) * DN_HEAD_DIM]
        outs.append(xh * lax.rsqrt(jnp.sum(xh * xh, axis=-1, keepdims=True) + EPS))
    return jnp.concatenate(outs, axis=-1)


def _mixer_pre_body(b_ref, c_ref, h_ref, q_ref, k_ref, v_ref, t_ref,
                    ch_ref, hh_ref, qh_ref, kh_ref, vh_ref,
                    cw_ref, qw_ref, kw_ref, vw_ref, alog_ref, dtb_ref, tril_ref,
                    yc_ref, qo_ref, ko_ref, vo_ref, gb_ref):
    first = pl.program_id(1) == 0
    keep = jnp.where(first, 0.0, 1.0).astype(F32)

    u = c_ref[...] * h_ref[...]
    uh = ch_ref[...] * hh_ref[...] * keep
    yc_ref[...] = b_ref[...] * _causal_conv(u, uh, cw_ref)

    def dn_branch(x_ref, xh_ref, w_ref):
        y = _causal_conv(x_ref[...], xh_ref[...] * keep, w_ref)
        return y * jax.nn.sigmoid(y)

    qo_ref[...] = _l2norm_heads(dn_branch(q_ref, qh_ref, qw_ref))
    ko_ref[...] = _l2norm_heads(dn_branch(k_ref, kh_ref, kw_ref))
    vo_ref[...] = dn_branch(v_ref, vh_ref, vw_ref)

    tail = t_ref[...]
    beta = jax.nn.sigmoid(tail)
    g = -jnp.exp(alog_ref[...]) * jax.nn.softplus(tail + dtb_ref[...])
    gcum = _hdot(tril_ref[...], g)
    lane = lax.broadcasted_iota(jnp.int32, tail.shape, 1)
    gb_ref[...] = jnp.where(lane < DN_HEADS, beta, gcum)


def _mixer_pre(proj, conv_w, dn_conv_w, dn_a_log, dn_dt_bias, batch, seq, ts):
    t = proj.shape[0]
    cw = 512
    nblk = seq // ts
    hb = ts // SUBLANES

    def cur(col):
        return pl.BlockSpec((ts, cw), lambda b, i, col=col: (b * nblk + i, col))

    def halo(col):
        return pl.BlockSpec(
            (SUBLANES, cw),
            lambda b, i, col=col: (jnp.maximum((b * nblk + i) * hb - 1, 0), col))

    def full(shape):
        return pl.BlockSpec(shape, lambda b, i: (0,) * len(shape))

    qw, kw, vw = (dn_conv_w[:, j * cw:(j + 1) * cw] for j in range(3))
    lane_pad = jnp.zeros((LANES - 2 * DN_HEADS,), F32)
    alog = jnp.concatenate([jnp.zeros((DN_HEADS,), F32), dn_a_log, lane_pad]).reshape(1, LANES)
    dtb = jnp.concatenate([jnp.zeros((DN_HEADS,), F32), dn_dt_bias, lane_pad]).reshape(1, LANES)
    r = jnp.arange(ts)
    tril = ((r[:, None] >= r[None, :]) &
            (r[:, None] // DN_CHUNK == r[None, :] // DN_CHUNK)).astype(F32)

    tail_spec = pl.BlockSpec((ts, LANES), lambda b, i: (b * nblk + i, 7 * cw // LANES))
    out_tok = lambda w: pl.BlockSpec((ts, w), lambda b, i: (b * nblk + i, 0))
    return pl.pallas_call(
        _mixer_pre_body,
        grid=(batch, nblk),
        in_specs=[cur(0), cur(1), cur(2), cur(3), cur(4), cur(5), tail_spec,
                  halo(1), halo(2), halo(3), halo(4), halo(5),
                  full((CONV_GROUP_K, cw)), full((DN_CONV_K, cw)), full((DN_CONV_K, cw)),
                  full((DN_CONV_K, cw)), full((1, LANES)), full((1, LANES)), full((ts, ts))],
        out_specs=[out_tok(cw), out_tok(cw), out_tok(cw), out_tok(cw), out_tok(LANES)],
        out_shape=[jax.ShapeDtypeStruct((t, cw), F32)] * 4 + [jax.ShapeDtypeStruct((t, LANES), F32)],
        compiler_params=_params("parallel", "parallel"),
        name="mixer_pre",
    )(proj, proj, proj, proj, proj, proj, proj, proj, proj, proj, proj, proj,
      conv_w, qw, kw, vw, alog, dtb, tril)


_NN = (((1,), (0,)), ((), ()))
_NT = (((1,), (1,)), ((), ()))
_TN = (((0,), (0,)), ((), ()))
DN_LOCAL_PASSES = 1
DN_SCAN_PASSES = 1


def _mm(a, b, dims, passes):
    dot = lambda x, y: lax.dot_general(x, y, dims, preferred_element_type=F32)
    if passes == 6:
        return lax.dot_general(a, b, dims, preferred_element_type=F32, precision=HIGHEST)
    a_hi = a.astype(BF16)
    b_hi = b.astype(BF16)
    if passes == 1:
        return dot(a_hi, b_hi)
    a_lo = (a - a_hi.astype(F32)).astype(BF16)
    b_lo = (b - b_hi.astype(F32)).astype(BF16)
    return dot(a_hi, b_hi) + (dot(a_hi, b_lo) + dot(a_lo, b_hi))


DN_LOCAL_CHUNKS = 4


def _dn_local_body(q_ref, k_ref, v_ref, gb_ref, u_ref, w_ref, qd_ref, kd_ref, in_ref, gl_ref):
    c = DN_CHUNK
    n = DN_HEADS * c
    p = DN_LOCAL_PASSES
    cis = range(q_ref.shape[0] // c)
    each = lambda f, *xs: [f(*a) for a in zip(*xs)]
    lanes = lambda x: jnp.broadcast_to(x, (x.shape[0], LANES))

    def stack(ref, ci):
        return jnp.concatenate([ref[ci * c:(ci + 1) * c, h * DN_HEAD_DIM:(h + 1) * DN_HEAD_DIM]
                                for h in range(DN_HEADS)], axis=0)

    def col(gb, j):
        return jnp.concatenate([lanes(gb[:, j + h:j + h + 1]) for h in range(DN_HEADS)], axis=0)

    gb = [gb_ref[ci * c:(ci + 1) * c, :] for ci in cis]
    q = [stack(q_ref, ci) * (DN_HEAD_DIM ** -0.5) for ci in cis]
    k = [stack(k_ref, ci) for ci in cis]
    v = [stack(v_ref, ci) for ci in cis]
    beta = [col(x, 0) for x in gb]
    g = [col(x, DN_HEADS) for x in gb]
    g_last = [jnp.concatenate(
        [jnp.broadcast_to(x[c - 1:c, DN_HEADS + h:DN_HEADS + h + 1], (c, LANES))
         for h in range(DN_HEADS)], axis=0) for x in gb]

    row = lax.broadcasted_iota(jnp.int32, (n, n), 0)
    cl = lax.broadcasted_iota(jnp.int32, (n, n), 1)
    shift = c.bit_length() - 1
    same_head = (row >> shift) == (cl >> shift)
    causal = same_head & (row >= cl)
    strict = same_head & (row > cl)
    eye = (row == cl).astype(F32)
    decay = [jnp.where(causal, jnp.exp(jnp.where(causal, x[:, 0:1] - x.T[0:1, :], 0.0)), 0.0)
             for x in g]
    k_beta = each(lambda a, b: a * b, k, beta)
    v_beta = each(lambda a, b: a * b, v, beta)
    a = each(lambda kb, kk, d: jnp.where(strict, _mm(kb, kk, _NT, p) * d, 0.0), k_beta, k, decay)

    t_mat = [eye - x for x in a]
    pw = [_mm(x, x, _NN, p) for x in a]
    for _ in range(4):
        t_mat = each(lambda t, w_: t + _mm(t, w_, _NN, p), t_mat, pw)
        pw = [_mm(x, x, _NN, p) for x in pw]
    t_mat = each(lambda t, w_: t + _mm(t, w_, _NN, p), t_mat, pw)

    e_g = [jnp.exp(x) for x in g]
    uw = each(lambda t, vb, kb, e: _mm(t, jnp.concatenate([vb, kb * e], axis=1), _NN, p),
              t_mat, v_beta, k_beta, e_g)
    intra = each(lambda qq, kk, d: _mm(qq, kk, _NT, p) * d, q, k, decay)
    for ci in cis:
        u_ref[ci] = uw[ci][:, :DN_HEAD_DIM]
        w_ref[ci] = uw[ci][:, DN_HEAD_DIM:]
        in_ref[ci] = intra[ci]
        qd_ref[ci] = q[ci] * e_g[ci]
        kd_ref[ci] = k[ci] * jnp.exp(g_last[ci] - g[ci])
        gl_ref[ci] = jnp.concatenate(
            [jnp.exp(g_last[ci][h * c:h * c + 1]) for h in range(DN_HEADS)] +
            [jnp.zeros((SUBLANES - DN_HEADS, LANES), F32)], axis=0)


def _dn_scan_body(u_ref, w_ref, qd_ref, kd_ref, in_ref, gl_ref, o_ref, state_ref):
    c = DN_CHUNK
    p = DN_SCAN_PASSES

    @pl.when(pl.program_id(0) == 0)
    def _():
        state_ref[...] = jnp.zeros_like(state_ref)

    nb = range(u_ref.shape[0])
    heads = range(DN_HEADS)
    rows = [slice(h * c, (h + 1) * c) for h in heads]
    ws = [[_mm(jnp.concatenate([w_ref[b, 0][rows[h]], qd_ref[b, 0][rows[h]]], axis=0),
               state_ref[b, h], _NN, p) for h in heads] for b in nb]
    v_new = [[u_ref[b, 0][rows[h]] - ws[b][h][:c] for h in heads] for b in nb]
    o = [jnp.concatenate([ws[b][h][c:] for h in heads], axis=0) +
         _mm(in_ref[b, 0], jnp.concatenate(v_new[b], axis=0), _NN, p) for b in nb]
    for b in nb:
        for h in heads:
            state_ref[b, h] = (state_ref[b, h] * gl_ref[b, 0][h:h + 1] +
                               _mm(kd_ref[b, 0][rows[h]], v_new[b][h], _TN, p))
    for b in nb:
        o_ref[b] = jnp.concatenate([o[b][rows[h]] for h in heads], axis=1)


def _delta_rule(q, k, v, gb, batch, seq):
    t, w = q.shape
    nc = seq // DN_CHUNK
    n = DN_HEADS * DN_CHUNK
    cps = _pick_tile(batch * nc, DN_LOCAL_CHUNKS)
    tok = lambda width: pl.BlockSpec((cps * DN_CHUNK, width), lambda i: (i, 0))
    per_chunk = lambda rows, width: pl.BlockSpec((cps, rows, width), lambda i: (i, 0, 0))
    f = lambda rows, width: jax.ShapeDtypeStruct((batch * nc, rows, width), F32)
    u, wm, qd, kd, intra, gl = pl.pallas_call(
        _dn_local_body,
        grid=(batch * nc // cps,),
        in_specs=[tok(w), tok(w), tok(w), tok(LANES)],
        out_specs=[per_chunk(n, DN_HEAD_DIM)] * 4 + [per_chunk(n, n), per_chunk(SUBLANES, LANES)],
        out_shape=[f(n, DN_HEAD_DIM)] * 4 + [f(n, n), f(SUBLANES, LANES)],
        compiler_params=_params("parallel"),
        name="dn_local",
    )(q, k, v, gb)

    seq_blk = lambda rows, width: pl.BlockSpec((batch, 1, rows, width), lambda i: (0, i, 0, 0))
    by_batch = lambda x: x.reshape(batch, nc, *x.shape[1:])
    o = pl.pallas_call(
        _dn_scan_body,
        grid=(nc,),
        in_specs=[seq_blk(n, DN_HEAD_DIM)] * 4 + [seq_blk(n, n), seq_blk(SUBLANES, LANES)],
        out_specs=pl.BlockSpec((batch, DN_CHUNK, w), lambda i: (0, i, 0)),
        out_shape=jax.ShapeDtypeStruct((batch, seq, w), F32),
        scratch_shapes=[pltpu.VMEM((batch, DN_HEADS, DN_HEAD_DIM, DN_HEAD_DIM), F32)],
        compiler_params=_params("arbitrary"),
        name="dn_scan",
    )(by_batch(u), by_batch(wm), by_batch(qd), by_batch(kd), by_batch(intra), by_batch(gl))
    return o.reshape(t, w)


def _mixer_out_body(x_ref, yc_ref, o_ref, z_ref, dng_ref, wa_ref, wb_ref, h_ref):
    o = o_ref[...]
    z = z_ref[...]
    gain = dng_ref[...]
    parts = []
    for h in range(DN_HEADS):
        sl = slice(h * DN_HEAD_DIM, (h + 1) * DN_HEAD_DIM)
        zh = z[:, sl]
        parts.append(_rms(o[:, sl], gain) * (zh * jax.nn.sigmoid(zh)))
    y_dn = jnp.concatenate(parts, axis=-1)
    h_ref[...] = (x_ref[...] + _bdot(yc_ref[...], wa_ref[...]) + _bdot(y_dn, wb_ref[...]))


def _mixer_out(x, y_conv, o_dn, proj, dn_norm, w_out, tm):
    t, d = x.shape
    cw = y_conv.shape[1]
    tok = lambda width, col=0: pl.BlockSpec((tm, width), lambda i, col=col: (i, col))
    full = lambda shape: pl.BlockSpec(shape, lambda i: (0, 0))
    return pl.pallas_call(
        _mixer_out_body,
        grid=(t // tm,),
        in_specs=[tok(d), tok(cw), tok(cw), tok(cw, 6), full((1, DN_HEAD_DIM)),
                  full((cw, d)), full((cw, d))],
        out_specs=tok(d),
        out_shape=jax.ShapeDtypeStruct((t, d), F32),
        compiler_params=_params("parallel"),
        name="mixer_out",
    )(x, y_conv, o_dn, proj, dn_norm.reshape(1, DN_HEAD_DIM),
      w_out[:cw].astype(BF16), w_out[cw:].astype(BF16))


def _cross_attn_body(h_ref, g_ref, wq_ref, kv_ref, wo_ref, o_ref):
    h_in = h_ref[...]
    d = h_in.shape[1]
    dh = d // X_HEADS
    q = _bdot(_rms(h_in, g_ref[...]), wq_ref[...])
    outs = []
    for hd in range(X_HEADS):
        qh = q[:, hd * dh:(hd + 1) * dh].astype(BF16)
        kh = kv_ref[:, hd * dh:(hd + 1) * dh]
        vh = kv_ref[:, d + hd * dh:d + (hd + 1) * dh]
        s = lax.dot_general(qh, kh, (((1,), (1,)), ((), ())),
                            preferred_element_type=F32) * (dh ** -0.5)
        s = s - jnp.max(s, axis=-1, keepdims=True)
        e = jnp.exp(s)
        p = e / jnp.sum(e, axis=-1, keepdims=True)
        outs.append(jnp.dot(p.astype(BF16), vh, preferred_element_type=F32))
    o = jnp.concatenate(outs, axis=-1)
    o_ref[...] = h_in + _bdot(o, wo_ref[...])


def _cross_attn(h, gain, w_q, kv, w_o, batch, seq, mem_len, tm):
    t, d = h.shape
    nblk = seq // tm
    tok = pl.BlockSpec((tm, d), lambda b, i: (b * nblk + i, 0))
    full = lambda shape: pl.BlockSpec(shape, lambda b, i: (0, 0))
    return pl.pallas_call(
        _cross_attn_body,
        grid=(batch, nblk),
        in_specs=[tok, full((1, d)), full((d, d)),
                  pl.BlockSpec((mem_len, 2 * d), lambda b, i: (b, 0)), full((d, d))],
        out_specs=tok,
        out_shape=jax.ShapeDtypeStruct((t, d), F32),
        compiler_params=_params("parallel", "parallel"),
        name="cross_attn",
    )(h, gain.reshape(1, d), w_q.astype(BF16), kv.astype(BF16), w_o.astype(BF16))


def _top16_rows(s, payload=None):
    n = s.shape[0]
    row = lax.broadcasted_iota(jnp.int32, s.shape, 0).astype(F32)
    vals, pays = [], []
    for _ in range(PEER_TOPK):
        m = jnp.max(s, axis=0, keepdims=True)
        pos = jnp.min(jnp.where(s == m, row, float(n)), axis=0, keepdims=True)
        hit = row == pos
        vals.append(m)
        if payload is None:
            pays.append(pos)
        else:
            pays.append(jnp.sum(jnp.where(hit, payload, 0.0), axis=0, keepdims=True))
        s = jnp.where(hit, -jnp.inf, s)
    return jnp.concatenate(vals, axis=0), jnp.concatenate(pays, axis=0)


def _pair_candidates(v1, v2, pad):
    sub = lax.broadcasted_iota(jnp.int32, (SUBLANES, v1.shape[1]), 0)
    row = lambda v, a: jnp.broadcast_to(v[a:a + 1], sub.shape)
    lo8 = v2[:SUBLANES]
    blocks = [
        row(v1, 0) + lo8,
        row(v1, 0) + v2[SUBLANES:],
        row(v1, 1) + lo8,
        jnp.where(sub < 5, row(v1, 2) + lo8, pad),
        jnp.where(sub < 7,
                  jnp.where(sub < 4, row(v1, 3), row(v1, 4)) +
                  jnp.where(sub < 4, lo8, pltpu.roll(lo8, 4, 0)), pad),
        jnp.where(sub < 6,
                  jnp.where(sub < 2, row(v1, 5), jnp.where(sub < 4, row(v1, 6), row(v1, 7))) +
                  jnp.where((sub & 1) == 0, row(v2, 0), row(v2, 1)), pad),
        v1[SUBLANES:] + row(v2, 0),
    ]
    return jnp.concatenate(blocks, axis=0)


def _peer_route_body(h_ref, g_ref, wq_ref, keys_ref, xn_ref, gate_ref, *rest):
    idx_refs = rest[:PEER_HEADS]
    sc_ref, sel_ref, gt_ref, it_ref = rest[PEER_HEADS:]
    xn = _rms(h_ref[...], g_ref[...])
    for r in range(SUBLANES):
        xn_ref[:, r] = xn[:, r * LANES:(r + 1) * LANES].reshape(-1, SUBLANES, LANES)
    pq = _bdot(xn, wq_ref[...])
    ngrp = pq.shape[0] // LANES
    for j in range(2 * PEER_HEADS):
        qs = pq[:, j * PEER_HALF:(j + 1) * PEER_HALF].astype(BF16)
        sc = lax.dot_general(keys_ref[j], qs, (((1,), (1,)), ((), ())),
                             preferred_element_type=F32)
        for g in range(ngrp):
            sc_ref[j, g] = sc[:, g * LANES:(g + 1) * LANES]

    def head(h, carry):
        for g in range(ngrp):
            s1, i1 = _top16_rows(sc_ref[2 * h, g])
            s2, i2 = _top16_rows(sc_ref[2 * h + 1, g])
            cand = _pair_candidates(s1, s2, -jnp.inf)
            cand_idx = _pair_candidates(i1 * float(PEER_KEYS), i2, 0.0)
            best, sel = _top16_rows(cand, cand_idx)
            e = jnp.exp(best - jnp.max(best, axis=0, keepdims=True))
            sel_ref[g, h] = sel
            gt_ref[g, h] = e / jnp.sum(e, axis=0, keepdims=True)
        return carry

    lax.fori_loop(0, PEER_HEADS, head, 0)

    nsel = PEER_HEADS * PEER_TOPK
    per_row = LANES // PEER_TOPK
    lane_blk = lax.broadcasted_iota(jnp.int32, (LANES // per_row, LANES), 1) // PEER_TOPK
    for g in range(ngrp):
        rows = slice(g * LANES, (g + 1) * LANES)
        gate_ref[rows, :] = gt_ref[g].reshape(nsel, LANES).T
        it_ref[...] = (sel_ref[g].reshape(nsel, LANES).T * float(PACK_ROWS)).astype(jnp.int32)
        by_k = [it_ref[pl.ds(k, LANES // per_row, stride=per_row), :] for k in range(per_row)]
        out_rows = slice(g * LANES // per_row, (g + 1) * LANES // per_row)
        for hd in range(PEER_HEADS):
            out = jnp.zeros_like(by_k[0])
            for k in range(per_row):
                shift = ((k - hd) * PEER_TOPK) % LANES
                moved = pltpu.roll(by_k[k], shift, 1) if shift else by_k[k]
                out = jnp.where(lane_blk == k, moved, out)
            idx_refs[hd][out_rows, :] = out


def _peer_route(h, gain, w_pq, peer_keys, tm):
    t, d = h.shape
    nk = PEER_HEADS * 2
    ngrp = tm // LANES
    tok = lambda width: pl.BlockSpec((tm, width), lambda i: (i, 0))
    full = lambda shape: pl.BlockSpec(shape, lambda i: (0,) * len(shape))
    idx_rows = tm * PEER_TOPK // LANES
    outs = pl.pallas_call(
        _peer_route_body,
        grid=(t // tm,),
        in_specs=[tok(d), full((1, d)), full((d, d)), full((nk, PEER_KEYS, PEER_HALF))],
        out_specs=[pl.BlockSpec((tm // SUBLANES, SUBLANES, SUBLANES, LANES), lambda i: (i, 0, 0, 0)),
                   tok(LANES)] + [pl.BlockSpec((idx_rows, LANES), lambda i: (i, 0))] * PEER_HEADS,
        out_shape=[jax.ShapeDtypeStruct((t // SUBLANES, SUBLANES, SUBLANES, LANES), F32),
                   jax.ShapeDtypeStruct((t, LANES), F32)] +
                  [jax.ShapeDtypeStruct((t * PEER_TOPK // LANES, LANES), jnp.int32)] * PEER_HEADS,
        scratch_shapes=[pltpu.VMEM((nk, ngrp, PEER_KEYS, LANES), F32),
                        pltpu.VMEM((ngrp, PEER_HEADS, PEER_TOPK, LANES), F32),
                        pltpu.VMEM((ngrp, PEER_HEADS, PEER_TOPK, LANES), F32),
                        pltpu.VMEM((LANES, LANES), jnp.int32)],
        compiler_params=_params("parallel"),
        name="peer_route",
    )(h, gain.reshape(1, d), w_pq.astype(BF16),
      peer_keys.reshape(nk, PEER_KEYS, PEER_HALF).astype(BF16))
    xr, gates = outs[0], outs[1]
    return xr, tuple(o.reshape(-1) for o in outs[2:]), gates


PACK_ROWS = 4


def _pack_table_body(t_ref, o_ref):
    rows, d = t_ref.shape
    half = d // 2
    bits = lambda x: pltpu.bitcast(x.astype(BF16).astype(F32), jnp.uint32)
    word = (bits(t_ref[:, half:]) & jnp.uint32(0xFFFF0000)) | (bits(t_ref[:, :half]) >> 16)
    for j in range(PACK_ROWS):
        o_ref[pl.ds(j, rows, stride=PACK_ROWS), :] = word[:, j * LANES:(j + 1) * LANES]


def _pack_table(tbl):
    e, d = tbl.shape
    rows = _pick_tile(e, 512)
    return pl.pallas_call(
        _pack_table_body,
        grid=(e // rows,),
        in_specs=[pl.BlockSpec((rows, d), lambda i: (i, 0))],
        out_specs=pl.BlockSpec((rows * PACK_ROWS, LANES), lambda i: (i, 0)),
        out_shape=jax.ShapeDtypeStruct((e * PACK_ROWS, LANES), jnp.uint32),
        compiler_params=_params("parallel"),
        name="pack_table",
    )(tbl)


def _unpack(slab):
    lo = pltpu.bitcast(slab << 16, F32)
    hi = pltpu.bitcast(slab & jnp.uint32(0xFFFF0000), F32)
    return lo, hi


STAGE_STEPS = 4
PEER_SEL = PEER_HEADS * PEER_TOPK
IDX_REFS = PEER_HEADS


def _idx_specs(tb):
    n = tb * PEER_SEL // IDX_REFS
    return [pl.BlockSpec((n,), lambda i: (i,), memory_space=pltpu.SMEM) for _ in range(IDX_REFS)]


def _stage_rows(idx_refs, t, tbl_ref, g_ref, step=None):
    ncol = PEER_SEL // IDX_REFS
    per = ncol // STAGE_STEPS
    qs = range(ncol) if step is None else range(step * per, (step + 1) * per)
    for q in qs:
        off = t * ncol + q
        for r in range(IDX_REFS):
            m = r * ncol + q
            row = pl.multiple_of(idx_refs[r][off], PACK_ROWS)
            g_ref[m * PACK_ROWS:(m + 1) * PACK_ROWS, :] = tbl_ref[pl.ds(row, PACK_ROWS), :]


def _bf16_pieces(x, n):
    pieces = []
    for _ in range(n - 1):
        p = x.astype(BF16).astype(F32)
        pieces.append(p)
        x = x - p
    pieces.append(x.astype(BF16).astype(F32))
    return pieces


def _staged_group(idx_ref, tbl_ref, bufs, tb, base, step_fn):
    for k in range(SUBLANES):
        t = base + k
        nxt = t + 1 if k + 1 < SUBLANES else jnp.minimum(t + 1, tb - 1)
        for step in range(STAGE_STEPS):
            _stage_rows(idx_ref, nxt, tbl_ref, bufs[(k + 1) % 2], step)
            step_fn(t, k, step, bufs[k % 2])


def _peer_dot_body(*refs):
    idx_ref = refs[:IDX_REFS]
    x_ref, tbl_ref, act_ref, g0_ref, g1_ref = refs[IDX_REFS:]
    tb, nsel = act_ref.shape
    nslab = SUBLANES * SUBLANES
    lane = lax.broadcasted_iota(jnp.int32, (LANES, LANES), 1)
    col_slab = lane & (nslab - 1)
    _stage_rows(idx_ref, 0, tbl_ref, g0_ref)

    def group(i, carry):
        base = pl.multiple_of(i * SUBLANES, SUBLANES)
        pieces = _bf16_pieces(x_ref[i].reshape(nslab, LANES), LANES // nslab)
        xt = jnp.concatenate(pieces, axis=0).T.astype(BF16)
        zero = jnp.zeros_like(xt)
        acc = [jnp.zeros((nsel, LANES), F32)]

        def step_fn(t, k, j, g_ref):
            lo, hi = _unpack(g_ref[pl.ds(j, nsel, stride=PACK_ROWS), :])
            lhs = jnp.concatenate([lo.astype(BF16), hi.astype(BF16)], axis=1)
            rhs = jnp.concatenate(
                [jnp.where(col_slab == SUBLANES * r + k, xt, zero) for r in (j, PACK_ROWS + j)],
                axis=0)
            acc[0] = acc[0] + jnp.dot(lhs, rhs, preferred_element_type=F32)

        _staged_group(idx_ref, tbl_ref, (g0_ref, g1_ref), tb, base, step_fn)
        out_t = acc[0].T
        act_ref[pl.ds(base, SUBLANES), :] = jnp.sum(
            out_t.reshape(LANES // SUBLANES, SUBLANES, nsel), axis=0)
        return carry

    lax.fori_loop(0, tb // SUBLANES, group, 0)


def _peer_dot(idx_split, xr, tbl, tb):
    t = xr.shape[0] * SUBLANES
    nsel = PEER_SEL
    gbuf = pltpu.VMEM((nsel * PACK_ROWS, LANES), jnp.uint32)
    return pl.pallas_call(
        _peer_dot_body,
        grid=(t // tb,),
        in_specs=_idx_specs(tb) + [
            pl.BlockSpec((tb // SUBLANES, SUBLANES, SUBLANES, LANES), lambda i: (i, 0, 0, 0)),
            pl.BlockSpec(tbl.shape, lambda i: (0, 0), pipeline_mode=pl.Buffered(1))],
        out_specs=pl.BlockSpec((tb, nsel), lambda i: (i, 0)),
        out_shape=jax.ShapeDtypeStruct((t, nsel), F32),
        scratch_shapes=[gbuf, gbuf],
        compiler_params=_params("arbitrary"),
        name="peer_dot",
    )(*idx_split, xr, tbl)


def _peer_weights_body(act_ref, gate_ref, rep_ref, w_ref):
    a = act_ref[...]
    w = gate_ref[...] * (0.5 * a * (1.0 + lax.erf(a * (2.0 ** -0.5))))
    rep = rep_ref[...]
    w_ref[...] = sum(jnp.dot(p.astype(BF16), rep, preferred_element_type=F32)
                     for p in _bf16_pieces(w, 3))


def _peer_weights(act, gates, tm):
    t, n = act.shape
    nrow = 2 * PACK_ROWS * n
    rep = (jnp.arange(n)[:, None] == (jnp.arange(nrow) // (2 * PACK_ROWS))[None, :]).astype(BF16)
    tok = lambda width: pl.BlockSpec((tm, width), lambda i: (i, 0))
    return pl.pallas_call(
        _peer_weights_body,
        grid=(t // tm,),
        in_specs=[tok(n), tok(n), pl.BlockSpec(rep.shape, lambda i: (0, 0))],
        out_specs=tok(nrow),
        out_shape=jax.ShapeDtypeStruct((t, nrow), F32),
        compiler_params=_params("parallel"),
        name="peer_weights",
    )(act, gates, rep)


def _peer_sum_body(has_gain, *refs):
    idx_ref = refs[:IDX_REFS]
    w_ref, h_ref, mask_ref = refs[IDX_REFS:IDX_REFS + 3]
    gain_ref = refs[IDX_REFS + 3] if has_gain else None
    tbl_ref, y_ref, g0_ref, g1_ref = refs[IDX_REFS + 3 + has_gain:]
    tb = y_ref.shape[0]
    mask = mask_ref[...]
    nrow = g0_ref.shape[0] // STAGE_STEPS
    ncol = mask.shape[1] // STAGE_STEPS
    _stage_rows(idx_ref, 0, tbl_ref, g0_ref)

    def finish(i, tiles):
        base = pl.multiple_of(i * SUBLANES, SUBLANES)
        slabs = []
        for r in range(SUBLANES):
            y_r = jnp.concatenate([tile[r:r + 1] for tile in tiles], axis=0)
            slabs.append(h_ref[pl.ds(base, SUBLANES), r * LANES:(r + 1) * LANES] + y_r)
        if gain_ref is not None:
            ms = sum(jnp.sum(s * s, axis=1, keepdims=True) for s in slabs)
            inv = lax.rsqrt(ms * (1.0 / (SUBLANES * LANES)) + EPS)
            slabs = [s * inv * gain_ref[:, r * LANES:(r + 1) * LANES]
                     for r, s in enumerate(slabs)]
        for r, s in enumerate(slabs):
            y_ref[pl.ds(base, SUBLANES), r * LANES:(r + 1) * LANES] = s

    def group(i, prev_rows):
        finish(jnp.maximum(i - 1, 0), prev_rows)
        base = pl.multiple_of(i * SUBLANES, SUBLANES)
        w_rep = w_ref[pl.ds(base, SUBLANES), :]
        acc = [None]

        def step_fn(t, k, step, g_ref):
            cols = slice(step * ncol, (step + 1) * ncol)
            wb = jnp.broadcast_to(w_rep[k:k + 1, cols], (SUBLANES, ncol)) * mask[:, cols]
            a_hi = wb.astype(BF16)
            a_lo = (wb - a_hi.astype(F32)).astype(BF16)
            lhs = jnp.concatenate([a_hi, a_lo], axis=0)
            staged = pltpu.bitcast(g_ref[step * nrow:(step + 1) * nrow, :], BF16)
            part = jnp.dot(lhs, staged, preferred_element_type=F32)
            acc[0] = part if step == 0 else acc[0] + part
            if step == STAGE_STEPS - 1:
                rows.append(acc[0][:SUBLANES] + acc[0][SUBLANES:])

        rows = []
        _staged_group(idx_ref, tbl_ref, (g0_ref, g1_ref), tb, base, step_fn)
        return tuple(rows)

    ngroup = tb // SUBLANES
    zeros = tuple(jnp.zeros((SUBLANES, LANES), F32) for _ in range(SUBLANES))
    finish(ngroup - 1, lax.fori_loop(0, ngroup, group, zeros))


def _peer_sum(idx_split, w, h, gain, tbl, tb):
    t, nrow = w.shape
    d = h.shape[1]
    nsel = PEER_SEL
    tile = pl.BlockSpec((tb, d), lambda i: (i, 0))
    extra = [] if gain is None else [gain.reshape(1, d)]
    c = jnp.arange(nrow)
    out_row = (c % 2) * PACK_ROWS + (c % (2 * PACK_ROWS)) // 2
    mask = (jnp.arange(SUBLANES)[:, None] == out_row[None, :]).astype(F32)
    gbuf = pltpu.VMEM((nsel * PACK_ROWS, LANES), jnp.uint32)
    const = lambda shape: pl.BlockSpec(shape, lambda i: (0, 0))
    return pl.pallas_call(
        functools.partial(_peer_sum_body, len(extra)),
        grid=(t // tb,),
        in_specs=_idx_specs(tb) + [
            pl.BlockSpec((tb, nrow), lambda i: (i, 0)),
            tile,
            const(mask.shape)] + [const((1, d))] * len(extra) + [
            pl.BlockSpec(tbl.shape, lambda i: (0, 0), pipeline_mode=pl.Buffered(1))],
        out_specs=tile,
        out_shape=jax.ShapeDtypeStruct((t, d), F32),
        scratch_shapes=[gbuf, gbuf],
        compiler_params=_params("arbitrary"),
        name="peer_sum",
    )(*idx_split, w, h, mask, *extra, tbl)


def _pick_tile(n, pref):
    tile = min(n, pref)
    while n % tile:
        tile //= 2
    return tile


def _layer(h, mem, batch, seq, norm_mix, w_in, conv_w, dn_conv_w, dn_a_log, dn_dt_bias, dn_norm,
           w_out, norm_x, norm_mem, w_xq, w_xkv, w_xo, norm_ffn, w_pq, peer_keys,
           expert_u, expert_v, final_gain):
    t, d = h.shape
    mem_len = mem.shape[0] // batch
    in_cols = w_in.shape[1]
    pad = (-in_cols) % LANES
    w_in_p = jnp.pad(w_in, ((0, 0), (0, pad))).astype(BF16)

    proj = _norm_matmul(h, norm_mix, w_in_p, _pick_tile(t, 256), "in_proj")
    y_conv, q, k, v, gb = _mixer_pre(proj, conv_w, dn_conv_w, dn_a_log, dn_dt_bias,
                                     batch, seq, _pick_tile(seq, 256))
    o_dn = _delta_rule(q, k, v, gb, batch, seq)
    h = _mixer_out(h, y_conv, o_dn, proj, dn_norm, w_out, _pick_tile(t, 512))

    kv = _norm_matmul(mem, norm_mem, w_xkv.astype(BF16), _pick_tile(mem.shape[0], 256), "kv_proj")
    h = _cross_attn(h, norm_x, w_xq, kv, w_xo, batch, seq, mem_len, _pick_tile(seq, 512))

    xr, idx_split, gates = _peer_route(h, norm_ffn, w_pq, peer_keys, _pick_tile(t, 512))
    tb = _pick_tile(t, 512)
    act = _peer_dot(idx_split, xr, _pack_table(expert_u), tb)
    w = _peer_weights(act, gates, _pick_tile(t, 1024))
    return _peer_sum(idx_split, w, h, final_gain, _pack_table(expert_v), tb)


def kernel(x, mem, norm_mix, w_in, conv_w, dn_conv_w, dn_a_log, dn_dt_bias, dn_norm, w_out,
           norm_x, norm_mem, w_xq, w_xkv, w_xo, norm_ffn, w_pq, peer_keys, expert_u, expert_v,
           norm_final):
    batch, seq, d = x.shape
    depth = norm_mix.shape[0]
    h = x.reshape(batch * seq, d)
    mem2 = mem.reshape(batch * mem.shape[1], d)
    for layer in range(depth):
        h = _layer(h, mem2, batch, seq, norm_mix[layer], w_in[layer], conv_w[layer],
                   dn_conv_w[layer], dn_a_log[layer], dn_dt_bias[layer], dn_norm[layer],
                   w_out[layer], norm_x[layer], norm_mem[layer], w_xq[layer], w_xkv[layer],
                   w_xo[layer], norm_ffn[layer], w_pq[layer], peer_keys[layer],
                   expert_u[layer], expert_v[layer],
                   norm_final if layer == depth - 1 else None)
    return h.reshape(batch, seq, d)
```
